```python
import math
import jax, jax.numpy as jnp
from jax import lax
import numpy as np

D_MODEL = 1024
BATCH = 32
SEQ = 256
DEPTH = 1
DEC_BATCH = 4
DEC_SEQ = 4096
PAST_LEN = 512

GRID_W = 64
POOL_WIDTH = 512
POOL_GROUPS = 4
POOL_GROUP_DIM = POOL_WIDTH // POOL_GROUPS
POOL_WINDOWS = (2, 4, 8, 16)
N_RET_HEADS = 4
RET_DK = 128
RET_DV = 256
RET_QK_WIDTH = N_RET_HEADS * RET_DK
RET_V_WIDTH = N_RET_HEADS * RET_DV
IN_WIDTH = POOL_WIDTH + 2 * RET_QK_WIDTH + 2 * RET_V_WIDTH
RET_CHUNK = 128
N_BRANCHES = 2
N_EXPERTS = 16
EXPERT_FF = 2816
EC_CAPACITY_FACTOR = 2
ROPE_BASE = 10000.0
EPS = 1e-6

kernel_name = "hybrid_pool_retention_ec_moe_diffusion_step"


def rms_norm(x, g):
    xf = x.astype(jnp.float32)
    y = xf * lax.rsqrt(jnp.mean(xf * xf, axis=-1, keepdims=True) + EPS)
    return (y * g.astype(jnp.float32)).astype(x.dtype)


def adaln_params(cond, w, b):
    return jnp.split(jax.nn.silu(cond) @ w + b, 6, axis=-1)


def rope_tables(L):
    rows = L // GRID_W
    row = jnp.repeat(jnp.arange(rows, dtype=jnp.float32), GRID_W)
    col = jnp.tile(jnp.arange(GRID_W, dtype=jnp.float32), rows)
    nf = RET_DK // 4
    inv = ROPE_BASE ** (-jnp.arange(nf, dtype=jnp.float32) / nf)
    ang = jnp.concatenate([row[:, None] * inv, col[:, None] * inv], axis=-1)
    return jnp.cos(ang), jnp.sin(ang)


def apply_rope(x, cos, sin):
    xe, xo = x[..., 0::2], x[..., 1::2]
    return jnp.stack([xe * cos - xo * sin, xe * sin + xo * cos], axis=-1).reshape(x.shape)


def multiscale_pool(u, w_group, scale):
    B, L, _ = u.shape
    uf = u.astype(jnp.float32)
    cs = jnp.concatenate([jnp.zeros((B, 1, POOL_WIDTH), jnp.float32), jnp.cumsum(uf, axis=1)], axis=1)
    t = jnp.arange(L)
    pooled = []
    for gi, w in enumerate(POOL_WINDOWS):
        lo = jnp.clip(t - w // 2, 0, L)
        hi = jnp.clip(t + w // 2, 0, L)
        cnt = (hi - lo).astype(jnp.float32)[:, None]
        sl = slice(gi * POOL_GROUP_DIM, (gi + 1) * POOL_GROUP_DIM)
        csg = cs[:, :, sl]
        pooled.append((csg[:, hi] - csg[:, lo]) / cnt - uf[:, :, sl])
    p = jnp.stack(pooled, axis=2).astype(u.dtype)
    y = jnp.einsum('blgc,gcd->blgd', p, w_group).reshape(B, L, POOL_WIDTH)
    return y * scale


def retention_direction(q, k, v, log_gamma, s0, strict):
    B, H, L, _ = q.shape
    n = L // RET_CHUNK
    qc = q.reshape(B, H, n, RET_CHUNK, RET_DK)
    kc = k.reshape(B, H, n, RET_CHUNK, RET_DK)
    vc = v.reshape(B, H, n, RET_CHUNK, RET_DV)
    i = jnp.arange(RET_CHUNK, dtype=jnp.float32)
    diff = i[:, None] - i[None, :]
    valid = (diff > 0) if strict else (diff >= 0)
    intra_decay = jnp.where(valid, jnp.exp(log_gamma[:, None, None] * jnp.maximum(diff, 0.0)), 0.0)
    q_decay = jnp.exp(log_gamma[:, None] * (i + 1.0))
    k_decay = jnp.exp(log_gamma[:, None] * (RET_CHUNK - 1.0 - i))
    chunk_decay = jnp.exp(log_gamma * RET_CHUNK)[None, :, None, None]
    scores = jnp.einsum('bhncd,bhnmd->bhncm', qc, kc) * intra_decay[None, :, None]
    intra = jnp.einsum('bhncm,bhnmv->bhncv', scores, vc)
    kv = jnp.einsum('bhnmd,bhnmv->nbhdv', kc * k_decay[None, :, None, :, None], vc)

    def step(s, kv_n):
        return chunk_decay * s + kv_n, s

    s_final, s_start = lax.scan(step, s0, kv)
    inter = jnp.einsum('bhncd,nbhdv->bhncv', qc * q_decay[None, :, None, :, None], s_start)
    return (intra + inter).reshape(B, H, L, RET_DV), s_final


def retention_branch(z, decay_param, gn_gain, s0_f, s0_b, rope):
    B, L, _ = z.shape
    q, k, v, g = jnp.split(z, [RET_QK_WIDTH, 2 * RET_QK_WIDTH, 2 * RET_QK_WIDTH + RET_V_WIDTH], axis=-1)

    def heads(t, d):
        return t.astype(jnp.float32).reshape(B, L, N_RET_HEADS, d).transpose(0, 2, 1, 3)

    q = heads(q, RET_DK)
    k = heads(k, RET_DK) * (RET_DK ** -0.5)
    v = heads(v, RET_DV)
    if rope is not None:
        cos, sin = rope
        q = apply_rope(q, cos, sin)
        k = apply_rope(k, cos, sin)
    log_gamma = -jnp.exp(decay_param.astype(jnp.float32))
    o_f, s_f = retention_direction(q, k, v, log_gamma[0], s0_f.astype(jnp.float32), False)
    o_b, s_b = retention_direction(jnp.flip(q, 2), jnp.flip(k, 2), jnp.flip(v, 2), log_gamma[1],
                                   s0_b.astype(jnp.float32), True)
    o = o_f + jnp.flip(o_b, 2)
    mu = jnp.mean(o, axis=-1, keepdims=True)
    var = jnp.mean(jnp.square(o - mu), axis=-1, keepdims=True)
    o = (o - mu) * lax.rsqrt(var + EPS)
    o = o.transpose(0, 2, 1, 3).reshape(B, L, RET_V_WIDTH) * gn_gain.astype(jnp.float32)
    return (jax.nn.silu(g.astype(jnp.float32)) * o).astype(z.dtype), s_f, s_b


def expert_choice_moe(h, w_router, w_gate, w_up, w_down):
    B, L, D = h.shape
    cap = max(1, EC_CAPACITY_FACTOR * L // N_EXPERTS)
    aff = jax.nn.softmax(jnp.einsum('bld,de->ble', h, w_router).astype(jnp.float32), axis=-1)
    gate, idx = lax.top_k(jnp.swapaxes(aff, 1, 2), cap)
    xs = jax.vmap(lambda hb, ib: hb[ib])(h, idx)
    hid = jax.nn.silu(jnp.einsum('becd,edf->becf', xs, w_gate)) * jnp.einsum('becd,edf->becf', xs, w_up)
    out = jnp.einsum('becf,efd->becd', hid, w_down) * gate[..., None].astype(h.dtype)
    return jax.vmap(lambda ob, ib: jnp.zeros((L, D), ob.dtype).at[ib.reshape(-1)].add(ob.reshape(-1, D)))(out, idx)


def trunk_layer(x, cond, s0_f, s0_b, rope, p):
    shift1, scale1, gate1, shift2, scale2, gate2 = adaln_params(cond, p['w_ada'], p['b_ada'])
    h = rms_norm(x, p['norm1_g']) * (1.0 + scale1[:, None]) + shift1[:, None]
    z = jnp.einsum('bld,de->ble', h, p['w_in'])
    pool_y = multiscale_pool(z[..., :POOL_WIDTH], p['w_pool'], p['pool_scale']) @ p['w_pool_out']
    ret_h, s_f, s_b = retention_branch(z[..., POOL_WIDTH:], p['ret_decay'], p['ret_gn_g'], s0_f, s0_b, rope)
    ret_y = ret_h @ p['w_ret_out']
    gates = jax.nn.sigmoid(h @ p['w_branch_gate'] + p['b_branch_gate'])
    merged = gates[..., :D_MODEL] * pool_y + gates[..., D_MODEL:] * ret_y
    x = x + gate1[:, None] * (merged @ p['w_out'])
    h2 = rms_norm(x, p['norm2_g']) * (1.0 + scale2[:, None]) + shift2[:, None]
    x = x + gate2[:, None] * expert_choice_moe(h2, p['w_router'], p['w_exp_gate'], p['w_exp_up'], p['w_exp_down'])
    return x, s_f, s_b


def setup_inputs(seed: int = 0) -> dict:
    key = jax.random.key(seed)
    ks = jax.random.split(key, 26)
    f32 = jnp.float32
    nrm = lambda k, shape, s: jax.random.normal(k, shape, f32) * s
    gamma0 = 1.0 - 2.0 ** (-5.0 - np.arange(N_RET_HEADS, dtype=np.float32))
    decay_init = jnp.asarray(np.log(-np.log(gamma0)).astype(np.float32))
    return {
        "x_prompt": nrm(ks[0], (BATCH, SEQ, D_MODEL), 1.0),
        "x_sample": nrm(ks[1], (DEC_BATCH, DEC_SEQ, D_MODEL), 1.0),
        "c": nrm(ks[2], (DEC_BATCH, D_MODEL), 1.0),
        "state_ret": nrm(ks[3], (DEC_BATCH, DEPTH, 2, N_RET_HEADS, RET_DK, RET_DV), 1.0),
        "c_ctx": nrm(ks[4], (D_MODEL,), 1.0),
        "w_ada": nrm(ks[5], (DEPTH, D_MODEL, 6 * D_MODEL), 0.5 * D_MODEL ** -0.5),
        "b_ada": nrm(ks[6], (DEPTH, 6 * D_MODEL), 0.02),
        "norm1_g": 1.0 + nrm(ks[7], (DEPTH, D_MODEL), 0.02),
        "w_in": nrm(ks[8], (DEPTH, D_MODEL, IN_WIDTH), D_MODEL ** -0.5),
        "w_pool": nrm(ks[9], (DEPTH, POOL_GROUPS, POOL_GROUP_DIM, POOL_GROUP_DIM), POOL_GROUP_DIM ** -0.5),
        "pool_scale": 1.0 + nrm(ks[10], (DEPTH, POOL_WIDTH), 0.02),
        "w_pool_out": nrm(ks[11], (DEPTH, POOL_WIDTH, D_MODEL), POOL_WIDTH ** -0.5),
        "ret_decay": decay_init[None, None, :] + nrm(ks[12], (DEPTH, 2, N_RET_HEADS), 0.05),
        "ret_gn_g": 1.0 + nrm(ks[13], (DEPTH, RET_V_WIDTH), 0.02),
        "w_ret_out": nrm(ks[14], (DEPTH, RET_V_WIDTH, D_MODEL), RET_V_WIDTH ** -0.5),
        "w_branch_gate": nrm(ks[15], (DEPTH, D_MODEL, N_BRANCHES * D_MODEL), D_MODEL ** -0.5),
        "b_branch_gate": nrm(ks[16], (DEPTH, N_BRANCHES * D_MODEL), 0.02),
        "w_out": nrm(ks[17], (DEPTH, D_MODEL, D_MODEL), D_MODEL ** -0.5),
        "norm2_g": 1.0 + nrm(ks[18], (DEPTH, D_MODEL), 0.02),
        "w_router": nrm(ks[19], (DEPTH, D_MODEL, N_EXPERTS), D_MODEL ** -0.5),
        "w_exp_gate": nrm(ks[20], (DEPTH, N_EXPERTS, D_MODEL, EXPERT_FF), D_MODEL ** -0.5),
        "w_exp_up": nrm(ks[21], (DEPTH, N_EXPERTS, D_MODEL, EXPERT_FF), D_MODEL ** -0.5),
        "w_exp_down": nrm(ks[22], (DEPTH, N_EXPERTS, EXPERT_FF, D_MODEL), EXPERT_FF ** -0.5),
        "final_norm_g": 1.0 + nrm(ks[23], (D_MODEL,), 0.02),
    }


def reference(x_prompt, x_sample, c, state_ret, c_ctx, w_ada, b_ada, norm1_g, w_in, w_pool, pool_scale,
              w_pool_out, ret_decay, ret_gn_g, w_ret_out, w_branch_gate, b_branch_gate, w_out, norm2_g,
              w_router, w_exp_gate, w_exp_up, w_exp_down, final_norm_g):
    rope = rope_tables(x_sample.shape[1])
    zero_state = jnp.zeros((x_prompt.shape[0], N_RET_HEADS, RET_DK, RET_DV), jnp.float32)
    cond_ctx = c_ctx[None, :]
    xp, xs = x_prompt, x_sample
    layer_states = []
    for l in range(DEPTH):
        p = {
            'w_ada': w_ada[l], 'b_ada': b_ada[l], 'norm1_g': norm1_g[l], 'w_in': w_in[l],
            'w_pool': w_pool[l], 'pool_scale': pool_scale[l], 'w_pool_out': w_pool_out[l],
            'ret_decay': ret_decay[l], 'ret_gn_g': ret_gn_g[l], 'w_ret_out': w_ret_out[l],
            'w_branch_gate': w_branch_gate[l], 'b_branch_gate': b_branch_gate[l], 'w_out': w_out[l],
            'norm2_g': norm2_g[l], 'w_router': w_router[l], 'w_exp_gate': w_exp_gate[l],
            'w_exp_up': w_exp_up[l], 'w_exp_down': w_exp_down[l],
        }
        xp, s_f, s_b = trunk_layer(xp, cond_ctx, zero_state, zero_state, None, p)
        layer_states.append(jnp.stack([s_f, s_b], axis=1))
        xs, _, _ = trunk_layer(xs, c, state_ret[:, l, 0], state_ret[:, l, 1], rope, p)
    y_prompt = rms_norm(xp, final_norm_g)
    y_sample = rms_norm(xs, final_norm_g)
    new_state_ret = jnp.stack(layer_states, axis=1).astype(x_prompt.dtype)
    return (y_prompt, y_sample, new_state_ret)
```

```python
import functools

import jax
import jax.numpy as jnp
from jax import lax
from jax.experimental import pallas as pl
from jax.experimental.pallas import tpu as pltpu

F32 = jnp.float32
BF16 = jnp.bfloat16
HIGHEST = lax.Precision.HIGHEST

N_HEADS = 4
DK = 128
DV = 256
CHUNK = 128
POOL_WINDOWS = (2, 4, 8, 16)
POOL_GROUP = 128
POOL_WIDTH = POOL_GROUP * len(POOL_WINDOWS)
QK_WIDTH = N_HEADS * DK
V_WIDTH = N_HEADS * DV
N_EXPERTS = 16
CAPACITY_FACTOR = 2
GRID_W = 64
ROPE_BASE = 10000.0
EPS = 1e-6

STEP_ROWS = 512
POOL_ROWS = 256
HALO = 16
ROUTE_LANES = 256
GATHER_TOKENS = 512
GATHER_SLOTS = 128
COUNT_LANES = 128
FF_TILE = 256
FFN_ROWS = 512
VMEM_LIMIT = 56 * 1024 * 1024


def _cparams(sem):
    return pltpu.CompilerParams(dimension_semantics=sem, vmem_limit_bytes=VMEM_LIMIT)


def _resident(shape):
    return pl.BlockSpec(shape, lambda *_: (0,) * len(shape), pipeline_mode=pl.Buffered(1))


def _silu(x):
    return x * jax.nn.sigmoid(x)


def _norm_mod(x, g, scale, shift):
    y = x * lax.rsqrt(jnp.mean(x * x, axis=-1, keepdims=True) + EPS)
    return (y * g) * (1.0 + scale) + shift


def _dot(a, b):
    return jnp.dot(a, b, preferred_element_type=F32)


def _dot_nt(a, b, precision=None):
    return lax.dot_general(a, b, (((1,), (1,)), ((), ())), preferred_element_type=F32, precision=precision)


def _dot_tn(a, b):
    return lax.dot_general(a, b, (((0,), (0,)), ((), ())), preferred_element_type=F32)


def _adaln_kernel(c_ref, w_ref, b_ref, o_ref):
    o_ref[...] = jnp.dot(_silu(c_ref[...]), w_ref[...], preferred_element_type=F32, precision=HIGHEST) + b_ref[...]


def _adaln(cond, w, b):
    rows, d = cond.shape
    n = w.shape[1]
    tn = d
    return pl.pallas_call(
        _adaln_kernel,
        grid=(n // tn,),
        in_specs=[pl.BlockSpec((rows, d), lambda i: (0, 0)),
                  pl.BlockSpec((d, tn), lambda i: (0, i)),
                  pl.BlockSpec((1, tn), lambda i: (0, i))],
        out_specs=pl.BlockSpec((rows, tn), lambda i: (0, i)),
        out_shape=jax.ShapeDtypeStruct((rows, n), F32),
        compiler_params=_cparams(("arbitrary",)),
        name="adaln",
    )(cond, w, b.reshape(1, n))


def _decay_tiles(decay_ref, head):
    lgf = -jnp.exp(jnp.full((CHUNK, DK), decay_ref[0, head], F32))
    lgb = -jnp.exp(jnp.full((CHUNK, DK), decay_ref[1, head], F32))
    i = lax.broadcasted_iota(jnp.int32, (CHUNK, DK), 0).astype(F32)
    m = lax.broadcasted_iota(jnp.int32, (CHUNK, DK), 1).astype(F32)
    diff = i - m
    return dict(
        mask=jnp.where(diff >= 0.0, jnp.exp(lgf * jnp.maximum(diff, 0.0)), jnp.exp(lgb * jnp.maximum(-diff, 0.0))),
        qdf=jnp.exp(lgf * (i + 1.0)),
        qdb=jnp.exp(lgb * (CHUNK - i)),
        kdf=jnp.exp(lgf * (CHUNK - 1.0 - i)),
        kdb=jnp.exp(lgb * i),
        cdf=jnp.exp(lgf * float(CHUNK)),
        cdb=jnp.exp(lgb * float(CHUNK)),
    )


def _wide(t):
    return jnp.concatenate([t, t], axis=1)


def _rope(x, cos, sin):
    even = (lax.broadcasted_iota(jnp.int32, x.shape, 1) % 2) == 0
    partner = jnp.where(even, pltpu.roll(x, x.shape[1] - 1, 1), pltpu.roll(x, 1, 1))
    return x * cos + partner * sin


def _revscan_kernel(x_ref, mod_ref, cos_ref, sin_ref, s0_ref, decay_ref, n1_ref, w_in_ref, sb_ref, s_scr):
    j = pl.program_id(1)

    @pl.when(j == 0)
    def _():
        s_scr[...] = s0_ref[0]

    mod = mod_ref[0]
    h = _norm_mod(x_ref[0], n1_ref[...], mod[1:2], mod[0:1]).astype(BF16)
    k_lo = POOL_WIDTH + QK_WIDTH
    kv = _dot(h, w_in_ref[:, k_lo:k_lo + QK_WIDTH + V_WIDTH])
    sb_ref[0, 0] = s_scr[...]
    for head in range(N_HEADS):
        t = _decay_tiles(decay_ref, head)
        s = s_scr[head]
        for c in reversed(range(STEP_ROWS // CHUNK)):
            rows = slice(c * CHUNK, (c + 1) * CHUNK)
            k = _rope(kv[rows, head * DK:(head + 1) * DK] * (DK ** -0.5), cos_ref[rows], sin_ref[rows])
            v = kv[rows, QK_WIDTH + head * DV:QK_WIDTH + (head + 1) * DV]
            s = _wide(t["cdb"]) * s + _dot_tn((k * t["kdb"]).astype(BF16), v.astype(BF16))
        s_scr[head] = s


def _revscan(x, mod, cos, sin, s0_b, decay, n1g, w_in):
    b, l, d = x.shape
    nblk = l // STEP_ROWS
    return pl.pallas_call(
        _revscan_kernel,
        grid=(b, nblk),
        in_specs=[pl.BlockSpec((1, STEP_ROWS, d), lambda bi, j: (bi, nblk - 1 - j, 0)),
                  pl.BlockSpec((1, 8, d), lambda bi, j: (bi, 0, 0)),
                  pl.BlockSpec((STEP_ROWS, DK), lambda bi, j: (nblk - 1 - j, 0)),
                  pl.BlockSpec((STEP_ROWS, DK), lambda bi, j: (nblk - 1 - j, 0)),
                  pl.BlockSpec((1, N_HEADS, DK, DV), lambda bi, j: (bi, 0, 0, 0)),
                  pl.BlockSpec(memory_space=pltpu.SMEM),
                  _resident(n1g.shape),
                  _resident(w_in.shape)],
        out_specs=pl.BlockSpec((1, 1, N_HEADS, DK, DV), lambda bi, j: (bi, nblk - 1 - j, 0, 0, 0)),
        out_shape=jax.ShapeDtypeStruct((b, nblk, N_HEADS, DK, DV), F32),
        scratch_shapes=[pltpu.VMEM((N_HEADS, DK, DV), F32)],
        compiler_params=_cparams(("arbitrary", "arbitrary")),
        name="revscan",
    )(x, mod, cos, sin, s0_b, decay, n1g, w_in)


def _mix_kernel(*refs, seq_len, n_seq, seq_rows, use_rope, has_state, emit_state):
    it = iter(refs)
    x_ref, xp_ref, xn_ref, mod_ref = next(it), next(it), next(it), next(it)
    cos_ref = sin_ref = s0f_ref = sb_ref = None
    if use_rope:
        cos_ref, sin_ref = next(it), next(it)
    if has_state:
        s0f_ref, sb_ref = next(it), next(it)
    (decay_ref, n1_ref, w_in_ref, w_bg_ref, b_bg_ref, w_pool_ref, pscale_ref, w_po_ref, gn_ref, w_ro_ref,
     w_out_ref, n2_ref, w_rt_ref) = (next(it) for _ in range(13))
    x1_ref, h2_ref, aff_ref = next(it), next(it), next(it)
    st_ref = next(it) if emit_state else None
    sf_scr = next(it)

    j = pl.program_id(1)
    d = x_ref.shape[2]
    n_chunks = seq_rows // CHUNK
    ext = seq_rows + 2 * HALO

    @pl.when(j == 0)
    def _():
        if has_state:
            sf_scr[...] = s0f_ref[...]
        else:
            sf_scr[...] = jnp.zeros(sf_scr.shape, F32)

    mods = [mod_ref[s] for s in range(n_seq)]

    def rows_of(vals):
        return jnp.concatenate([jnp.broadcast_to(v, (seq_rows, d)) for v in vals], axis=0) if n_seq > 1 else vals[0]

    he_parts = []
    for s in range(n_seq):
        xe = jnp.concatenate([xp_ref[s], x_ref[s], xn_ref[s]], axis=0)
        he_parts.append(_norm_mod(xe, n1_ref[...], mods[s][1:2], mods[s][0:1]).astype(BF16))
    he = jnp.concatenate(he_parts, axis=0) if n_seq > 1 else he_parts[0]
    hb_parts = [hp[HALO:HALO + seq_rows] for hp in he_parts]
    hb = jnp.concatenate(hb_parts, axis=0) if n_seq > 1 else hb_parts[0]
    x = jnp.concatenate([x_ref[s] for s in range(n_seq)], axis=0) if n_seq > 1 else x_ref[0]

    ue_all = _dot(he, w_in_ref[:, :POOL_WIDTH])
    epos = j * seq_rows - HALO + lax.broadcasted_iota(jnp.int32, (ext, 1), 0)
    valid = (epos >= 0) & (epos < seq_len)
    band_rows = POOL_ROWS + 2 * HALO
    rel = (lax.broadcasted_iota(jnp.int32, (POOL_ROWS, band_rows), 1) - HALO
           - lax.broadcasted_iota(jnp.int32, (POOL_ROWS, band_rows), 0))
    pooled = [[] for _ in POOL_WINDOWS]
    for s in range(n_seq):
        ue = jnp.where(valid, ue_all[s * ext:(s + 1) * ext], 0.0)
        for p0 in range(0, seq_rows, POOL_ROWS):
            tpos = j * seq_rows + p0 + lax.broadcasted_iota(jnp.int32, (POOL_ROWS, 1), 0)
            for gi, w in enumerate(POOL_WINDOWS):
                band = ((rel >= -(w // 2)) & (rel < w // 2)).astype(BF16)
                ug = ue[p0:p0 + band_rows, gi * POOL_GROUP:(gi + 1) * POOL_GROUP]
                hi = ug.astype(BF16)
                lo = (ug - hi.astype(F32)).astype(BF16)
                wsum = _dot(band, hi) + _dot(band, lo)
                cnt = (jnp.minimum(tpos + w // 2, seq_len) - jnp.maximum(tpos - w // 2, 0)).astype(F32)
                pooled[gi].append((wsum / cnt - ug[HALO:HALO + POOL_ROWS]).astype(BF16))
    pool_h = jnp.concatenate(
        [_dot(jnp.concatenate(pg, axis=0) if len(pg) > 1 else pg[0], w_pool_ref[gi]) for gi, pg in enumerate(pooled)],
        axis=1) * pscale_ref[...]
    pool_y = _dot(pool_h.astype(BF16), w_po_ref[...])

    zr = _dot(hb, w_in_ref[:, POOL_WIDTH:])
    seq_out = []
    for s in range(n_seq):
        r0 = s * seq_rows
        head_out = []
        for head in range(N_HEADS):
            t = _decay_tiles(decay_ref, head)
            qs, ks, vs = [], [], []
            for c in range(n_chunks):
                rows = slice(r0 + c * CHUNK, r0 + (c + 1) * CHUNK)
                q = zr[rows, head * DK:(head + 1) * DK]
                k = zr[rows, QK_WIDTH + head * DK:QK_WIDTH + (head + 1) * DK] * (DK ** -0.5)
                if use_rope:
                    crow = slice(c * CHUNK, (c + 1) * CHUNK)
                    q = _rope(q, cos_ref[crow], sin_ref[crow])
                    k = _rope(k, cos_ref[crow], sin_ref[crow])
                qs.append(q)
                ks.append(k)
                vs.append(zr[rows, 2 * QK_WIDTH + head * DV:2 * QK_WIDTH + (head + 1) * DV].astype(BF16))
            sf = [sf_scr[s, head]]
            for c in range(n_chunks):
                sf.append(_wide(t["cdf"]) * sf[c] + _dot_tn((ks[c] * t["kdf"]).astype(BF16), vs[c]))
            sf_scr[s, head] = sf[n_chunks]
            sb = [None] * (n_chunks + 1)
            sb[n_chunks] = sb_ref[s, 0, head] if has_state else jnp.zeros((DK, DV), F32)
            for c in reversed(range(n_chunks)):
                sb[c] = _wide(t["cdb"]) * sb[c + 1] + _dot_tn((ks[c] * t["kdb"]).astype(BF16), vs[c])
            if emit_state:
                st_ref[s, 0, head] = sf[n_chunks]
                st_ref[s, 1, head] = sb[0]
            outs = []
            for c in range(n_chunks):
                scores = _dot_nt(qs[c].astype(BF16), ks[c].astype(BF16)) * t["mask"]
                o = _dot(scores.astype(BF16), vs[c])
                o = o + _dot((qs[c] * t["qdf"]).astype(BF16), sf[c].astype(BF16))
                o = o + _dot((qs[c] * t["qdb"]).astype(BF16), sb[c + 1].astype(BF16))
                outs.append(o)
            o = jnp.concatenate(outs, axis=0)
            mu = jnp.mean(o, axis=-1, keepdims=True)
            oc = o - mu
            var = jnp.mean(oc * oc, axis=-1, keepdims=True)
            head_out.append(oc * lax.rsqrt(var + EPS))
        seq_out.append(jnp.concatenate(head_out, axis=1))
    o_n = (jnp.concatenate(seq_out, axis=0) if n_seq > 1 else seq_out[0]) * gn_ref[...]
    g_all = zr[:, 2 * QK_WIDTH + V_WIDTH:]
    ret_y = _dot((_silu(g_all) * o_n).astype(BF16), w_ro_ref[...])

    gates = jax.nn.sigmoid(_dot(hb, w_bg_ref[...]) + b_bg_ref[...])
    merged = gates[:, :d] * pool_y + gates[:, d:] * ret_y
    x1 = x + rows_of([m[2:3] for m in mods]) * _dot(merged.astype(BF16), w_out_ref[...])

    h2 = _norm_mod(x1, n2_ref[...], rows_of([m[4:5] for m in mods]), rows_of([m[3:4] for m in mods]))
    for s in range(n_seq):
        rows = slice(s * seq_rows, (s + 1) * seq_rows)
        x1_ref[s] = x1[rows]
        h2_ref[s] = h2[rows].astype(BF16)
        logits = _dot_nt(w_rt_ref[...], h2[rows], precision=HIGHEST)
        e = jnp.exp(logits - jnp.max(logits, axis=0, keepdims=True))
        aff_ref[s] = e / jnp.sum(e, axis=0, keepdims=True)


def _mix(x, mod, rope, states, decay, p):
    b, l, d = x.shape
    seq_rows = min(l, STEP_ROWS)
    n_seq = STEP_ROWS // seq_rows
    nblk = l // seq_rows
    hb = seq_rows // HALO
    n_halo = l // HALO
    use_rope, has_state = rope is not None, states is not None
    emit_state = not has_state
    assert b % n_seq == 0 and (n_seq == 1 or not (use_rope or has_state))
    in_specs = [pl.BlockSpec((n_seq, seq_rows, d), lambda bi, j: (bi, j, 0)),
                pl.BlockSpec((n_seq, HALO, d), lambda bi, j: (bi, jnp.maximum(j * hb - 1, 0), 0)),
                pl.BlockSpec((n_seq, HALO, d), lambda bi, j: (bi, jnp.minimum((j + 1) * hb, n_halo - 1), 0)),
                pl.BlockSpec((n_seq, 8, d), lambda bi, j: (bi, 0, 0))]
    args = [x, x, x, mod]
    if use_rope:
        in_specs += [pl.BlockSpec((seq_rows, DK), lambda bi, j: (j, 0))] * 2
        args += list(rope)
    if has_state:
        in_specs += [pl.BlockSpec((1, N_HEADS, DK, DV), lambda bi, j: (bi, 0, 0, 0)),
                     pl.BlockSpec((1, 1, N_HEADS, DK, DV), lambda bi, j: (bi, j, 0, 0, 0))]
        args += list(states)
    weights = [p["n1g"], p["w_in"], p["w_bg"], p["b_bg"], p["w_pool"], p["pool_scale"], p["w_pool_out"],
               p["gn_g"], p["w_ret_out"], p["w_out"], p["n2g"], p["w_router_t"]]
    in_specs.append(pl.BlockSpec(memory_space=pltpu.SMEM))
    args.append(decay)
    for w in weights:
        in_specs.append(_resident(w.shape))
        args.append(w)
    out_specs = [pl.BlockSpec((n_seq, seq_rows, d), lambda bi, j: (bi, j, 0)),
                 pl.BlockSpec((n_seq, seq_rows, d), lambda bi, j: (bi, j, 0)),
                 pl.BlockSpec((n_seq, N_EXPERTS, seq_rows), lambda bi, j: (bi, 0, j))]
    out_shape = [jax.ShapeDtypeStruct((b, l, d), F32), jax.ShapeDtypeStruct((b, l, d), BF16),
                 jax.ShapeDtypeStruct((b, N_EXPERTS, l), F32)]
    if emit_state:
        out_specs.append(pl.BlockSpec((n_seq, 2, N_HEADS, DK, DV), lambda bi, j: (bi, 0, 0, 0, 0)))
        out_shape.append(jax.ShapeDtypeStruct((b, 2, N_HEADS, DK, DV), F32))
    kern = functools.partial(_mix_kernel, seq_len=l, n_seq=n_seq, seq_rows=seq_rows, use_rope=use_rope,
                             has_state=has_state, emit_state=emit_state)
    return pl.pallas_call(
        kern,
        grid=(b // n_seq, nblk),
        in_specs=in_specs,
        out_specs=out_specs,
        out_shape=out_shape,
        scratch_shapes=[pltpu.VMEM((n_seq, N_HEADS, DK, DV), F32)],
        compiler_params=_cparams(("arbitrary", "arbitrary")),
        name="mix_rope" if use_rope else "mix",
    )(*args)


def _route_kernel(aff_ref, wsel_ref, pos_ref, cnt_ref, *, cap):
    aff = aff_ref[...]
    rows, l = aff.shape

    def as_value(bits):
        return pltpu.bitcast(bits, F32)

    def bisect(i, tau):
        cand = tau | jnp.left_shift(jnp.int32(1), 30 - i)
        cnt = jnp.sum((aff >= as_value(cand)).astype(F32), axis=1, keepdims=True)
        return jnp.where(cnt >= cap, cand, tau)

    tau = lax.fori_loop(0, 31, bisect, jnp.zeros((rows, 1), jnp.int32))
    gt = aff >= as_value(tau + 1)
    eq = (aff >= as_value(tau)) & jnp.logical_not(gt)
    need = cap - jnp.sum(gt.astype(F32), axis=1, keepdims=True)

    nlb = l // ROUTE_LANES
    tri = (lax.broadcasted_iota(jnp.int32, (ROUTE_LANES, ROUTE_LANES), 0)
           < lax.broadcasted_iota(jnp.int32, (ROUTE_LANES, ROUTE_LANES), 1)).astype(BF16)

    def prefix(mask):
        carry = jnp.zeros((rows, 1), F32)
        parts, starts = [], []
        for blk in range(nlb):
            mb = mask[:, blk * ROUTE_LANES:(blk + 1) * ROUTE_LANES].astype(F32)
            starts.append(carry)
            parts.append(_dot(mb.astype(BF16), tri) + carry)
            carry = carry + jnp.sum(mb, axis=1, keepdims=True)
        starts.append(carry)
        return jnp.concatenate(parts, axis=1), starts

    eq_rank, _ = prefix(eq)
    sel = gt | (eq & (eq_rank < need))
    pos, starts = prefix(sel)
    wsel_ref[...] = jnp.where(sel, aff, 0.0)
    pos_ref[...] = jnp.where(sel, pos.astype(jnp.int32), -1)
    lane = lax.broadcasted_iota(jnp.int32, (rows, COUNT_LANES), 1)
    table = jnp.zeros((rows, COUNT_LANES), jnp.int32)
    per = max(GATHER_TOKENS // ROUTE_LANES, 1)
    for tb in range(nlb // per + 1):
        table = jnp.where(lane == tb, starts[min(tb * per, nlb)].astype(jnp.int32), table)
    cnt_ref[...] = table


def _route(aff_rows, cap, row_block):
    r, l = aff_rows.shape
    spec = pl.BlockSpec((row_block, l), lambda i: (i, 0))
    return pl.pallas_call(
        functools.partial(_route_kernel, cap=cap),
        grid=(r // row_block,),
        in_specs=[spec],
        out_specs=[spec, spec, pl.BlockSpec((row_block, COUNT_LANES), lambda i: (i, 0))],
        out_shape=[jax.ShapeDtypeStruct((r, l), F32), jax.ShapeDtypeStruct((r, l), jnp.int32),
                   jax.ShapeDtypeStruct((r, COUNT_LANES), jnp.int32)],
        compiler_params=_cparams(("arbitrary",)),
        name=f"route_{l}",
    )(aff_rows)


def _stacked_hits(pos_ref, cap):
    l = pos_ref.shape[1]
    slot = lax.broadcasted_iota(jnp.int32, (cap, l), 0)
    return [pos_ref[e:e + 1, :] == slot for e in range(N_EXPERTS)]


def _gather_small_kernel(h_ref, pos_ref, xs_ref, *, cap):
    onehot = jnp.concatenate([h.astype(BF16) for h in _stacked_hits(pos_ref, cap)], axis=0)
    xs = _dot(onehot, h_ref[0]).astype(BF16)
    for e in range(N_EXPERTS):
        xs_ref[e] = xs[e * cap:(e + 1) * cap]


def _gather_small(h2, pos, cap):
    b, l, d = h2.shape
    return pl.pallas_call(
        functools.partial(_gather_small_kernel, cap=cap),
        grid=(b,),
        in_specs=[pl.BlockSpec((1, l, d), lambda bi: (bi, 0, 0)),
                  pl.BlockSpec((N_EXPERTS, l), lambda bi: (bi, 0))],
        out_specs=pl.BlockSpec((N_EXPERTS, cap, d), lambda bi: (0, bi, 0)),
        out_shape=jax.ShapeDtypeStruct((N_EXPERTS, b * cap, d), BF16),
        compiler_params=_cparams(("arbitrary",)),
        name="gather_small",
    )(h2, pos)


def _overlaps(cnt_ref, row, tb, s):
    lo = cnt_ref[row * COUNT_LANES + tb]
    hi = cnt_ref[row * COUNT_LANES + tb + 1]
    return (lo < (s + 1) * GATHER_SLOTS) & (hi > s * GATHER_SLOTS)


def _gather_big_kernel(cnt_ref, h_ref, pos_ref, xs_ref, acc_ref):
    bi, e, s = pl.program_id(0), pl.program_id(1), pl.program_id(2)
    l = h_ref.shape[1]
    row = bi * N_EXPERTS + e
    acc_ref[...] = jnp.zeros(acc_ref.shape, F32)
    slot = s * GATHER_SLOTS + lax.broadcasted_iota(jnp.int32, (GATHER_SLOTS, GATHER_TOKENS), 0)
    for tb in range(l // GATHER_TOKENS):
        @pl.when(_overlaps(cnt_ref, row, tb, s))
        def _():
            toks = slice(tb * GATHER_TOKENS, (tb + 1) * GATHER_TOKENS)
            onehot = (pos_ref[pl.ds(e, 1), toks] == slot).astype(BF16)
            acc_ref[...] += _dot(onehot, h_ref[0, toks, :])
    xs_ref[0] = acc_ref[...].astype(BF16)


def _gather_big(h2, pos, cnt_flat, cap):
    b, l, d = h2.shape
    ns = cap // GATHER_SLOTS
    grid_spec = pltpu.PrefetchScalarGridSpec(
        num_scalar_prefetch=1,
        grid=(b, N_EXPERTS, ns),
        in_specs=[pl.BlockSpec((1, l, d), lambda bi, e, s, c: (bi, 0, 0)),
                  pl.BlockSpec((N_EXPERTS, l), lambda bi, e, s, c: (bi, 0))],
        out_specs=pl.BlockSpec((1, GATHER_SLOTS, d), lambda bi, e, s, c: (e, bi * ns + s, 0)),
        scratch_shapes=[pltpu.VMEM((GATHER_SLOTS, d), F32)],
    )
    return pl.pallas_call(
        _gather_big_kernel,
        grid_spec=grid_spec,
        out_shape=jax.ShapeDtypeStruct((N_EXPERTS, b * cap, d), BF16),
        compiler_params=_cparams(("arbitrary", "arbitrary", "arbitrary")),
        name="gather_big",
    )(cnt_flat, h2, pos)


def _ffn_kernel(xa_ref, xb_ref, wg_ref, wu_ref, wd_ref, ya_ref, yb_ref):
    @pl.when(pl.program_id(1) == 0)
    def _():
        ya_ref[...] = jnp.zeros(ya_ref.shape, F32)
        yb_ref[...] = jnp.zeros(yb_ref.shape, F32)

    wg = wg_ref[0].astype(BF16)
    wu = wu_ref[0].astype(BF16)
    wd = wd_ref[0].astype(BF16)
    for x_ref, y_ref in ((xa_ref, ya_ref), (xb_ref, yb_ref)):
        m = x_ref.shape[1]
        step = min(FFN_ROWS, m)
        for r0 in range(0, m, step):
            x = x_ref[0, r0:r0 + step, :]
            hid = (_silu(_dot(x, wg)) * _dot(x, wu)).astype(BF16)
            y_ref[0, r0:r0 + step, :] += _dot(hid, wd)


def _ffn(xs_a, xs_b, w_gate, w_up, w_down):
    e, ma, d = xs_a.shape
    mb = xs_b.shape[1]
    ff = w_gate.shape[2]
    nf = ff // FF_TILE
    return pl.pallas_call(
        _ffn_kernel,
        grid=(e, nf),
        in_specs=[pl.BlockSpec((1, ma, d), lambda ei, f: (ei, 0, 0)),
                  pl.BlockSpec((1, mb, d), lambda ei, f: (ei, 0, 0)),
                  pl.BlockSpec((1, d, FF_TILE), lambda ei, f: (ei, 0, f)),
                  pl.BlockSpec((1, d, FF_TILE), lambda ei, f: (ei, 0, f)),
                  pl.BlockSpec((1, FF_TILE, d), lambda ei, f: (ei, f, 0))],
        out_specs=[pl.BlockSpec((1, ma, d), lambda ei, f: (ei, 0, 0)),
                   pl.BlockSpec((1, mb, d), lambda ei, f: (ei, 0, 0))],
        out_shape=[jax.ShapeDtypeStruct((e, ma, d), F32), jax.ShapeDtypeStruct((e, mb, d), F32)],
        compiler_params=_cparams(("arbitrary", "arbitrary")),
        name="ffn",
    )(xs_a, xs_b, w_gate, w_up, w_down)


def _slot_gate(hit, w_row):
    return jnp.sum(jnp.where(hit, w_row, 0.0), axis=1, keepdims=True)


def _finish(x1, moe, gate2, fn_g):
    x = x1 + gate2 * moe
    return (x * lax.rsqrt(jnp.mean(x * x, axis=-1, keepdims=True) + EPS)) * fn_g


def _combine_small_kernel(x1_ref, y_ref, pos_ref, w_ref, mod_ref, fn_ref, o_ref, *, cap):
    hits = _stacked_hits(pos_ref, cap)
    gated = [(y_ref[e] * _slot_gate(hits[e], w_ref[e:e + 1, :])).astype(BF16) for e in range(N_EXPERTS)]
    onehot = jnp.concatenate([h.astype(BF16) for h in hits], axis=0)
    moe = _dot_tn(onehot, jnp.concatenate(gated, axis=0))
    o_ref[0] = _finish(x1_ref[0], moe, mod_ref[0, 5:6], fn_ref[...])


def _combine_small(x1, y, pos, wsel, mod, fn_g, cap):
    b, l, d = x1.shape
    return pl.pallas_call(
        functools.partial(_combine_small_kernel, cap=cap),
        grid=(b,),
        in_specs=[pl.BlockSpec((1, l, d), lambda bi: (bi, 0, 0)),
                  pl.BlockSpec((N_EXPERTS, cap, d), lambda bi: (0, bi, 0)),
                  pl.BlockSpec((N_EXPERTS, l), lambda bi: (bi, 0)),
                  pl.BlockSpec((N_EXPERTS, l), lambda bi: (bi, 0)),
                  pl.BlockSpec((1, 8, d), lambda bi: (bi, 0, 0)),
                  pl.BlockSpec((1, d), lambda bi: (0, 0))],
        out_specs=pl.BlockSpec((1, l, d), lambda bi: (bi, 0, 0)),
        out_shape=jax.ShapeDtypeStruct((b, l, d), F32),
        compiler_params=_cparams(("arbitrary",)),
        name="combine_small",
    )(x1, y, pos, wsel, mod, fn_g)


def _combine_big_kernel(cnt_ref, x1_ref, y_ref, pos_ref, w_ref, mod_ref, fn_ref, o_ref, *, cap, span):
    bi, tg, e = pl.program_id(0), pl.program_id(1), pl.program_id(2)
    row = bi * N_EXPERTS + e

    @pl.when(e == 0)
    def _():
        o_ref[...] = jnp.zeros(o_ref.shape, F32)

    for tl in range(span // GATHER_TOKENS):
        tb = tg * (span // GATHER_TOKENS) + tl
        toks = slice(tl * GATHER_TOKENS, (tl + 1) * GATHER_TOKENS)
        for s in range(cap // GATHER_SLOTS):
            @pl.when(_overlaps(cnt_ref, row, tb, s))
            def _():
                slot = s * GATHER_SLOTS + lax.broadcasted_iota(jnp.int32, (GATHER_SLOTS, GATHER_TOKENS), 0)
                hit = pos_ref[pl.ds(e, 1), toks] == slot
                y = y_ref[0, s * GATHER_SLOTS:(s + 1) * GATHER_SLOTS, :]
                gated = (y * _slot_gate(hit, w_ref[pl.ds(e, 1), toks])).astype(BF16)
                o_ref[0, toks, :] += _dot_tn(hit.astype(BF16), gated)

    @pl.when(e == N_EXPERTS - 1)
    def _():
        o_ref[0] = _finish(x1_ref[0], o_ref[0], mod_ref[0, 5:6], fn_ref[...])


def _combine_big(x1, y, pos, wsel, cnt_flat, mod, fn_g, cap):
    b, l, d = x1.shape
    span = min(l, 2048)
    grid_spec = pltpu.PrefetchScalarGridSpec(
        num_scalar_prefetch=1,
        grid=(b, l // span, N_EXPERTS),
        in_specs=[pl.BlockSpec((1, span, d), lambda bi, tg, e, c: (bi, tg, 0)),
                  pl.BlockSpec((1, cap, d), lambda bi, tg, e, c: (e, bi, 0)),
                  pl.BlockSpec((N_EXPERTS, span), lambda bi, tg, e, c: (bi, tg)),
                  pl.BlockSpec((N_EXPERTS, span), lambda bi, tg, e, c: (bi, tg)),
                  pl.BlockSpec((1, 8, d), lambda bi, tg, e, c: (bi, 0, 0)),
                  pl.BlockSpec((1, d), lambda bi, tg, e, c: (0, 0))],
        out_specs=pl.BlockSpec((1, span, d), lambda bi, tg, e, c: (bi, tg, 0)),
    )
    return pl.pallas_call(
        functools.partial(_combine_big_kernel, cap=cap, span=span),
        grid_spec=grid_spec,
        out_shape=jax.ShapeDtypeStruct((b, l, d), F32),
        compiler_params=_cparams(("arbitrary", "arbitrary", "arbitrary")),
        name="combine_big",
    )(cnt_flat, x1, y, pos, wsel, mod, fn_g)


def _rope_tables(l):
    rows = l // GRID_W
    row = jnp.repeat(jnp.arange(rows, dtype=F32), GRID_W)
    col = jnp.tile(jnp.arange(GRID_W, dtype=F32), rows)
    nf = DK // 4
    inv = ROPE_BASE ** (-jnp.arange(nf, dtype=F32) / nf)
    ang = jnp.concatenate([row[:, None] * inv, col[:, None] * inv], axis=-1)
    cos = jnp.repeat(jnp.cos(ang), 2, axis=-1)
    sin = jnp.repeat(jnp.sin(ang), 2, axis=-1) * jnp.tile(jnp.asarray([-1.0, 1.0], F32), DK // 2)
    return cos, sin


def kernel(x_prompt, x_sample, c, state_ret, c_ctx, w_ada, b_ada, norm1_g, w_in, w_pool, pool_scale, w_pool_out,
           ret_decay, ret_gn_g, w_ret_out, w_branch_gate, b_branch_gate, w_out, norm2_g, w_router, w_exp_gate,
           w_exp_up, w_exp_down, final_norm_g):
    depth = w_ada.shape[0]
    assert depth == 1, "single trunk layer"
    bc, lc, d = x_prompt.shape
    bl, ll, _ = x_sample.shape
    assert lc % POOL_ROWS == 0 and STEP_ROWS % lc == 0 and ll % STEP_ROWS == 0 and ll % GATHER_TOKENS == 0
    cap_c = max(1, CAPACITY_FACTOR * lc // N_EXPERTS)
    cap_l = max(1, CAPACITY_FACTOR * ll // N_EXPERTS)
    assert cap_l % GATHER_SLOTS == 0 and ll // GATHER_TOKENS < COUNT_LANES

    cond = jnp.zeros((8 * pl.cdiv(bl + 1, 8), d), F32).at[:bl].set(c).at[bl].set(c_ctx)
    ada = _adaln(cond, w_ada[0], b_ada[0]).reshape(-1, 6, d)
    ada = jnp.pad(ada, ((0, 0), (0, 2), (0, 0)))
    mod_l = ada[:bl]
    mod_c = jnp.broadcast_to(ada[bl:bl + 1], (bc, 8, d))

    p = dict(
        n1g=norm1_g[0][None], w_in=w_in[0].astype(BF16), w_bg=w_branch_gate[0].astype(BF16),
        b_bg=b_branch_gate[0][None], w_pool=w_pool[0].astype(BF16), pool_scale=pool_scale[0][None],
        w_pool_out=w_pool_out[0].astype(BF16), gn_g=ret_gn_g[0][None], w_ret_out=w_ret_out[0].astype(BF16),
        w_out=w_out[0].astype(BF16), n2g=norm2_g[0][None], w_router_t=w_router[0].T,
    )
    decay = ret_decay[0]
    fn_g = final_norm_g[None]

    x1_c, h2_c, aff_c, st_c = _mix(x_prompt, mod_c, None, None, decay, p)
    rope = _rope_tables(ll)
    sb = _revscan(x_sample, mod_l, rope[0], rope[1], state_ret[:, 0, 1], decay, p["n1g"], p["w_in"])
    x1_l, h2_l, aff_l = _mix(x_sample, mod_l, rope, (state_ret[:, 0, 0], sb), decay, p)

    wsel_c, pos_c, _ = _route(aff_c.reshape(bc * N_EXPERTS, lc), cap_c, min(128, bc * N_EXPERTS))
    wsel_l, pos_l, cnt_l = _route(aff_l.reshape(bl * N_EXPERTS, ll), cap_l, N_EXPERTS)
    cnt_flat = cnt_l.reshape(-1)

    xs_c = _gather_small(h2_c, pos_c, cap_c)
    xs_l = _gather_big(h2_l, pos_l, cnt_flat, cap_l)
    y_c, y_l = _ffn(xs_c, xs_l, w_exp_gate[0], w_exp_up[0], w_exp_down[0])

    y_prompt = _combine_small(x1_c, y_c, pos_c, wsel_c, mod_c, fn_g, cap_c)
    y_sample = _combine_big(x1_l, y_l, pos_l, wsel_l, cnt_flat, mod_l, fn_g, cap_l)
    new_state = st_c[:, None].astype(x_prompt.dtype)
    return (y_prompt, y_sample, new_state)
```

```python
import functools

import jax
import jax.numpy as jnp
from jax import lax
from jax.experimental import pallas as pl
from jax.experimental.pallas import tpu as pltpu

F32 = jnp.float32
BF16 = jnp.bfloat16
HIGHEST = lax.Precision.HIGHEST

N_HEADS = 4
DK = 128
DV = 256
CHUNK = 128
POOL_WINDOWS = (2, 4, 8, 16)
POOL_GROUP = 128
POOL_WIDTH = POOL_GROUP * len(POOL_WINDOWS)
QK_WIDTH = N_HEADS * DK
V_WIDTH = N_HEADS * DV
N_EXPERTS = 16
CAPACITY_FACTOR = 2
GRID_W = 64
ROPE_BASE = 10000.0
EPS = 1e-6

STEP_ROWS = 512
ROUTER_LANES = 128
HALO = 16
ROUTE_LANES = 256
GATHER_TOKENS = 512
GATHER_SLOTS = 128
COUNT_LANES = 128
EXPERT_PAIR = 2
FF_TILE = 256
FFN_ROWS = 512
VMEM_LIMIT = 56 * 1024 * 1024


def _cparams(sem):
    return pltpu.CompilerParams(dimension_semantics=sem, vmem_limit_bytes=VMEM_LIMIT)


def _resident(shape):
    return pl.BlockSpec(shape, lambda *_: (0,) * len(shape), pipeline_mode=pl.Buffered(1))


def _silu(x):
    return x * jax.nn.sigmoid(x)


def _norm_mod(x, g, scale, shift):
    y = x * lax.rsqrt(jnp.mean(x * x, axis=-1, keepdims=True) + EPS)
    return (y * g) * (1.0 + scale) + shift


def _dot(a, b):
    return jnp.dot(a, b, preferred_element_type=F32)


def _dot_nt(a, b, precision=None):
    return lax.dot_general(a, b, (((1,), (1,)), ((), ())), preferred_element_type=F32, precision=precision)


def _dot_tn(a, b):
    return lax.dot_general(a, b, (((0,), (0,)), ((), ())), preferred_element_type=F32)


def _adaln_kernel(c_ref, w_ref, b_ref, o_ref):
    o_ref[...] = jnp.dot(_silu(c_ref[...]), w_ref[...], preferred_element_type=F32, precision=HIGHEST) + b_ref[...]


def _adaln(cond, w, b):
    rows, d = cond.shape
    n = w.shape[1]
    tn = d
    return pl.pallas_call(
        _adaln_kernel,
        grid=(n // tn,),
        in_specs=[pl.BlockSpec((rows, d), lambda i: (0, 0)),
                  pl.BlockSpec((d, tn), lambda i: (0, i)),
                  pl.BlockSpec((1, tn), lambda i: (0, i))],
        out_specs=pl.BlockSpec((rows, tn), lambda i: (0, i)),
        out_shape=jax.ShapeDtypeStruct((rows, n), F32),
        compiler_params=_cparams(("arbitrary",)),
        name="adaln",
    )(cond, w, b.reshape(1, n))


def _decay_tiles(decay_ref, head):
    lgf = -jnp.exp(jnp.full((CHUNK, DK), decay_ref[0, head], F32))
    lgb = -jnp.exp(jnp.full((CHUNK, DK), decay_ref[1, head], F32))
    i = lax.broadcasted_iota(jnp.int32, (CHUNK, DK), 0).astype(F32)
    m = lax.broadcasted_iota(jnp.int32, (CHUNK, DK), 1).astype(F32)
    diff = i - m
    return dict(
        mask=jnp.where(diff >= 0.0, jnp.exp(lgf * jnp.maximum(diff, 0.0)), jnp.exp(lgb * jnp.maximum(-diff, 0.0))),
        qdf=jnp.exp(lgf * (i + 1.0)),
        qdb=jnp.exp(lgb * (CHUNK - i)),
        kdf=jnp.exp(lgf * (CHUNK - 1.0 - i)),
        kdb=jnp.exp(lgb * i),
        cdf=jnp.exp(lgf * float(CHUNK)),
        cdb=jnp.exp(lgb * float(CHUNK)),
    )


def _wide(t):
    return jnp.concatenate([t, t], axis=1)


def _rope(x, cos, sin):
    even = (lax.broadcasted_iota(jnp.int32, x.shape, 1) % 2) == 0
    partner = jnp.where(even, pltpu.roll(x, x.shape[1] - 1, 1), pltpu.roll(x, 1, 1))
    return x * cos + partner * sin


def _revscan_kernel(x_ref, mod_ref, cos_ref, sin_ref, s0_ref, decay_ref, n1_ref, w_in_ref, sb_ref, s_scr):
    j = pl.program_id(1)

    @pl.when(j == 0)
    def _():
        s_scr[...] = s0_ref[0]

    mod = mod_ref[0]
    h = _norm_mod(x_ref[0], n1_ref[...], mod[1:2], mod[0:1]).astype(BF16)
    k_lo = POOL_WIDTH + QK_WIDTH
    kv = _dot(h, w_in_ref[:, k_lo:k_lo + QK_WIDTH + V_WIDTH])
    sb_ref[0, 0] = s_scr[...]
    for head in range(N_HEADS):
        t = _decay_tiles(decay_ref, head)
        s = s_scr[head]
        for c in reversed(range(STEP_ROWS // CHUNK)):
            rows = slice(c * CHUNK, (c + 1) * CHUNK)
            k = _rope(kv[rows, head * DK:(head + 1) * DK] * (DK ** -0.5), cos_ref[rows], sin_ref[rows])
            v = kv[rows, QK_WIDTH + head * DV:QK_WIDTH + (head + 1) * DV]
            s = _wide(t["cdb"]) * s + _dot_tn((k * t["kdb"]).astype(BF16), v.astype(BF16))
        s_scr[head] = s


def _revscan(x, mod, cos, sin, s0_b, decay, n1g, w_in):
    b, l, d = x.shape
    nblk = l // STEP_ROWS
    return pl.pallas_call(
        _revscan_kernel,
        grid=(b, nblk),
        in_specs=[pl.BlockSpec((1, STEP_ROWS, d), lambda bi, j: (bi, nblk - 1 - j, 0)),
                  pl.BlockSpec((1, 8, d), lambda bi, j: (bi, 0, 0)),
                  pl.BlockSpec((STEP_ROWS, DK), lambda bi, j: (nblk - 1 - j, 0)),
                  pl.BlockSpec((STEP_ROWS, DK), lambda bi, j: (nblk - 1 - j, 0)),
                  pl.BlockSpec((1, N_HEADS, DK, DV), lambda bi, j: (bi, 0, 0, 0)),
                  pl.BlockSpec(memory_space=pltpu.SMEM),
                  _resident(n1g.shape),
                  _resident(w_in.shape)],
        out_specs=pl.BlockSpec((1, 1, N_HEADS, DK, DV), lambda bi, j: (bi, nblk - 1 - j, 0, 0, 0)),
        out_shape=jax.ShapeDtypeStruct((b, nblk, N_HEADS, DK, DV), F32),
        scratch_shapes=[pltpu.VMEM((N_HEADS, DK, DV), F32)],
        compiler_params=_cparams(("arbitrary", "arbitrary")),
        name="revscan",
    )(x, mod, cos, sin, s0_b, decay, n1g, w_in)


def _mix_kernel(*refs, seq_len, n_seq, seq_rows, use_rope, has_state, emit_state):
    it = iter(refs)
    x_ref, xp_ref, xn_ref, mod_ref = next(it), next(it), next(it), next(it)
    cos_ref = sin_ref = s0f_ref = sb_ref = None
    if use_rope:
        cos_ref, sin_ref = next(it), next(it)
    if has_state:
        s0f_ref, sb_ref = next(it), next(it)
    (decay_ref, n1_ref, w_in_ref, w_bg_ref, b_bg_ref, w_pool_ref, pscale_ref, w_po_ref, gn_ref, w_ro_ref,
     w_out_ref, n2_ref, w_rt_ref) = (next(it) for _ in range(13))
    x1_ref, h2_ref, aff_ref = next(it), next(it), next(it)
    st_ref = next(it) if emit_state else None
    sf_scr = next(it)

    j = pl.program_id(1)
    d = x_ref.shape[2]
    n_chunks = seq_rows // CHUNK
    ext = seq_rows + 2 * HALO

    @pl.when(j == 0)
    def _():
        if has_state:
            sf_scr[...] = s0f_ref[...]
        else:
            sf_scr[...] = jnp.zeros(sf_scr.shape, F32)

    mods = [mod_ref[s] for s in range(n_seq)]

    def rows_of(vals):
        return jnp.concatenate([jnp.broadcast_to(v, (seq_rows, d)) for v in vals], axis=0) if n_seq > 1 else vals[0]

    he_parts = []
    for s in range(n_seq):
        xe = jnp.concatenate([xp_ref[s], x_ref[s], xn_ref[s]], axis=0)
        he_parts.append(_norm_mod(xe, n1_ref[...], mods[s][1:2], mods[s][0:1]).astype(BF16))
    he = jnp.concatenate(he_parts, axis=0) if n_seq > 1 else he_parts[0]
    hb_parts = [hp[HALO:HALO + seq_rows] for hp in he_parts]
    hb = jnp.concatenate(hb_parts, axis=0) if n_seq > 1 else hb_parts[0]
    x = jnp.concatenate([x_ref[s] for s in range(n_seq)], axis=0) if n_seq > 1 else x_ref[0]

    ue_all = _dot(he, w_in_ref[:, :POOL_WIDTH])
    epos = j * seq_rows - HALO + lax.broadcasted_iota(jnp.int32, (ext, 1), 0)
    valid = (epos >= 0) & (epos < seq_len)
    tpos = j * seq_rows + lax.broadcasted_iota(jnp.int32, (seq_rows, 1), 0)
    pooled = [[] for _ in POOL_WINDOWS]
    for s in range(n_seq):
        ue = jnp.where(valid, ue_all[s * ext:(s + 1) * ext], 0.0)
        for gi, w in enumerate(POOL_WINDOWS):
            ug = ue[:, gi * POOL_GROUP:(gi + 1) * POOL_GROUP]
            acc, shift = ug, 1
            while shift < w:
                acc = acc + pltpu.roll(acc, shift, 0)
                shift *= 2
            if w // 2 > 1:
                acc = pltpu.roll(acc, ext - (w // 2 - 1), 0)
            cnt = (jnp.minimum(tpos + w // 2, seq_len) - jnp.maximum(tpos - w // 2, 0)).astype(F32)
            own = slice(HALO, HALO + seq_rows)
            pooled[gi].append((acc[own] / cnt - ug[own]).astype(BF16))
    pool_h = jnp.concatenate(
        [_dot(jnp.concatenate(pg, axis=0) if len(pg) > 1 else pg[0], w_pool_ref[gi]) for gi, pg in enumerate(pooled)],
        axis=1) * pscale_ref[...]
    pool_y = _dot(pool_h.astype(BF16), w_po_ref[...])

    zr = _dot(hb, w_in_ref[:, POOL_WIDTH:])
    seq_out = []
    for s in range(n_seq):
        r0 = s * seq_rows
        head_out = []
        for head in range(N_HEADS):
            t = _decay_tiles(decay_ref, head)
            qs, ks, vs = [], [], []
            for c in range(n_chunks):
                rows = slice(r0 + c * CHUNK, r0 + (c + 1) * CHUNK)
                q = zr[rows, head * DK:(head + 1) * DK]
                k = zr[rows, QK_WIDTH + head * DK:QK_WIDTH + (head + 1) * DK] * (DK ** -0.5)
                if use_rope:
                    crow = slice(c * CHUNK, (c + 1) * CHUNK)
                    q = _rope(q, cos_ref[crow], sin_ref[crow])
                    k = _rope(k, cos_ref[crow], sin_ref[crow])
                qs.append(q)
                ks.append(k)
                vs.append(zr[rows, 2 * QK_WIDTH + head * DV:2 * QK_WIDTH + (head + 1) * DV].astype(BF16))
            sf = [sf_scr[s, head]]
            for c in range(n_chunks):
                sf.append(_wide(t["cdf"]) * sf[c] + _dot_tn((ks[c] * t["kdf"]).astype(BF16), vs[c]))
            sf_scr[s, head] = sf[n_chunks]
            sb = [None] * (n_chunks + 1)
            sb[n_chunks] = sb_ref[s, 0, head] if has_state else jnp.zeros((DK, DV), F32)
            for c in reversed(range(n_chunks)):
                sb[c] = _wide(t["cdb"]) * sb[c + 1] + _dot_tn((ks[c] * t["kdb"]).astype(BF16), vs[c])
            if emit_state:
                st_ref[s, 0, head] = sf[n_chunks]
                st_ref[s, 1, head] = sb[0]
            outs = []
            for c in range(n_chunks):
                scores = _dot_nt(qs[c].astype(BF16), ks[c].astype(BF16)) * t["mask"]
                q_both = jnp.concatenate([(qs[c] * t["qdf"]).astype(BF16), (qs[c] * t["qdb"]).astype(BF16)], axis=1)
                s_both = jnp.concatenate([sf[c].astype(BF16), sb[c + 1].astype(BF16)], axis=0)
                outs.append(_dot(scores.astype(BF16), vs[c]) + _dot(q_both, s_both))
            o = jnp.concatenate(outs, axis=0)
            mu = jnp.mean(o, axis=-1, keepdims=True)
            oc = o - mu
            var = jnp.mean(oc * oc, axis=-1, keepdims=True)
            head_out.append(oc * lax.rsqrt(var + EPS))
        seq_out.append(jnp.concatenate(head_out, axis=1))
    o_n = (jnp.concatenate(seq_out, axis=0) if n_seq > 1 else seq_out[0]) * gn_ref[...]
    g_all = zr[:, 2 * QK_WIDTH + V_WIDTH:]
    ret_y = _dot((_silu(g_all) * o_n).astype(BF16), w_ro_ref[...])

    gates = jax.nn.sigmoid(_dot(hb, w_bg_ref[...]) + b_bg_ref[...])
    merged = gates[:, :d] * pool_y + gates[:, d:] * ret_y
    x1 = x + rows_of([m[2:3] for m in mods]) * _dot(merged.astype(BF16), w_out_ref[...])

    h2 = _norm_mod(x1, n2_ref[...], rows_of([m[4:5] for m in mods]), rows_of([m[3:4] for m in mods]))
    h2_hi = h2.astype(BF16)
    h2_lo = (h2 - h2_hi.astype(F32)).astype(BF16)
    n_rows = n_seq * seq_rows
    parts = _dot(jnp.concatenate([h2_hi, h2_lo], axis=0), w_rt_ref[...])
    logits_t = (parts[:n_rows, :ROUTER_LANES] + parts[n_rows:, :ROUTER_LANES]
                + parts[:n_rows, ROUTER_LANES:])
    for s in range(n_seq):
        rows = slice(s * seq_rows, (s + 1) * seq_rows)
        x1_ref[s] = x1[rows]
        h2_ref[s] = h2_hi[rows]
        logits = logits_t[rows].T[:N_EXPERTS]
        e = jnp.exp(logits - jnp.max(logits, axis=0, keepdims=True))
        aff_ref[s] = e / jnp.sum(e, axis=0, keepdims=True)


def _mix(x, mod, rope, states, decay, p):
    b, l, d = x.shape
    seq_rows = min(l, STEP_ROWS)
    n_seq = STEP_ROWS // seq_rows
    nblk = l // seq_rows
    hb = seq_rows // HALO
    n_halo = l // HALO
    use_rope, has_state = rope is not None, states is not None
    emit_state = not has_state
    assert b % n_seq == 0 and (n_seq == 1 or not (use_rope or has_state))
    in_specs = [pl.BlockSpec((n_seq, seq_rows, d), lambda bi, j: (bi, j, 0)),
                pl.BlockSpec((n_seq, HALO, d), lambda bi, j: (bi, jnp.maximum(j * hb - 1, 0), 0)),
                pl.BlockSpec((n_seq, HALO, d), lambda bi, j: (bi, jnp.minimum((j + 1) * hb, n_halo - 1), 0)),
                pl.BlockSpec((n_seq, 8, d), lambda bi, j: (bi, 0, 0))]
    args = [x, x, x, mod]
    if use_rope:
        in_specs += [pl.BlockSpec((seq_rows, DK), lambda bi, j: (j, 0))] * 2
        args += list(rope)
    if has_state:
        in_specs += [pl.BlockSpec((1, N_HEADS, DK, DV), lambda bi, j: (bi, 0, 0, 0)),
                     pl.BlockSpec((1, 1, N_HEADS, DK, DV), lambda bi, j: (bi, j, 0, 0, 0))]
        args += list(states)
    weights = [p["n1g"], p["w_in"], p["w_bg"], p["b_bg"], p["w_pool"], p["pool_scale"], p["w_pool_out"],
               p["gn_g"], p["w_ret_out"], p["w_out"], p["n2g"], p["w_router_parts"]]
    in_specs.append(pl.BlockSpec(memory_space=pltpu.SMEM))
    args.append(decay)
    for w in weights:
        in_specs.append(_resident(w.shape))
        args.append(w)
    out_specs = [pl.BlockSpec((n_seq, seq_rows, d), lambda bi, j: (bi, j, 0)),
                 pl.BlockSpec((n_seq, seq_rows, d), lambda bi, j: (bi, j, 0)),
                 pl.BlockSpec((n_seq, N_EXPERTS, seq_rows), lambda bi, j: (bi, 0, j))]
    out_shape = [jax.ShapeDtypeStruct((b, l, d), F32), jax.ShapeDtypeStruct((b, l, d), BF16),
                 jax.ShapeDtypeStruct((b, N_EXPERTS, l), F32)]
    if emit_state:
        out_specs.append(pl.BlockSpec((n_seq, 2, N_HEADS, DK, DV), lambda bi, j: (bi, 0, 0, 0, 0)))
        out_shape.append(jax.ShapeDtypeStruct((b, 2, N_HEADS, DK, DV), F32))
    kern = functools.partial(_mix_kernel, seq_len=l, n_seq=n_seq, seq_rows=seq_rows, use_rope=use_rope,
                             has_state=has_state, emit_state=emit_state)
    return pl.pallas_call(
        kern,
        grid=(b // n_seq, nblk),
        in_specs=in_specs,
        out_specs=out_specs,
        out_shape=out_shape,
        scratch_shapes=[pltpu.VMEM((n_seq, N_HEADS, DK, DV), F32)],
        compiler_params=_cparams(("arbitrary", "arbitrary")),
        name="mix_rope" if use_rope else "mix",
    )(*args)


def _route_kernel(aff_ref, wsel_ref, pos_ref, cnt_ref, *, cap):
    aff = aff_ref[...]
    rows, l = aff.shape

    def as_value(bits):
        return pltpu.bitcast(bits, F32)

    def bisect(i, tau):
        cand = tau | jnp.left_shift(jnp.int32(1), 30 - i)
        cnt = jnp.sum((aff >= as_value(cand)).astype(F32), axis=1, keepdims=True)
        return jnp.where(cnt >= cap, cand, tau)

    tau = lax.fori_loop(0, 31, bisect, jnp.zeros((rows, 1), jnp.int32))
    gt = aff >= as_value(tau + 1)
    eq = (aff >= as_value(tau)) & jnp.logical_not(gt)
    need = cap - jnp.sum(gt.astype(F32), axis=1, keepdims=True)

    nlb = l // ROUTE_LANES
    tri = (lax.broadcasted_iota(jnp.int32, (ROUTE_LANES, ROUTE_LANES), 0)
           < lax.broadcasted_iota(jnp.int32, (ROUTE_LANES, ROUTE_LANES), 1)).astype(BF16)

    def prefix(mask):
        carry = jnp.zeros((rows, 1), F32)
        parts, starts = [], []
        for blk in range(nlb):
            mb = mask[:, blk * ROUTE_LANES:(blk + 1) * ROUTE_LANES].astype(F32)
            starts.append(carry)
            parts.append(_dot(mb.astype(BF16), tri) + carry)
            carry = carry + jnp.sum(mb, axis=1, keepdims=True)
        starts.append(carry)
        return jnp.concatenate(parts, axis=1), starts

    eq_rank, _ = prefix(eq)
    sel = gt | (eq & (eq_rank < need))
    pos, starts = prefix(sel)
    wsel_ref[...] = jnp.where(sel, aff, 0.0)
    pos_ref[...] = jnp.where(sel, pos.astype(jnp.int32), -1)
    lane = lax.broadcasted_iota(jnp.int32, (rows, COUNT_LANES), 1)
    table = jnp.zeros((rows, COUNT_LANES), jnp.int32)
    per = max(GATHER_TOKENS // ROUTE_LANES, 1)
    for tb in range(nlb // per + 1):
        table = jnp.where(lane == tb, starts[min(tb * per, nlb)].astype(jnp.int32), table)
    cnt_ref[...] = table


def _route(aff_rows, cap, row_block):
    r, l = aff_rows.shape
    spec = pl.BlockSpec((row_block, l), lambda i: (i, 0))
    return pl.pallas_call(
        functools.partial(_route_kernel, cap=cap),
        grid=(r // row_block,),
        in_specs=[spec],
        out_specs=[spec, spec, pl.BlockSpec((row_block, COUNT_LANES), lambda i: (i, 0))],
        out_shape=[jax.ShapeDtypeStruct((r, l), F32), jax.ShapeDtypeStruct((r, l), jnp.int32),
                   jax.ShapeDtypeStruct((r, COUNT_LANES), jnp.int32)],
        compiler_params=_cparams(("arbitrary",)),
        name=f"route_{l}",
    )(aff_rows)


def _stacked_hits(pos_ref, cap):
    l = pos_ref.shape[1]
    slot = lax.broadcasted_iota(jnp.int32, (cap, l), 0)
    return [pos_ref[e:e + 1, :] == slot for e in range(N_EXPERTS)]


def _gather_small_kernel(h_ref, pos_ref, xs_ref, *, cap):
    onehot = jnp.concatenate([h.astype(BF16) for h in _stacked_hits(pos_ref, cap)], axis=0)
    xs = _dot(onehot, h_ref[0]).astype(BF16)
    for e in range(N_EXPERTS):
        xs_ref[e] = xs[e * cap:(e + 1) * cap]


def _gather_small(h2, pos, cap):
    b, l, d = h2.shape
    return pl.pallas_call(
        functools.partial(_gather_small_kernel, cap=cap),
        grid=(b,),
        in_specs=[pl.BlockSpec((1, l, d), lambda bi: (bi, 0, 0)),
                  pl.BlockSpec((N_EXPERTS, l), lambda bi: (bi, 0))],
        out_specs=pl.BlockSpec((N_EXPERTS, cap, d), lambda bi: (0, bi, 0)),
        out_shape=jax.ShapeDtypeStruct((N_EXPERTS, b * cap, d), BF16),
        compiler_params=_cparams(("arbitrary",)),
        name="gather_small",
    )(h2, pos)


def _slot_windows(cnt_ref, row, tb, cap):
    lo = cnt_ref[row * COUNT_LANES + tb]
    hi = cnt_ref[row * COUNT_LANES + tb + 1]
    start = jnp.minimum(lo & jnp.int32(-8), jnp.int32(cap - GATHER_SLOTS))
    n_win = lax.shift_right_logical(hi - start + jnp.int32(GATHER_SLOTS - 1),
                                    jnp.int32(GATHER_SLOTS.bit_length() - 1))
    return pl.multiple_of(start, 8), n_win


def _window_hits(pos_row, start, first_slot=None):
    slot = start + lax.broadcasted_iota(jnp.int32, (GATHER_SLOTS, pos_row.shape[1]), 0)
    hit = pos_row == slot
    return hit if first_slot is None else hit & (slot >= first_slot)


def _gather_big_kernel(cnt_ref, h_ref, pos_ref, xs_ref, acc_ref, *, cap):
    bi, e = pl.program_id(0), pl.program_id(1)
    n_tb = h_ref.shape[1] // GATHER_TOKENS
    row = bi * N_EXPERTS + e
    acc_ref[...] = jnp.zeros(acc_ref.shape, F32)

    def add_window(tb, start, first_slot=None):
        toks = slice(tb * GATHER_TOKENS, (tb + 1) * GATHER_TOKENS)
        onehot = _window_hits(pos_ref[pl.ds(e, 1), toks], start, first_slot).astype(BF16)
        acc_ref[pl.ds(start, GATHER_SLOTS), :] += _dot(onehot, h_ref[0, toks, :])

    windows = [_slot_windows(cnt_ref, row, tb, cap) for tb in range(n_tb)]
    for tb, (start, _) in enumerate(windows):
        add_window(tb, start)
    for tb, (start, n_win) in enumerate(windows):
        def extra(k, carry, tb=tb, start=start):
            first = start + k * GATHER_SLOTS
            add_window(tb, pl.multiple_of(jnp.minimum(first, cap - GATHER_SLOTS), 8), first)
            return carry
        lax.fori_loop(1, n_win, extra, 0)
    xs_ref[0] = acc_ref[...].astype(BF16)


def _gather_big(h2, pos, cnt_flat, cap):
    b, l, d = h2.shape
    grid_spec = pltpu.PrefetchScalarGridSpec(
        num_scalar_prefetch=1,
        grid=(b, N_EXPERTS),
        in_specs=[pl.BlockSpec((1, l, d), lambda bi, e, c: (bi, 0, 0)),
                  pl.BlockSpec((N_EXPERTS, l), lambda bi, e, c: (bi, 0))],
        out_specs=pl.BlockSpec((1, cap, d), lambda bi, e, c: (e, bi, 0)),
        scratch_shapes=[pltpu.VMEM((cap, d), F32)],
    )
    return pl.pallas_call(
        functools.partial(_gather_big_kernel, cap=cap),
        grid_spec=grid_spec,
        out_shape=jax.ShapeDtypeStruct((N_EXPERTS, b * cap, d), BF16),
        compiler_params=_cparams(("arbitrary", "arbitrary")),
        name="gather_big",
    )(cnt_flat, h2, pos)


def _ffn_kernel(xa_ref, xb_ref, wg_ref, wu_ref, wd_ref, ya_ref, yb_ref):
    @pl.when(pl.program_id(1) == 0)
    def _():
        ya_ref[...] = jnp.zeros(ya_ref.shape, F32)
        yb_ref[...] = jnp.zeros(yb_ref.shape, F32)

    wg = wg_ref[0].astype(BF16)
    wu = wu_ref[0].astype(BF16)
    wd = wd_ref[0].astype(BF16)
    for x_ref, y_ref in ((xa_ref, ya_ref), (xb_ref, yb_ref)):
        m = x_ref.shape[1]
        step = min(FFN_ROWS, m)
        for r0 in range(0, m, step):
            x = x_ref[0, r0:r0 + step, :]
            hid = (_silu(_dot(x, wg)) * _dot(x, wu)).astype(BF16)
            y_ref[0, r0:r0 + step, :] += _dot(hid, wd)


def _ffn(xs_a, xs_b, w_gate, w_up, w_down):
    e, ma, d = xs_a.shape
    mb = xs_b.shape[1]
    ff = w_gate.shape[2]
    nf = ff // FF_TILE
    return pl.pallas_call(
        _ffn_kernel,
        grid=(e, nf),
        in_specs=[pl.BlockSpec((1, ma, d), lambda ei, f: (ei, 0, 0)),
                  pl.BlockSpec((1, mb, d), lambda ei, f: (ei, 0, 0)),
                  pl.BlockSpec((1, d, FF_TILE), lambda ei, f: (ei, 0, f)),
                  pl.BlockSpec((1, d, FF_TILE), lambda ei, f: (ei, 0, f)),
                  pl.BlockSpec((1, FF_TILE, d), lambda ei, f: (ei, f, 0))],
        out_specs=[pl.BlockSpec((1, ma, d), lambda ei, f: (ei, 0, 0)),
                   pl.BlockSpec((1, mb, d), lambda ei, f: (ei, 0, 0))],
        out_shape=[jax.ShapeDtypeStruct((e, ma, d), F32), jax.ShapeDtypeStruct((e, mb, d), F32)],
        compiler_params=_cparams(("arbitrary", "arbitrary")),
        name="ffn",
    )(xs_a, xs_b, w_gate, w_up, w_down)


def _slot_gate(hit, w_row):
    return jnp.sum(jnp.where(hit, w_row, 0.0), axis=1, keepdims=True)


def _finish(x1, moe, gate2, fn_g):
    x = x1 + gate2 * moe
    return (x * lax.rsqrt(jnp.mean(x * x, axis=-1, keepdims=True) + EPS)) * fn_g


def _combine_small_kernel(x1_ref, y_ref, pos_ref, w_ref, mod_ref, fn_ref, o_ref, *, cap):
    hits = _stacked_hits(pos_ref, cap)
    gated = [(y_ref[e] * _slot_gate(hits[e], w_ref[e:e + 1, :])).astype(BF16) for e in range(N_EXPERTS)]
    onehot = jnp.concatenate([h.astype(BF16) for h in hits], axis=0)
    moe = _dot_tn(onehot, jnp.concatenate(gated, axis=0))
    o_ref[0] = _finish(x1_ref[0], moe, mod_ref[0, 5:6], fn_ref[...])


def _combine_small(x1, y, pos, wsel, mod, fn_g, cap):
    b, l, d = x1.shape
    return pl.pallas_call(
        functools.partial(_combine_small_kernel, cap=cap),
        grid=(b,),
        in_specs=[pl.BlockSpec((1, l, d), lambda bi: (bi, 0, 0)),
                  pl.BlockSpec((N_EXPERTS, cap, d), lambda bi: (0, bi, 0)),
                  pl.BlockSpec((N_EXPERTS, l), lambda bi: (bi, 0)),
                  pl.BlockSpec((N_EXPERTS, l), lambda bi: (bi, 0)),
                  pl.BlockSpec((1, 8, d), lambda bi: (bi, 0, 0)),
                  pl.BlockSpec((1, d), lambda bi: (0, 0))],
        out_specs=pl.BlockSpec((1, l, d), lambda bi: (bi, 0, 0)),
        out_shape=jax.ShapeDtypeStruct((b, l, d), F32),
        compiler_params=_cparams(("arbitrary",)),
        name="combine_small",
    )(x1, y, pos, wsel, mod, fn_g)


def _combine_big_kernel(cnt_ref, x1_ref, y_ref, pos_ref, w_ref, mod_ref, fn_ref, o_ref, *, cap, span):
    bi, tg, ep = pl.program_id(0), pl.program_id(1), pl.program_id(2)
    n_tl = span // GATHER_TOKENS

    @pl.when(ep == 0)
    def _():
        o_ref[...] = jnp.zeros(o_ref.shape, F32)

    def window_terms(el, tl, start, first_slot=None):
        e = ep * EXPERT_PAIR + el
        toks = slice(tl * GATHER_TOKENS, (tl + 1) * GATHER_TOKENS)
        hit = _window_hits(pos_ref[pl.ds(e, 1), toks], start, first_slot)
        y = y_ref[el, pl.ds(start, GATHER_SLOTS), :]
        return hit.astype(BF16), (y * _slot_gate(hit, w_ref[pl.ds(e, 1), toks])).astype(BF16)

    windows = {}
    for tl in range(n_tl):
        toks = slice(tl * GATHER_TOKENS, (tl + 1) * GATHER_TOKENS)
        terms = []
        for el in range(EXPERT_PAIR):
            row = bi * N_EXPERTS + ep * EXPERT_PAIR + el
            windows[tl, el] = _slot_windows(cnt_ref, row, tg * n_tl + tl, cap)
            terms.append(window_terms(el, tl, windows[tl, el][0]))
        o_ref[0, toks, :] += _dot_tn(jnp.concatenate([t[0] for t in terms], axis=0),
                                     jnp.concatenate([t[1] for t in terms], axis=0))
    for (tl, el), (start, n_win) in windows.items():
        def extra(k, carry, tl=tl, el=el, start=start):
            first = start + k * GATHER_SLOTS
            hit, gated = window_terms(el, tl, pl.multiple_of(jnp.minimum(first, cap - GATHER_SLOTS), 8), first)
            o_ref[0, tl * GATHER_TOKENS:(tl + 1) * GATHER_TOKENS, :] += _dot_tn(hit, gated)
            return carry
        lax.fori_loop(1, n_win, extra, 0)

    @pl.when(ep == N_EXPERTS // EXPERT_PAIR - 1)
    def _():
        o_ref[0] = _finish(x1_ref[0], o_ref[0], mod_ref[0, 5:6], fn_ref[...])


def _combine_big(x1, y, pos, wsel, cnt_flat, mod, fn_g, cap):
    b, l, d = x1.shape
    span = min(l, 2048)
    grid_spec = pltpu.PrefetchScalarGridSpec(
        num_scalar_prefetch=1,
        grid=(b, l // span, N_EXPERTS // EXPERT_PAIR),
        in_specs=[pl.BlockSpec((1, span, d), lambda bi, tg, e, c: (bi, tg, 0)),
                  pl.BlockSpec((EXPERT_PAIR, cap, d), lambda bi, tg, e, c: (e, bi, 0)),
                  pl.BlockSpec((N_EXPERTS, span), lambda bi, tg, e, c: (bi, tg)),
                  pl.BlockSpec((N_EXPERTS, span), lambda bi, tg, e, c: (bi, tg)),
                  pl.BlockSpec((1, 8, d), lambda bi, tg, e, c: (bi, 0, 0)),
                  pl.BlockSpec((1, d), lambda bi, tg, e, c: (0, 0))],
        out_specs=pl.BlockSpec((1, span, d), lambda bi, tg, e, c: (bi, tg, 0)),
    )
    return pl.pallas_call(
        functools.partial(_combine_big_kernel, cap=cap, span=span),
        grid_spec=grid_spec,
        out_shape=jax.ShapeDtypeStruct((b, l, d), F32),
        compiler_params=_cparams(("arbitrary", "arbitrary", "arbitrary")),
        name="combine_big",
    )(cnt_flat, x1, y, pos, wsel, mod, fn_g)


def _router_parts(w):
    hi = w.astype(BF16)
    lo = (w - hi.astype(F32)).astype(BF16)
    pad = ((0, 0), (0, ROUTER_LANES - w.shape[1]))
    return jnp.concatenate([jnp.pad(hi, pad), jnp.pad(lo, pad)], axis=1)


def _rope_tables(l):
    rows = l // GRID_W
    row = jnp.repeat(jnp.arange(rows, dtype=F32), GRID_W)
    col = jnp.tile(jnp.arange(GRID_W, dtype=F32), rows)
    nf = DK // 4
    inv = ROPE_BASE ** (-jnp.arange(nf, dtype=F32) / nf)
    ang = jnp.concatenate([row[:, None] * inv, col[:, None] * inv], axis=-1)
    cos = jnp.repeat(jnp.cos(ang), 2, axis=-1)
    sin = jnp.repeat(jnp.sin(ang), 2, axis=-1) * jnp.tile(jnp.asarray([-1.0, 1.0], F32), DK // 2)
    return cos, sin


def kernel(x_prompt, x_sample, c, state_ret, c_ctx, w_ada, b_ada, norm1_g, w_in, w_pool, pool_scale, w_pool_out,
           ret_decay, ret_gn_g, w_ret_out, w_branch_gate, b_branch_gate, w_out, norm2_g, w_router, w_exp_gate,
           w_exp_up, w_exp_down, final_norm_g):
    depth = w_ada.shape[0]
    assert depth == 1, "single trunk layer"
    bc, lc, d = x_prompt.shape
    bl, ll, _ = x_sample.shape
    assert lc % CHUNK == 0 and STEP_ROWS % lc == 0 and ll % STEP_ROWS == 0 and ll % GATHER_TOKENS == 0
    cap_c = max(1, CAPACITY_FACTOR * lc // N_EXPERTS)
    cap_l = max(1, CAPACITY_FACTOR * ll // N_EXPERTS)
    assert cap_l % GATHER_SLOTS == 0 and ll // GATHER_TOKENS < COUNT_LANES

    cond = jnp.zeros((8 * pl.cdiv(bl + 1, 8), d), F32).at[:bl].set(c).at[bl].set(c_ctx)
    ada = _adaln(cond, w_ada[0], b_ada[0]).reshape(-1, 6, d)
    ada = jnp.pad(ada, ((0, 0), (0, 2), (0, 0)))
    mod_l = ada[:bl]
    mod_c = jnp.broadcast_to(ada[bl:bl + 1], (bc, 8, d))

    p = dict(
        n1g=norm1_g[0][None], w_in=w_in[0].astype(BF16), w_bg=w_branch_gate[0].astype(BF16),
        b_bg=b_branch_gate[0][None], w_pool=w_pool[0].astype(BF16), pool_scale=pool_scale[0][None],
        w_pool_out=w_pool_out[0].astype(BF16), gn_g=ret_gn_g[0][None], w_ret_out=w_ret_out[0].astype(BF16),
        w_out=w_out[0].astype(BF16), n2g=norm2_g[0][None], w_router_parts=_router_parts(w_router[0]),
    )
    decay = ret_decay[0]
    fn_g = final_norm_g[None]

    x1_c, h2_c, aff_c, st_c = _mix(x_prompt, mod_c, None, None, decay, p)
    rope = _rope_tables(ll)
    sb = _revscan(x_sample, mod_l, rope[0], rope[1], state_ret[:, 0, 1], decay, p["n1g"], p["w_in"])
    x1_l, h2_l, aff_l = _mix(x_sample, mod_l, rope, (state_ret[:, 0, 0], sb), decay, p)

    wsel_c, pos_c, _ = _route(aff_c.reshape(bc * N_EXPERTS, lc), cap_c, min(128, bc * N_EXPERTS))
    wsel_l, pos_l, cnt_l = _route(aff_l.reshape(bl * N_EXPERTS, ll), cap_l, N_EXPERTS)
    cnt_flat = cnt_l.reshape(-1)

    xs_c = _gather_small(h2_c, pos_c, cap_c)
    xs_l = _gather_big(h2_l, pos_l, cnt_flat, cap_l)
    y_c, y_l = _ffn(xs_c, xs_l, w_exp_gate[0], w_exp_up[0], w_exp_down[0])

    y_prompt = _combine_small(x1_c, y_c, pos_c, wsel_c, mod_c, fn_g, cap_c)
    y_sample = _combine_big(x1_l, y_l, pos_l, wsel_l, cnt_flat, mod_l, fn_g, cap_l)
    new_state = st_c[:, None].astype(x_prompt.dtype)
    return (y_prompt, y_sample, new_state)
```

```python
import functools

import jax
import jax.numpy as jnp
from jax import lax
from jax.experimental import pallas as pl
from jax.experimental.pallas import tpu as pltpu

F32 = jnp.float32
BF16 = jnp.bfloat16
HIGHEST = lax.Precision.HIGHEST

N_HEADS = 4
DK = 128
DV = 256
CHUNK = 128
POOL_WINDOWS = (2, 4, 8, 16)
POOL_GROUP = 128
POOL_WIDTH = POOL_GROUP * len(POOL_WINDOWS)
QK_WIDTH = N_HEADS * DK
V_WIDTH = N_HEADS * DV
N_EXPERTS = 16
CAPACITY_FACTOR = 2
GRID_W = 64
ROPE_BASE = 10000.0
EPS = 1e-6

STEP_ROWS = 512
ROUTER_LANES = 128
HALO = 16
ROUTE_LANES = 256
ROUTE_ROWS = 512
SMALL_REQS = 4
GATHER_TOKENS = 512
GATHER_SLOTS = 128
COUNT_LANES = 128
EXPERT_PAIR = 2
FF_TILE = 256
FFN_ROWS = 512
VMEM_LIMIT = 56 * 1024 * 1024


def _cparams(sem):
    return pltpu.CompilerParams(dimension_semantics=sem, vmem_limit_bytes=VMEM_LIMIT)


def _resident(shape):
    return pl.BlockSpec(shape, lambda *_: (0,) * len(shape), pipeline_mode=pl.Buffered(1))


def _silu(x):
    return x * jax.nn.sigmoid(x)


def _norm_mod(x, g, scale, shift):
    y = x * lax.rsqrt(jnp.mean(x * x, axis=-1, keepdims=True) + EPS)
    return (y * g) * (1.0 + scale) + shift


def _dot(a, b):
    return jnp.dot(a, b, preferred_element_type=F32)


def _dot_nt(a, b, precision=None):
    return lax.dot_general(a, b, (((1,), (1,)), ((), ())), preferred_element_type=F32, precision=precision)


def _dot_tn(a, b):
    return lax.dot_general(a, b, (((0,), (0,)), ((), ())), preferred_element_type=F32)


def _adaln_kernel(c_ref, w_ref, b_ref, o_ref):
    o_ref[...] = jnp.dot(_silu(c_ref[...]), w_ref[...], preferred_element_type=F32, precision=HIGHEST) + b_ref[...]


def _adaln(cond, w, b):
    rows, d = cond.shape
    n = w.shape[1]
    tn = d
    return pl.pallas_call(
        _adaln_kernel,
        grid=(n // tn,),
        in_specs=[pl.BlockSpec((rows, d), lambda i: (0, 0)),
                  pl.BlockSpec((d, tn), lambda i: (0, i)),
                  pl.BlockSpec((1, tn), lambda i: (0, i))],
        out_specs=pl.BlockSpec((rows, tn), lambda i: (0, i)),
        out_shape=jax.ShapeDtypeStruct((rows, n), F32),
        compiler_params=_cparams(("arbitrary",)),
        name="adaln",
    )(cond, w, b.reshape(1, n))


def _decay_tiles(decay_ref, head):
    lgf = -jnp.exp(jnp.full((CHUNK, DK), decay_ref[0, head], F32))
    lgb = -jnp.exp(jnp.full((CHUNK, DK), decay_ref[1, head], F32))
    i = lax.broadcasted_iota(jnp.int32, (CHUNK, DK), 0).astype(F32)
    m = lax.broadcasted_iota(jnp.int32, (CHUNK, DK), 1).astype(F32)
    diff = i - m
    return dict(
        mask=jnp.where(diff >= 0.0, jnp.exp(lgf * jnp.maximum(diff, 0.0)), jnp.exp(lgb * jnp.maximum(-diff, 0.0))),
        qdf=jnp.exp(lgf * (i + 1.0)),
        qdb=jnp.exp(lgb * (CHUNK - i)),
        kdf=jnp.exp(lgf * (CHUNK - 1.0 - i)),
        kdb=jnp.exp(lgb * i),
        cdf=jnp.exp(lgf * float(CHUNK)),
        cdb=jnp.exp(lgb * float(CHUNK)),
    )


def _wide(t):
    return jnp.concatenate([t, t], axis=1)


def _rope(x, cos, sin):
    even = (lax.broadcasted_iota(jnp.int32, x.shape, 1) % 2) == 0
    partner = jnp.where(even, pltpu.roll(x, x.shape[1] - 1, 1), pltpu.roll(x, 1, 1))
    return x * cos + partner * sin


def _revscan_kernel(x_ref, mod_ref, cos_ref, sin_ref, s0_ref, decay_ref, n1_ref, w_in_ref, sb_ref, kv_ref, s_scr):
    j = pl.program_id(1)

    @pl.when(j == 0)
    def _():
        s_scr[...] = s0_ref[0]

    mod = mod_ref[0]
    h = _norm_mod(x_ref[0], n1_ref[...], mod[1:2], mod[0:1]).astype(BF16)
    k_lo = POOL_WIDTH + QK_WIDTH
    kv = _dot(h, w_in_ref[:, k_lo:k_lo + QK_WIDTH + V_WIDTH])
    sb_ref[0, 0] = s_scr[...]
    kv_ref[0, :, QK_WIDTH:] = kv[:, QK_WIDTH:].astype(BF16)
    for head in range(N_HEADS):
        t = _decay_tiles(decay_ref, head)
        s = s_scr[head]
        for c in reversed(range(STEP_ROWS // CHUNK)):
            rows = slice(c * CHUNK, (c + 1) * CHUNK)
            cols = slice(head * DK, (head + 1) * DK)
            k = _rope(kv[rows, cols] * (DK ** -0.5), cos_ref[rows], sin_ref[rows])
            kv_ref[0, rows, cols] = k.astype(BF16)
            v = kv[rows, QK_WIDTH + head * DV:QK_WIDTH + (head + 1) * DV]
            s = _wide(t["cdb"]) * s + _dot_tn((k * t["kdb"]).astype(BF16), v.astype(BF16))
        s_scr[head] = s


def _revscan(x, mod, cos, sin, s0_b, decay, n1g, w_in):
    b, l, d = x.shape
    nblk = l // STEP_ROWS
    return pl.pallas_call(
        _revscan_kernel,
        grid=(b, nblk),
        in_specs=[pl.BlockSpec((1, STEP_ROWS, d), lambda bi, j: (bi, nblk - 1 - j, 0)),
                  pl.BlockSpec((1, 8, d), lambda bi, j: (bi, 0, 0)),
                  pl.BlockSpec((STEP_ROWS, DK), lambda bi, j: (nblk - 1 - j, 0)),
                  pl.BlockSpec((STEP_ROWS, DK), lambda bi, j: (nblk - 1 - j, 0)),
                  pl.BlockSpec((1, N_HEADS, DK, DV), lambda bi, j: (bi, 0, 0, 0)),
                  pl.BlockSpec(memory_space=pltpu.SMEM),
                  _resident(n1g.shape),
                  _resident(w_in.shape)],
        out_specs=[pl.BlockSpec((1, 1, N_HEADS, DK, DV), lambda bi, j: (bi, nblk - 1 - j, 0, 0, 0)),
                   pl.BlockSpec((1, STEP_ROWS, QK_WIDTH + V_WIDTH), lambda bi, j: (bi, nblk - 1 - j, 0))],
        out_shape=[jax.ShapeDtypeStruct((b, nblk, N_HEADS, DK, DV), F32),
                   jax.ShapeDtypeStruct((b, l, QK_WIDTH + V_WIDTH), BF16)],
        scratch_shapes=[pltpu.VMEM((N_HEADS, DK, DV), F32)],
        compiler_params=_cparams(("arbitrary", "arbitrary")),
        name="revscan",
    )(x, mod, cos, sin, s0_b, decay, n1g, w_in)


def _mix_kernel(*refs, seq_len, n_seq, seq_rows, use_rope, has_state, emit_state):
    it = iter(refs)
    x_ref, xp_ref, xn_ref, mod_ref = next(it), next(it), next(it), next(it)
    cos_ref = sin_ref = s0f_ref = sb_ref = kv_ref = None
    if use_rope:
        cos_ref, sin_ref = next(it), next(it)
    if has_state:
        s0f_ref, sb_ref, kv_ref = next(it), next(it), next(it)
    (decay_ref, n1_ref, w_in_ref, w_bg_ref, b_bg_ref, w_pool_ref, pscale_ref, w_po_ref, gn_ref, w_ro_ref,
     w_out_ref, n2_ref, w_rt_ref) = (next(it) for _ in range(13))
    x1_ref, h2_ref, aff_ref = next(it), next(it), next(it)
    st_ref = next(it) if emit_state else None
    sf_scr = next(it)

    j = pl.program_id(1)
    d = x_ref.shape[2]
    n_chunks = seq_rows // CHUNK
    ext = seq_rows + 2 * HALO

    @pl.when(j == 0)
    def _():
        if has_state:
            sf_scr[...] = s0f_ref[...]
        else:
            sf_scr[...] = jnp.zeros(sf_scr.shape, F32)

    mods = [mod_ref[s] for s in range(n_seq)]

    def rows_of(vals):
        return jnp.concatenate([jnp.broadcast_to(v, (seq_rows, d)) for v in vals], axis=0) if n_seq > 1 else vals[0]

    he_parts = []
    for s in range(n_seq):
        xe = jnp.concatenate([xp_ref[s], x_ref[s], xn_ref[s]], axis=0)
        he_parts.append(_norm_mod(xe, n1_ref[...], mods[s][1:2], mods[s][0:1]).astype(BF16))
    he = jnp.concatenate(he_parts, axis=0) if n_seq > 1 else he_parts[0]
    hb_parts = [hp[HALO:HALO + seq_rows] for hp in he_parts]
    hb = jnp.concatenate(hb_parts, axis=0) if n_seq > 1 else hb_parts[0]
    x = jnp.concatenate([x_ref[s] for s in range(n_seq)], axis=0) if n_seq > 1 else x_ref[0]

    ue_all = _dot(he, w_in_ref[:, :POOL_WIDTH])
    epos = j * seq_rows - HALO + lax.broadcasted_iota(jnp.int32, (ext, 1), 0)
    valid = (epos >= 0) & (epos < seq_len)
    tpos = j * seq_rows + lax.broadcasted_iota(jnp.int32, (seq_rows, 1), 0)
    pooled = [[] for _ in POOL_WINDOWS]
    for s in range(n_seq):
        ue = jnp.where(valid, ue_all[s * ext:(s + 1) * ext], 0.0)
        for gi, w in enumerate(POOL_WINDOWS):
            ug = ue[:, gi * POOL_GROUP:(gi + 1) * POOL_GROUP]
            acc, shift = ug, 1
            while shift < w:
                acc = acc + pltpu.roll(acc, shift, 0)
                shift *= 2
            if w // 2 > 1:
                acc = pltpu.roll(acc, ext - (w // 2 - 1), 0)
            cnt = (jnp.minimum(tpos + w // 2, seq_len) - jnp.maximum(tpos - w // 2, 0)).astype(F32)
            own = slice(HALO, HALO + seq_rows)
            pooled[gi].append((acc[own] / cnt - ug[own]).astype(BF16))
    pool_h = jnp.concatenate(
        [_dot(jnp.concatenate(pg, axis=0) if len(pg) > 1 else pg[0], w_pool_ref[gi]) for gi, pg in enumerate(pooled)],
        axis=1) * pscale_ref[...]
    pool_y = _dot(pool_h.astype(BF16), w_po_ref[...])

    g_lo = POOL_WIDTH + 2 * QK_WIDTH + V_WIDTH
    if has_state:
        zq = _dot(hb, w_in_ref[:, POOL_WIDTH:POOL_WIDTH + QK_WIDTH])
        g_all = _dot(hb, w_in_ref[:, g_lo:])
    else:
        zr = _dot(hb, w_in_ref[:, POOL_WIDTH:])
        zq, g_all = zr[:, :QK_WIDTH], zr[:, 2 * QK_WIDTH + V_WIDTH:]
    seq_out = []
    for s in range(n_seq):
        r0 = s * seq_rows
        head_out = []
        for head in range(N_HEADS):
            t = _decay_tiles(decay_ref, head)
            qs, ks, vs = [], [], []
            for c in range(n_chunks):
                rows = slice(r0 + c * CHUNK, r0 + (c + 1) * CHUNK)
                crow = slice(c * CHUNK, (c + 1) * CHUNK)
                q = zq[rows, head * DK:(head + 1) * DK]
                if use_rope:
                    q = _rope(q, cos_ref[crow], sin_ref[crow])
                if has_state:
                    k = kv_ref[s, crow, head * DK:(head + 1) * DK].astype(F32)
                    v = kv_ref[s, crow, QK_WIDTH + head * DV:QK_WIDTH + (head + 1) * DV]
                else:
                    k = zr[rows, QK_WIDTH + head * DK:QK_WIDTH + (head + 1) * DK] * (DK ** -0.5)
                    if use_rope:
                        k = _rope(k, cos_ref[crow], sin_ref[crow])
                    v = zr[rows, 2 * QK_WIDTH + head * DV:2 * QK_WIDTH + (head + 1) * DV].astype(BF16)
                qs.append(q)
                ks.append(k)
                vs.append(v)
            sf = [sf_scr[s, head]]
            for c in range(n_chunks):
                sf.append(_wide(t["cdf"]) * sf[c] + _dot_tn((ks[c] * t["kdf"]).astype(BF16), vs[c]))
            sf_scr[s, head] = sf[n_chunks]
            sb = [None] * (n_chunks + 1)
            sb[n_chunks] = sb_ref[s, 0, head] if has_state else jnp.zeros((DK, DV), F32)
            for c in reversed(range(n_chunks)):
                sb[c] = _wide(t["cdb"]) * sb[c + 1] + _dot_tn((ks[c] * t["kdb"]).astype(BF16), vs[c])
            if emit_state:
                st_ref[s, 0, head] = sf[n_chunks]
                st_ref[s, 1, head] = sb[0]
            outs = []
            for c in range(n_chunks):
                scores = _dot_nt(qs[c].astype(BF16), ks[c].astype(BF16)) * t["mask"]
                q_both = jnp.concatenate([(qs[c] * t["qdf"]).astype(BF16), (qs[c] * t["qdb"]).astype(BF16)], axis=1)
                s_both = jnp.concatenate([sf[c].astype(BF16), sb[c + 1].astype(BF16)], axis=0)
                outs.append(_dot(scores.astype(BF16), vs[c]) + _dot(q_both, s_both))
            o = jnp.concatenate(outs, axis=0)
            mu = jnp.mean(o, axis=-1, keepdims=True)
            oc = o - mu
            var = jnp.mean(oc * oc, axis=-1, keepdims=True)
            head_out.append(oc * lax.rsqrt(var + EPS))
        seq_out.append(jnp.concatenate(head_out, axis=1))
    o_n = (jnp.concatenate(seq_out, axis=0) if n_seq > 1 else seq_out[0]) * gn_ref[...]
    ret_y = _dot((_silu(g_all) * o_n).astype(BF16), w_ro_ref[...])

    gates = jax.nn.sigmoid(_dot(hb, w_bg_ref[...]) + b_bg_ref[...])
    merged = gates[:, :d] * pool_y + gates[:, d:] * ret_y
    x1 = x + rows_of([m[2:3] for m in mods]) * _dot(merged.astype(BF16), w_out_ref[...])

    h2 = _norm_mod(x1, n2_ref[...], rows_of([m[4:5] for m in mods]), rows_of([m[3:4] for m in mods]))
    h2_hi = h2.astype(BF16)
    h2_lo = (h2 - h2_hi.astype(F32)).astype(BF16)
    n_rows = n_seq * seq_rows
    parts = _dot(jnp.concatenate([h2_hi, h2_lo], axis=0), w_rt_ref[...])
    logits_t = (parts[:n_rows, :ROUTER_LANES] + parts[n_rows:, :ROUTER_LANES]
                + parts[:n_rows, ROUTER_LANES:])
    for s in range(n_seq):
        rows = slice(s * seq_rows, (s + 1) * seq_rows)
        x1_ref[s] = x1[rows]
        h2_ref[s] = h2_hi[rows]
        logits = logits_t[rows].T[:N_EXPERTS]
        e = jnp.exp(logits - jnp.max(logits, axis=0, keepdims=True))
        aff_ref[s] = e / jnp.sum(e, axis=0, keepdims=True)


def _mix(x, mod, rope, states, decay, p):
    b, l, d = x.shape
    seq_rows = min(l, STEP_ROWS)
    n_seq = STEP_ROWS // seq_rows
    nblk = l // seq_rows
    hb = seq_rows // HALO
    n_halo = l // HALO
    use_rope, has_state = rope is not None, states is not None
    emit_state = not has_state
    assert b % n_seq == 0 and (n_seq == 1 or not (use_rope or has_state))
    in_specs = [pl.BlockSpec((n_seq, seq_rows, d), lambda bi, j: (bi, j, 0)),
                pl.BlockSpec((n_seq, HALO, d), lambda bi, j: (bi, jnp.maximum(j * hb - 1, 0), 0)),
                pl.BlockSpec((n_seq, HALO, d), lambda bi, j: (bi, jnp.minimum((j + 1) * hb, n_halo - 1), 0)),
                pl.BlockSpec((n_seq, 8, d), lambda bi, j: (bi, 0, 0))]
    args = [x, x, x, mod]
    if use_rope:
        in_specs += [pl.BlockSpec((seq_rows, DK), lambda bi, j: (j, 0))] * 2
        args += list(rope)
    if has_state:
        in_specs += [pl.BlockSpec((1, N_HEADS, DK, DV), lambda bi, j: (bi, 0, 0, 0)),
                     pl.BlockSpec((1, 1, N_HEADS, DK, DV), lambda bi, j: (bi, j, 0, 0, 0)),
                     pl.BlockSpec((1, seq_rows, QK_WIDTH + V_WIDTH), lambda bi, j: (bi, j, 0))]
        args += list(states)
    weights = [p["n1g"], p["w_in"], p["w_bg"], p["b_bg"], p["w_pool"], p["pool_scale"], p["w_pool_out"],
               p["gn_g"], p["w_ret_out"], p["w_out"], p["n2g"], p["w_router_parts"]]
    in_specs.append(pl.BlockSpec(memory_space=pltpu.SMEM))
    args.append(decay)
    for w in weights:
        in_specs.append(_resident(w.shape))
        args.append(w)
    out_specs = [pl.BlockSpec((n_seq, seq_rows, d), lambda bi, j: (bi, j, 0)),
                 pl.BlockSpec((n_seq, seq_rows, d), lambda bi, j: (bi, j, 0)),
                 pl.BlockSpec((n_seq, N_EXPERTS, seq_rows), lambda bi, j: (bi, 0, j))]
    out_shape = [jax.ShapeDtypeStruct((b, l, d), F32), jax.ShapeDtypeStruct((b, l, d), BF16),
                 jax.ShapeDtypeStruct((b, N_EXPERTS, l), F32)]
    if emit_state:
        out_specs.append(pl.BlockSpec((n_seq, 2, N_HEADS, DK, DV), lambda bi, j: (bi, 0, 0, 0, 0)))
        out_shape.append(jax.ShapeDtypeStruct((b, 2, N_HEADS, DK, DV), F32))
    kern = functools.partial(_mix_kernel, seq_len=l, n_seq=n_seq, seq_rows=seq_rows, use_rope=use_rope,
                             has_state=has_state, emit_state=emit_state)
    return pl.pallas_call(
        kern,
        grid=(b // n_seq, nblk),
        in_specs=in_specs,
        out_specs=out_specs,
        out_shape=out_shape,
        scratch_shapes=[pltpu.VMEM((n_seq, N_HEADS, DK, DV), F32)],
        compiler_params=_cparams(("arbitrary", "arbitrary")),
        name="mix_rope" if use_rope else "mix",
    )(*args)


def _route_kernel(aff_ref, wsel_ref, pos_ref, cnt_ref, *, cap):
    aff = aff_ref[...]
    rows, l = aff.shape

    def as_value(bits):
        return pltpu.bitcast(bits, F32)

    def bisect(i, tau):
        cand = tau | jnp.left_shift(jnp.int32(1), 30 - i)
        cnt = jnp.sum((aff >= as_value(cand)).astype(F32), axis=1, keepdims=True)
        return jnp.where(cnt >= cap, cand, tau)

    tau = lax.fori_loop(0, 31, bisect, jnp.zeros((rows, 1), jnp.int32))
    gt = aff >= as_value(tau + 1)
    eq = (aff >= as_value(tau)) & jnp.logical_not(gt)
    need = cap - jnp.sum(gt.astype(F32), axis=1, keepdims=True)

    nlb = l // ROUTE_LANES
    tri = (lax.broadcasted_iota(jnp.int32, (ROUTE_LANES, ROUTE_LANES), 0)
           < lax.broadcasted_iota(jnp.int32, (ROUTE_LANES, ROUTE_LANES), 1)).astype(BF16)

    def prefix(mask):
        carry = jnp.zeros((rows, 1), F32)
        parts, starts = [], []
        for blk in range(nlb):
            mb = mask[:, blk * ROUTE_LANES:(blk + 1) * ROUTE_LANES].astype(F32)
            starts.append(carry)
            parts.append(_dot(mb.astype(BF16), tri) + carry)
            carry = carry + jnp.sum(mb, axis=1, keepdims=True)
        starts.append(carry)
        return jnp.concatenate(parts, axis=1), starts

    eq_rank, _ = prefix(eq)
    sel = gt | (eq & (eq_rank < need))
    pos, starts = prefix(sel)
    wsel_ref[...] = jnp.where(sel, aff, 0.0)
    pos_ref[...] = jnp.where(sel, pos.astype(jnp.int32), -1)
    lane = lax.broadcasted_iota(jnp.int32, (rows, COUNT_LANES), 1)
    table = jnp.zeros((rows, COUNT_LANES), jnp.int32)
    per = max(GATHER_TOKENS // ROUTE_LANES, 1)
    for tb in range(nlb // per + 1):
        table = jnp.where(lane == tb, starts[min(tb * per, nlb)].astype(jnp.int32), table)
    cnt_ref[...] = table


def _route(aff_rows, cap, row_block):
    r, l = aff_rows.shape
    spec = pl.BlockSpec((row_block, l), lambda i: (i, 0))
    return pl.pallas_call(
        functools.partial(_route_kernel, cap=cap),
        grid=(r // row_block,),
        in_specs=[spec],
        out_specs=[spec, spec, pl.BlockSpec((row_block, COUNT_LANES), lambda i: (i, 0))],
        out_shape=[jax.ShapeDtypeStruct((r, l), F32), jax.ShapeDtypeStruct((r, l), jnp.int32),
                   jax.ShapeDtypeStruct((r, COUNT_LANES), jnp.int32)],
        compiler_params=_cparams(("arbitrary",)),
        name=f"route_{l}",
    )(aff_rows)


def _stacked_hits(pos_ref, req, cap):
    l = pos_ref.shape[1]
    slot = lax.broadcasted_iota(jnp.int32, (cap, l), 0)
    return [pos_ref[req * N_EXPERTS + e:req * N_EXPERTS + e + 1, :] == slot for e in range(N_EXPERTS)]


def _gather_small_kernel(h_ref, pos_ref, xs_ref, *, cap):
    for req in range(h_ref.shape[0]):
        onehot = jnp.concatenate([h.astype(BF16) for h in _stacked_hits(pos_ref, req, cap)], axis=0)
        xs = _dot(onehot, h_ref[req]).astype(BF16)
        for e in range(N_EXPERTS):
            xs_ref[e, req * cap:(req + 1) * cap, :] = xs[e * cap:(e + 1) * cap]


def _gather_small(h2, pos, cap):
    b, l, d = h2.shape
    nr = SMALL_REQS if b % SMALL_REQS == 0 else 1
    return pl.pallas_call(
        functools.partial(_gather_small_kernel, cap=cap),
        grid=(b // nr,),
        in_specs=[pl.BlockSpec((nr, l, d), lambda bi: (bi, 0, 0)),
                  pl.BlockSpec((nr * N_EXPERTS, l), lambda bi: (bi, 0))],
        out_specs=pl.BlockSpec((N_EXPERTS, nr * cap, d), lambda bi: (0, bi, 0)),
        out_shape=jax.ShapeDtypeStruct((N_EXPERTS, b * cap, d), BF16),
        compiler_params=_cparams(("arbitrary",)),
        name="gather_small",
    )(h2, pos)


def _slot_windows(cnt_ref, row, tb, cap):
    lo = cnt_ref[row * COUNT_LANES + tb]
    hi = cnt_ref[row * COUNT_LANES + tb + 1]
    start = jnp.minimum(lo & jnp.int32(-8), jnp.int32(cap - GATHER_SLOTS))
    n_win = lax.shift_right_logical(hi - start + jnp.int32(GATHER_SLOTS - 1),
                                    jnp.int32(GATHER_SLOTS.bit_length() - 1))
    return pl.multiple_of(start, 8), n_win


def _window_hits(pos_row, start, first_slot=None):
    slot = start + lax.broadcasted_iota(jnp.int32, (GATHER_SLOTS, pos_row.shape[1]), 0)
    hit = pos_row == slot
    return hit if first_slot is None else hit & (slot >= first_slot)


def _gather_big_kernel(cnt_ref, h_ref, pos_ref, xs_ref, acc_ref, *, cap):
    bi, e = pl.program_id(0), pl.program_id(1)
    n_tb = h_ref.shape[1] // GATHER_TOKENS
    row = bi * N_EXPERTS + e
    acc_ref[...] = jnp.zeros(acc_ref.shape, F32)

    def add_window(tb, start, first_slot=None):
        toks = slice(tb * GATHER_TOKENS, (tb + 1) * GATHER_TOKENS)
        onehot = _window_hits(pos_ref[pl.ds(e, 1), toks], start, first_slot).astype(BF16)
        acc_ref[pl.ds(start, GATHER_SLOTS), :] += _dot(onehot, h_ref[0, toks, :])

    windows = [_slot_windows(cnt_ref, row, tb, cap) for tb in range(n_tb)]
    for tb, (start, _) in enumerate(windows):
        add_window(tb, start)
    for tb, (start, n_win) in enumerate(windows):
        def extra(k, carry, tb=tb, start=start):
            first = start + k * GATHER_SLOTS
            add_window(tb, pl.multiple_of(jnp.minimum(first, cap - GATHER_SLOTS), 8), first)
            return carry
        lax.fori_loop(1, n_win, extra, 0)
    xs_ref[0] = acc_ref[...].astype(BF16)


def _gather_big(h2, pos, cnt_flat, cap):
    b, l, d = h2.shape
    grid_spec = pltpu.PrefetchScalarGridSpec(
        num_scalar_prefetch=1,
        grid=(b, N_EXPERTS),
        in_specs=[pl.BlockSpec((1, l, d), lambda bi, e, c: (bi, 0, 0)),
                  pl.BlockSpec((N_EXPERTS, l), lambda bi, e, c: (bi, 0))],
        out_specs=pl.BlockSpec((1, cap, d), lambda bi, e, c: (e, bi, 0)),
        scratch_shapes=[pltpu.VMEM((cap, d), F32)],
    )
    return pl.pallas_call(
        functools.partial(_gather_big_kernel, cap=cap),
        grid_spec=grid_spec,
        out_shape=jax.ShapeDtypeStruct((N_EXPERTS, b * cap, d), BF16),
        compiler_params=_cparams(("arbitrary", "arbitrary")),
        name="gather_big",
    )(cnt_flat, h2, pos)


def _ffn_kernel(xa_ref, xb_ref, wg_ref, wu_ref, wd_ref, ya_ref, yb_ref):
    @pl.when(pl.program_id(1) == 0)
    def _():
        ya_ref[...] = jnp.zeros(ya_ref.shape, F32)
        yb_ref[...] = jnp.zeros(yb_ref.shape, F32)

    wg = wg_ref[0].astype(BF16)
    wu = wu_ref[0].astype(BF16)
    wd = wd_ref[0].astype(BF16)
    for x_ref, y_ref in ((xa_ref, ya_ref), (xb_ref, yb_ref)):
        m = x_ref.shape[1]
        step = min(FFN_ROWS, m)
        for r0 in range(0, m, step):
            x = x_ref[0, r0:r0 + step, :]
            hid = (_silu(_dot(x, wg)) * _dot(x, wu)).astype(BF16)
            y_ref[0, r0:r0 + step, :] += _dot(hid, wd)


def _ffn(xs_a, xs_b, w_gate, w_up, w_down):
    e, ma, d = xs_a.shape
    mb = xs_b.shape[1]
    ff = w_gate.shape[2]
    nf = ff // FF_TILE
    return pl.pallas_call(
        _ffn_kernel,
        grid=(e, nf),
        in_specs=[pl.BlockSpec((1, ma, d), lambda ei, f: (ei, 0, 0)),
                  pl.BlockSpec((1, mb, d), lambda ei, f: (ei, 0, 0)),
                  pl.BlockSpec((1, d, FF_TILE), lambda ei, f: (ei, 0, f)),
                  pl.BlockSpec((1, d, FF_TILE), lambda ei, f: (ei, 0, f)),
                  pl.BlockSpec((1, FF_TILE, d), lambda ei, f: (ei, f, 0))],
        out_specs=[pl.BlockSpec((1, ma, d), lambda ei, f: (ei, 0, 0)),
                   pl.BlockSpec((1, mb, d), lambda ei, f: (ei, 0, 0))],
        out_shape=[jax.ShapeDtypeStruct((e, ma, d), F32), jax.ShapeDtypeStruct((e, mb, d), F32)],
        compiler_params=_cparams(("arbitrary", "arbitrary")),
        name="ffn",
    )(xs_a, xs_b, w_gate, w_up, w_down)


def _slot_gate(hit, w_row):
    return jnp.sum(jnp.where(hit, w_row, 0.0), axis=1, keepdims=True)


def _finish(x1, moe, gate2, fn_g):
    x = x1 + gate2 * moe
    return (x * lax.rsqrt(jnp.mean(x * x, axis=-1, keepdims=True) + EPS)) * fn_g


def _combine_small_kernel(x1_ref, y_ref, pos_ref, w_ref, mod_ref, fn_ref, o_ref, *, cap):
    for req in range(x1_ref.shape[0]):
        hits = _stacked_hits(pos_ref, req, cap)
        gated = [(y_ref[e, req * cap:(req + 1) * cap, :]
                  * _slot_gate(hits[e], w_ref[req * N_EXPERTS + e:req * N_EXPERTS + e + 1, :])).astype(BF16)
                 for e in range(N_EXPERTS)]
        onehot = jnp.concatenate([h.astype(BF16) for h in hits], axis=0)
        moe = _dot_tn(onehot, jnp.concatenate(gated, axis=0))
        o_ref[req] = _finish(x1_ref[req], moe, mod_ref[req, 5:6], fn_ref[...])


def _combine_small(x1, y, pos, wsel, mod, fn_g, cap):
    b, l, d = x1.shape
    nr = SMALL_REQS if b % SMALL_REQS == 0 else 1
    return pl.pallas_call(
        functools.partial(_combine_small_kernel, cap=cap),
        grid=(b // nr,),
        in_specs=[pl.BlockSpec((nr, l, d), lambda bi: (bi, 0, 0)),
                  pl.BlockSpec((N_EXPERTS, nr * cap, d), lambda bi: (0, bi, 0)),
                  pl.BlockSpec((nr * N_EXPERTS, l), lambda bi: (bi, 0)),
                  pl.BlockSpec((nr * N_EXPERTS, l), lambda bi: (bi, 0)),
                  pl.BlockSpec((nr, 8, d), lambda bi: (bi, 0, 0)),
                  pl.BlockSpec((1, d), lambda bi: (0, 0))],
        out_specs=pl.BlockSpec((nr, l, d), lambda bi: (bi, 0, 0)),
        out_shape=jax.ShapeDtypeStruct((b, l, d), F32),
        compiler_params=_cparams(("arbitrary",)),
        name="combine_small",
    )(x1, y, pos, wsel, mod, fn_g)


def _combine_big_kernel(cnt_ref, x1_ref, y_ref, pos_ref, w_ref, mod_ref, fn_ref, o_ref, *, cap, span):
    bi, tg, ep = pl.program_id(0), pl.program_id(1), pl.program_id(2)
    n_tl = span // GATHER_TOKENS

    @pl.when(ep == 0)
    def _():
        o_ref[...] = jnp.zeros(o_ref.shape, F32)

    def window_terms(el, tl, start, first_slot=None):
        e = ep * EXPERT_PAIR + el
        toks = slice(tl * GATHER_TOKENS, (tl + 1) * GATHER_TOKENS)
        hit = _window_hits(pos_ref[pl.ds(e, 1), toks], start, first_slot)
        y = y_ref[el, pl.ds(start, GATHER_SLOTS), :]
        return hit.astype(BF16), (y * _slot_gate(hit, w_ref[pl.ds(e, 1), toks])).astype(BF16)

    windows = {}
    for tl in range(n_tl):
        toks = slice(tl * GATHER_TOKENS, (tl + 1) * GATHER_TOKENS)
        terms = []
        for el in range(EXPERT_PAIR):
            row = bi * N_EXPERTS + ep * EXPERT_PAIR + el
            windows[tl, el] = _slot_windows(cnt_ref, row, tg * n_tl + tl, cap)
            terms.append(window_terms(el, tl, windows[tl, el][0]))
        o_ref[0, toks, :] += _dot_tn(jnp.concatenate([t[0] for t in terms], axis=0),
                                     jnp.concatenate([t[1] for t in terms], axis=0))
    for (tl, el), (start, n_win) in windows.items():
        def extra(k, carry, tl=tl, el=el, start=start):
            first = start + k * GATHER_SLOTS
            hit, gated = window_terms(el, tl, pl.multiple_of(jnp.minimum(first, cap - GATHER_SLOTS), 8), first)
            o_ref[0, tl * GATHER_TOKENS:(tl + 1) * GATHER_TOKENS, :] += _dot_tn(hit, gated)
            return carry
        lax.fori_loop(1, n_win, extra, 0)

    @pl.when(ep == N_EXPERTS // EXPERT_PAIR - 1)
    def _():
        o_ref[0] = _finish(x1_ref[0], o_ref[0], mod_ref[0, 5:6], fn_ref[...])


def _combine_big(x1, y, pos, wsel, cnt_flat, mod, fn_g, cap):
    b, l, d = x1.shape
    span = min(l, 2048)
    grid_spec = pltpu.PrefetchScalarGridSpec(
        num_scalar_prefetch=1,
        grid=(b, l // span, N_EXPERTS // EXPERT_PAIR),
        in_specs=[pl.BlockSpec((1, span, d), lambda bi, tg, e, c: (bi, tg, 0)),
                  pl.BlockSpec((EXPERT_PAIR, cap, d), lambda bi, tg, e, c: (e, bi, 0)),
                  pl.BlockSpec((N_EXPERTS, span), lambda bi, tg, e, c: (bi, tg)),
                  pl.BlockSpec((N_EXPERTS, span), lambda bi, tg, e, c: (bi, tg)),
                  pl.BlockSpec((1, 8, d), lambda bi, tg, e, c: (bi, 0, 0)),
                  pl.BlockSpec((1, d), lambda bi, tg, e, c: (0, 0))],
        out_specs=pl.BlockSpec((1, span, d), lambda bi, tg, e, c: (bi, tg, 0)),
    )
    return pl.pallas_call(
        functools.partial(_combine_big_kernel, cap=cap, span=span),
        grid_spec=grid_spec,
        out_shape=jax.ShapeDtypeStruct((b, l, d), F32),
        compiler_params=_cparams(("arbitrary", "arbitrary", "arbitrary")),
        name="combine_big",
    )(cnt_flat, x1, y, pos, wsel, mod, fn_g)


def _router_parts(w):
    hi = w.astype(BF16)
    lo = (w - hi.astype(F32)).astype(BF16)
    pad = ((0, 0), (0, ROUTER_LANES - w.shape[1]))
    return jnp.concatenate([jnp.pad(hi, pad), jnp.pad(lo, pad)], axis=1)


def _rope_tables(l):
    rows = l // GRID_W
    row = jnp.repeat(jnp.arange(rows, dtype=F32), GRID_W)
    col = jnp.tile(jnp.arange(GRID_W, dtype=F32), rows)
    nf = DK // 4
    inv = ROPE_BASE ** (-jnp.arange(nf, dtype=F32) / nf)
    ang = jnp.concatenate([row[:, None] * inv, col[:, None] * inv], axis=-1)
    cos = jnp.repeat(jnp.cos(ang), 2, axis=-1)
    sin = jnp.repeat(jnp.sin(ang), 2, axis=-1) * jnp.tile(jnp.asarray([-1.0, 1.0], F32), DK // 2)
    return cos, sin


def kernel(x_prompt, x_sample, c, state_ret, c_ctx, w_ada, b_ada, norm1_g, w_in, w_pool, pool_scale, w_pool_out,
           ret_decay, ret_gn_g, w_ret_out, w_branch_gate, b_branch_gate, w_out, norm2_g, w_router, w_exp_gate,
           w_exp_up, w_exp_down, final_norm_g):
    depth = w_ada.shape[0]
    assert depth == 1, "single trunk layer"
    bc, lc, d = x_prompt.shape
    bl, ll, _ = x_sample.shape
    assert lc % CHUNK == 0 and STEP_ROWS % lc == 0 and ll % STEP_ROWS == 0 and ll % GATHER_TOKENS == 0
    cap_c = max(1, CAPACITY_FACTOR * lc // N_EXPERTS)
    cap_l = max(1, CAPACITY_FACTOR * ll // N_EXPERTS)
    assert cap_l % GATHER_SLOTS == 0 and ll // GATHER_TOKENS < COUNT_LANES

    cond = jnp.zeros((8 * pl.cdiv(bl + 1, 8), d), F32).at[:bl].set(c).at[bl].set(c_ctx)
    ada = _adaln(cond, w_ada[0], b_ada[0]).reshape(-1, 6, d)
    ada = jnp.pad(ada, ((0, 0), (0, 2), (0, 0)))
    mod_l = ada[:bl]
    mod_c = jnp.broadcast_to(ada[bl:bl + 1], (bc, 8, d))

    p = dict(
        n1g=norm1_g[0][None], w_in=w_in[0].astype(BF16), w_bg=w_branch_gate[0].astype(BF16),
        b_bg=b_branch_gate[0][None], w_pool=w_pool[0].astype(BF16), pool_scale=pool_scale[0][None],
        w_pool_out=w_pool_out[0].astype(BF16), gn_g=ret_gn_g[0][None], w_ret_out=w_ret_out[0].astype(BF16),
        w_out=w_out[0].astype(BF16), n2g=norm2_g[0][None], w_router_parts=_router_parts(w_router[0]),
    )
    decay = ret_decay[0]
    fn_g = final_norm_g[None]

    x1_c, h2_c, aff_c, st_c = _mix(x_prompt, mod_c, None, None, decay, p)
    rope = _rope_tables(ll)
    sb, kv = _revscan(x_sample, mod_l, rope[0], rope[1], state_ret[:, 0, 1], decay, p["n1g"], p["w_in"])
    x1_l, h2_l, aff_l = _mix(x_sample, mod_l, rope, (state_ret[:, 0, 0], sb, kv), decay, p)

    wsel_c, pos_c, _ = _route(aff_c.reshape(bc * N_EXPERTS, lc), cap_c, min(ROUTE_ROWS, bc * N_EXPERTS))
    wsel_l, pos_l, cnt_l = _route(aff_l.reshape(bl * N_EXPERTS, ll), cap_l, min(ROUTE_ROWS, bl * N_EXPERTS))
    cnt_flat = cnt_l.reshape(-1)

    xs_c = _gather_small(h2_c, pos_c, cap_c)
    xs_l = _gather_big(h2_l, pos_l, cnt_flat, cap_l)
    y_c, y_l = _ffn(xs_c, xs_l, w_exp_gate[0], w_exp_up[0], w_exp_down[0])

    y_prompt = _combine_small(x1_c, y_c, pos_c, wsel_c, mod_c, fn_g, cap_c)
    y_sample = _combine_big(x1_l, y_l, pos_l, wsel_l, cnt_flat, mod_l, fn_g, cap_l)
    new_state = st_c[:, None].astype(x_prompt.dtype)
    return (y_prompt, y_sample, new_state)
```

```python
import functools

import jax
import jax.numpy as jnp
from jax import lax
from jax.experimental import pallas as pl
from jax.experimental.pallas import tpu as pltpu

F32 = jnp.float32
BF16 = jnp.bfloat16
HIGHEST = lax.Precision.HIGHEST

N_HEADS = 4
DK = 128
DV = 256
CHUNK = 128
POOL_WINDOWS = (2, 4, 8, 16)
POOL_GROUP = 128
POOL_WIDTH = POOL_GROUP * len(POOL_WINDOWS)
QK_WIDTH = N_HEADS * DK
V_WIDTH = N_HEADS * DV
N_EXPERTS = 16
CAPACITY_FACTOR = 2
GRID_W = 64
ROPE_BASE = 10000.0
EPS = 1e-6

STEP_ROWS = 512
ROUTER_LANES = 128
HALO = 16
ROUTE_LANES = 256
ROUTE_ROWS = 512
SMALL_REQS = 4
GATHER_TOKENS = 512
GATHER_SLOTS = 128
COUNT_LANES = 128
SLOT_ALIGN = 16
FF_TILE = 256
FFN_ROWS = 512
VMEM_LIMIT = 56 * 1024 * 1024


def _cparams(sem):
    return pltpu.CompilerParams(dimension_semantics=sem, vmem_limit_bytes=VMEM_LIMIT)


def _resident(shape):
    return pl.BlockSpec(shape, lambda *_: (0,) * len(shape), pipeline_mode=pl.Buffered(1))


def _sigmoid(x):
    return 0.5 * jnp.tanh(0.5 * x) + 0.5


def _silu(x):
    return x * _sigmoid(x)


def _norm_mod(x, g, scale, shift):
    y = x * lax.rsqrt(jnp.mean(x * x, axis=-1, keepdims=True) + EPS)
    return (y * g) * (1.0 + scale) + shift


def _dot(a, b):
    return jnp.dot(a, b, preferred_element_type=F32)


def _dot_nt(a, b, precision=None):
    return lax.dot_general(a, b, (((1,), (1,)), ((), ())), preferred_element_type=F32, precision=precision)


def _dot_tn(a, b):
    return lax.dot_general(a, b, (((0,), (0,)), ((), ())), preferred_element_type=F32)


def _adaln_kernel(c_ref, w_ref, b_ref, o_ref):
    o_ref[...] = jnp.dot(_silu(c_ref[...]), w_ref[...], preferred_element_type=F32, precision=HIGHEST) + b_ref[...]


def _adaln(cond, w, b):
    rows, d = cond.shape
    n = w.shape[1]
    tn = d
    return pl.pallas_call(
        _adaln_kernel,
        grid=(n // tn,),
        in_specs=[pl.BlockSpec((rows, d), lambda i: (0, 0)),
                  pl.BlockSpec((d, tn), lambda i: (0, i)),
                  pl.BlockSpec((1, tn), lambda i: (0, i))],
        out_specs=pl.BlockSpec((rows, tn), lambda i: (0, i)),
        out_shape=jax.ShapeDtypeStruct((rows, n), F32),
        compiler_params=_cparams(("arbitrary",)),
        name="adaln",
    )(cond, w, b.reshape(1, n))


def _decay_tiles(decay_ref, head):
    lgf = -jnp.exp(jnp.full((CHUNK, DK), decay_ref[0, head], F32))
    lgb = -jnp.exp(jnp.full((CHUNK, DK), decay_ref[1, head], F32))
    i = lax.broadcasted_iota(jnp.int32, (CHUNK, DK), 0).astype(F32)
    m = lax.broadcasted_iota(jnp.int32, (CHUNK, DK), 1).astype(F32)
    diff = i - m
    return dict(
        mask=jnp.where(diff >= 0.0, jnp.exp(lgf * jnp.maximum(diff, 0.0)), jnp.exp(lgb * jnp.maximum(-diff, 0.0))),
        qdf=jnp.exp(lgf * (i + 1.0)),
        qdb=jnp.exp(lgb * (CHUNK - i)),
        kdf=jnp.exp(lgf * (CHUNK - 1.0 - i)),
        kdb=jnp.exp(lgb * i),
        cdf=jnp.exp(lgf * float(CHUNK)),
        cdb=jnp.exp(lgb * float(CHUNK)),
    )


def _wide(t):
    return jnp.concatenate([t, t], axis=1)


def _rope(x, cos, sin):
    even = (lax.broadcasted_iota(jnp.int32, x.shape, 1) % 2) == 0
    partner = jnp.where(even, pltpu.roll(x, x.shape[1] - 1, 1), pltpu.roll(x, 1, 1))
    return x * cos + partner * sin


def _revscan_kernel(x_ref, mod_ref, cos_ref, sin_ref, s0_ref, decay_ref, n1_ref, w_in_ref, sb_ref, kv_ref, s_scr):
    j = pl.program_id(1)

    @pl.when(j == 0)
    def _():
        s_scr[...] = s0_ref[0]

    mod = mod_ref[0]
    h = _norm_mod(x_ref[0], n1_ref[...], mod[1:2], mod[0:1]).astype(BF16)
    k_lo = POOL_WIDTH + QK_WIDTH
    kv = _dot(h, w_in_ref[:, k_lo:k_lo + QK_WIDTH + V_WIDTH])
    sb_ref[0, 0] = s_scr[...]
    kv_ref[0, :, QK_WIDTH:] = kv[:, QK_WIDTH:].astype(BF16)
    for head in range(N_HEADS):
        t = _decay_tiles(decay_ref, head)
        s = s_scr[head]
        for c in reversed(range(STEP_ROWS // CHUNK)):
            rows = slice(c * CHUNK, (c + 1) * CHUNK)
            cols = slice(head * DK, (head + 1) * DK)
            k = _rope(kv[rows, cols] * (DK ** -0.5), cos_ref[rows], sin_ref[rows])
            kv_ref[0, rows, cols] = k.astype(BF16)
            v = kv[rows, QK_WIDTH + head * DV:QK_WIDTH + (head + 1) * DV]
            s = _wide(t["cdb"]) * s + _dot_tn((k * t["kdb"]).astype(BF16), v.astype(BF16))
        s_scr[head] = s


def _revscan(x, mod, cos, sin, s0_b, decay, n1g, w_in):
    b, l, d = x.shape
    nblk = l // STEP_ROWS
    return pl.pallas_call(
        _revscan_kernel,
        grid=(b, nblk),
        in_specs=[pl.BlockSpec((1, STEP_ROWS, d), lambda bi, j: (bi, nblk - 1 - j, 0)),
                  pl.BlockSpec((1, 8, d), lambda bi, j: (bi, 0, 0)),
                  pl.BlockSpec((STEP_ROWS, DK), lambda bi, j: (nblk - 1 - j, 0)),
                  pl.BlockSpec((STEP_ROWS, DK), lambda bi, j: (nblk - 1 - j, 0)),
                  pl.BlockSpec((1, N_HEADS, DK, DV), lambda bi, j: (bi, 0, 0, 0)),
                  pl.BlockSpec(memory_space=pltpu.SMEM),
                  _resident(n1g.shape),
                  _resident(w_in.shape)],
        out_specs=[pl.BlockSpec((1, 1, N_HEADS, DK, DV), lambda bi, j: (bi, nblk - 1 - j, 0, 0, 0)),
                   pl.BlockSpec((1, STEP_ROWS, QK_WIDTH + V_WIDTH), lambda bi, j: (bi, nblk - 1 - j, 0))],
        out_shape=[jax.ShapeDtypeStruct((b, nblk, N_HEADS, DK, DV), F32),
                   jax.ShapeDtypeStruct((b, l, QK_WIDTH + V_WIDTH), BF16)],
        scratch_shapes=[pltpu.VMEM((N_HEADS, DK, DV), F32)],
        compiler_params=_cparams(("arbitrary", "arbitrary")),
        name="revscan",
    )(x, mod, cos, sin, s0_b, decay, n1g, w_in)


def _mix_kernel(*refs, seq_len, n_seq, seq_rows, use_rope, has_state, emit_state):
    it = iter(refs)
    x_ref, xp_ref, xn_ref, mod_ref = next(it), next(it), next(it), next(it)
    cos_ref = sin_ref = s0f_ref = sb_ref = kv_ref = None
    if use_rope:
        cos_ref, sin_ref = next(it), next(it)
    if has_state:
        s0f_ref, sb_ref, kv_ref = next(it), next(it), next(it)
    (decay_ref, n1_ref, w_in_ref, w_bg_ref, b_bg_ref, w_pool_ref, pscale_ref, w_po_ref, gn_ref, w_ro_ref,
     w_out_ref, n2_ref, w_rt_ref) = (next(it) for _ in range(13))
    x1_ref, h2_ref, aff_ref = next(it), next(it), next(it)
    st_ref = next(it) if emit_state else None
    sf_scr = next(it)

    j = pl.program_id(1)
    d = x_ref.shape[2]
    n_chunks = seq_rows // CHUNK
    ext = seq_rows + 2 * HALO

    @pl.when(j == 0)
    def _():
        if has_state:
            sf_scr[...] = s0f_ref[...]
        else:
            sf_scr[...] = jnp.zeros(sf_scr.shape, F32)

    mods = [mod_ref[s] for s in range(n_seq)]

    def rows_of(vals):
        return jnp.concatenate([jnp.broadcast_to(v, (seq_rows, d)) for v in vals], axis=0) if n_seq > 1 else vals[0]

    he_parts = []
    for s in range(n_seq):
        xe = jnp.concatenate([xp_ref[s], x_ref[s], xn_ref[s]], axis=0)
        he_parts.append(_norm_mod(xe, n1_ref[...], mods[s][1:2], mods[s][0:1]).astype(BF16))
    he = jnp.concatenate(he_parts, axis=0) if n_seq > 1 else he_parts[0]
    hb_parts = [hp[HALO:HALO + seq_rows] for hp in he_parts]
    hb = jnp.concatenate(hb_parts, axis=0) if n_seq > 1 else hb_parts[0]
    x = jnp.concatenate([x_ref[s] for s in range(n_seq)], axis=0) if n_seq > 1 else x_ref[0]
    gates = _sigmoid(_dot(hb, w_bg_ref[...]) + b_bg_ref[...])

    ue_all = _dot(he, w_in_ref[:, :POOL_WIDTH])
    epos = j * seq_rows - HALO + lax.broadcasted_iota(jnp.int32, (ext, 1), 0)
    valid = (epos >= 0) & (epos < seq_len)
    tpos = j * seq_rows + lax.broadcasted_iota(jnp.int32, (seq_rows, 1), 0)
    pooled = [[] for _ in POOL_WINDOWS]
    for s in range(n_seq):
        ue = jnp.where(valid, ue_all[s * ext:(s + 1) * ext], 0.0)
        for gi, w in enumerate(POOL_WINDOWS):
            ug = ue[:, gi * POOL_GROUP:(gi + 1) * POOL_GROUP]
            acc, shift = ug, 1
            while shift < w:
                acc = acc + pltpu.roll(acc, shift, 0)
                shift *= 2
            if w // 2 > 1:
                acc = pltpu.roll(acc, ext - (w // 2 - 1), 0)
            cnt = (jnp.minimum(tpos + w // 2, seq_len) - jnp.maximum(tpos - w // 2, 0)).astype(F32)
            own = slice(HALO, HALO + seq_rows)
            pooled[gi].append((acc[own] / cnt - ug[own]).astype(BF16))
    pool_h = jnp.concatenate(
        [_dot(jnp.concatenate(pg, axis=0) if len(pg) > 1 else pg[0], w_pool_ref[gi]) for gi, pg in enumerate(pooled)],
        axis=1) * pscale_ref[...]
    pool_y = _dot(pool_h.astype(BF16), w_po_ref[...])

    g_lo = POOL_WIDTH + 2 * QK_WIDTH + V_WIDTH
    if has_state:
        zq = _dot(hb, w_in_ref[:, POOL_WIDTH:POOL_WIDTH + QK_WIDTH])
        g_all = _dot(hb, w_in_ref[:, g_lo:])
    else:
        zr = _dot(hb, w_in_ref[:, POOL_WIDTH:])
        zq, g_all = zr[:, :QK_WIDTH], zr[:, 2 * QK_WIDTH + V_WIDTH:]
    seq_out = []
    for s in range(n_seq):
        r0 = s * seq_rows
        head_out = []
        for head in range(N_HEADS):
            t = _decay_tiles(decay_ref, head)
            qs, ks, vs = [], [], []
            for c in range(n_chunks):
                rows = slice(r0 + c * CHUNK, r0 + (c + 1) * CHUNK)
                crow = slice(c * CHUNK, (c + 1) * CHUNK)
                q = zq[rows, head * DK:(head + 1) * DK]
                if use_rope:
                    q = _rope(q, cos_ref[crow], sin_ref[crow])
                if has_state:
                    k = kv_ref[s, crow, head * DK:(head + 1) * DK].astype(F32)
                    v = kv_ref[s, crow, QK_WIDTH + head * DV:QK_WIDTH + (head + 1) * DV]
                else:
                    k = zr[rows, QK_WIDTH + head * DK:QK_WIDTH + (head + 1) * DK] * (DK ** -0.5)
                    if use_rope:
                        k = _rope(k, cos_ref[crow], sin_ref[crow])
                    v = zr[rows, 2 * QK_WIDTH + head * DV:2 * QK_WIDTH + (head + 1) * DV].astype(BF16)
                qs.append(q)
                ks.append(k)
                vs.append(v)
            sf = [sf_scr[s, head]]
            for c in range(n_chunks):
                sf.append(_wide(t["cdf"]) * sf[c] + _dot_tn((ks[c] * t["kdf"]).astype(BF16), vs[c]))
            sf_scr[s, head] = sf[n_chunks]
            sb = [None] * (n_chunks + 1)
            sb[n_chunks] = sb_ref[s, 0, head] if has_state else jnp.zeros((DK, DV), F32)
            for c in reversed(range(n_chunks)):
                sb[c] = _wide(t["cdb"]) * sb[c + 1] + _dot_tn((ks[c] * t["kdb"]).astype(BF16), vs[c])
            if emit_state:
                st_ref[s, 0, head] = sf[n_chunks]
                st_ref[s, 1, head] = sb[0]
            outs = []
            for c in range(n_chunks):
                scores = _dot_nt(qs[c].astype(BF16), ks[c].astype(BF16)) * t["mask"]
                q_both = jnp.concatenate([(qs[c] * t["qdf"]).astype(BF16), (qs[c] * t["qdb"]).astype(BF16)], axis=1)
                s_both = jnp.concatenate([sf[c].astype(BF16), sb[c + 1].astype(BF16)], axis=0)
                outs.append(_dot(scores.astype(BF16), vs[c]) + _dot(q_both, s_both))
            o = jnp.concatenate(outs, axis=0)
            mu = jnp.mean(o, axis=-1, keepdims=True)
            oc = o - mu
            var = jnp.mean(oc * oc, axis=-1, keepdims=True)
            head_out.append(oc * lax.rsqrt(var + EPS))
        seq_out.append(jnp.concatenate(head_out, axis=1))
    o_n = (jnp.concatenate(seq_out, axis=0) if n_seq > 1 else seq_out[0]) * gn_ref[...]
    ret_y = _dot((_silu(g_all) * o_n).astype(BF16), w_ro_ref[...])

    merged = gates[:, :d] * pool_y + gates[:, d:] * ret_y
    x1 = x + rows_of([m[2:3] for m in mods]) * _dot(merged.astype(BF16), w_out_ref[...])

    h2 = _norm_mod(x1, n2_ref[...], rows_of([m[4:5] for m in mods]), rows_of([m[3:4] for m in mods]))
    h2_hi = h2.astype(BF16)
    h2_lo = (h2 - h2_hi.astype(F32)).astype(BF16)
    n_rows = n_seq * seq_rows
    parts = _dot(jnp.concatenate([h2_hi, h2_lo], axis=0), w_rt_ref[...])
    logits_t = (parts[:n_rows, :ROUTER_LANES] + parts[n_rows:, :ROUTER_LANES]
                + parts[:n_rows, ROUTER_LANES:])
    for s in range(n_seq):
        rows = slice(s * seq_rows, (s + 1) * seq_rows)
        x1_ref[s] = x1[rows]
        h2_ref[s] = h2_hi[rows]
        logits = logits_t[rows].T[:N_EXPERTS]
        e = jnp.exp(logits - jnp.max(logits, axis=0, keepdims=True))
        aff_ref[s] = e / jnp.sum(e, axis=0, keepdims=True)


def _mix(x, mod, rope, states, decay, p):
    b, l, d = x.shape
    seq_rows = min(l, STEP_ROWS)
    n_seq = STEP_ROWS // seq_rows
    nblk = l // seq_rows
    hb = seq_rows // HALO
    n_halo = l // HALO
    use_rope, has_state = rope is not None, states is not None
    emit_state = not has_state
    assert b % n_seq == 0 and (n_seq == 1 or not (use_rope or has_state))
    in_specs = [pl.BlockSpec((n_seq, seq_rows, d), lambda bi, j: (bi, j, 0)),
                pl.BlockSpec((n_seq, HALO, d), lambda bi, j: (bi, jnp.maximum(j * hb - 1, 0), 0)),
                pl.BlockSpec((n_seq, HALO, d), lambda bi, j: (bi, jnp.minimum((j + 1) * hb, n_halo - 1), 0)),
                pl.BlockSpec((n_seq, 8, d), lambda bi, j: (bi, 0, 0))]
    args = [x, x, x, mod]
    if use_rope:
        in_specs += [pl.BlockSpec((seq_rows, DK), lambda bi, j: (j, 0))] * 2
        args += list(rope)
    if has_state:
        in_specs += [pl.BlockSpec((1, N_HEADS, DK, DV), lambda bi, j: (bi, 0, 0, 0)),
                     pl.BlockSpec((1, 1, N_HEADS, DK, DV), lambda bi, j: (bi, j, 0, 0, 0)),
                     pl.BlockSpec((1, seq_rows, QK_WIDTH + V_WIDTH), lambda bi, j: (bi, j, 0))]
        args += list(states)
    weights = [p["n1g"], p["w_in"], p["w_bg"], p["b_bg"], p["w_pool"], p["pool_scale"], p["w_pool_out"],
               p["gn_g"], p["w_ret_out"], p["w_out"], p["n2g"], p["w_router_parts"]]
    in_specs.append(pl.BlockSpec(memory_space=pltpu.SMEM))
    args.append(decay)
    for w in weights:
        in_specs.append(_resident(w.shape))
        args.append(w)
    out_specs = [pl.BlockSpec((n_seq, seq_rows, d), lambda bi, j: (bi, j, 0)),
                 pl.BlockSpec((n_seq, seq_rows, d), lambda bi, j: (bi, j, 0)),
                 pl.BlockSpec((n_seq, N_EXPERTS, seq_rows), lambda bi, j: (bi, 0, j))]
    out_shape = [jax.ShapeDtypeStruct((b, l, d), F32), jax.ShapeDtypeStruct((b, l, d), BF16),
                 jax.ShapeDtypeStruct((b, N_EXPERTS, l), F32)]
    if emit_state:
        out_specs.append(pl.BlockSpec((n_seq, 2, N_HEADS, DK, DV), lambda bi, j: (bi, 0, 0, 0, 0)))
        out_shape.append(jax.ShapeDtypeStruct((b, 2, N_HEADS, DK, DV), F32))
    kern = functools.partial(_mix_kernel, seq_len=l, n_seq=n_seq, seq_rows=seq_rows, use_rope=use_rope,
                             has_state=has_state, emit_state=emit_state)
    return pl.pallas_call(
        kern,
        grid=(b // n_seq, nblk),
        in_specs=in_specs,
        out_specs=out_specs,
        out_shape=out_shape,
        scratch_shapes=[pltpu.VMEM((n_seq, N_HEADS, DK, DV), F32)],
        compiler_params=_cparams(("arbitrary", "arbitrary")),
        name="mix_rope" if use_rope else "mix",
    )(*args)


def _route_kernel(aff_ref, wsel_ref, pos_ref, cnt_ref, *, cap):
    aff = aff_ref[...]
    rows, l = aff.shape

    def as_value(bits):
        return pltpu.bitcast(bits, F32)

    def bisect(i, tau):
        cand = tau | jnp.left_shift(jnp.int32(1), 30 - i)
        cnt = jnp.sum((aff >= as_value(cand)).astype(F32), axis=1, keepdims=True)
        return jnp.where(cnt >= cap, cand, tau)

    tau = lax.fori_loop(0, 31, bisect, jnp.zeros((rows, 1), jnp.int32))
    gt = aff >= as_value(tau + 1)
    eq = (aff >= as_value(tau)) & jnp.logical_not(gt)
    need = cap - jnp.sum(gt.astype(F32), axis=1, keepdims=True)

    nlb = l // ROUTE_LANES
    tri = (lax.broadcasted_iota(jnp.int32, (ROUTE_LANES, ROUTE_LANES), 0)
           < lax.broadcasted_iota(jnp.int32, (ROUTE_LANES, ROUTE_LANES), 1)).astype(BF16)

    def prefix(mask):
        carry = jnp.zeros((rows, 1), F32)
        parts, starts = [], []
        for blk in range(nlb):
            mb = mask[:, blk * ROUTE_LANES:(blk + 1) * ROUTE_LANES].astype(F32)
            starts.append(carry)
            parts.append(_dot(mb.astype(BF16), tri) + carry)
            carry = carry + jnp.sum(mb, axis=1, keepdims=True)
        starts.append(carry)
        return jnp.concatenate(parts, axis=1), starts

    eq_rank, _ = prefix(eq)
    sel = gt | (eq & (eq_rank < need))
    pos, starts = prefix(sel)
    wsel_ref[...] = jnp.where(sel, aff, 0.0)
    pos_ref[...] = jnp.where(sel, pos.astype(jnp.int32), -1)
    lane = lax.broadcasted_iota(jnp.int32, (rows, COUNT_LANES), 1)
    table = jnp.zeros((rows, COUNT_LANES), jnp.int32)
    per = max(GATHER_TOKENS // ROUTE_LANES, 1)
    for tb in range(nlb // per + 1):
        table = jnp.where(lane == tb, starts[min(tb * per, nlb)].astype(jnp.int32), table)
    cnt_ref[...] = table


def _route(aff_rows, cap, row_block):
    r, l = aff_rows.shape
    spec = pl.BlockSpec((row_block, l), lambda i: (i, 0))
    return pl.pallas_call(
        functools.partial(_route_kernel, cap=cap),
        grid=(r // row_block,),
        in_specs=[spec],
        out_specs=[spec, spec, pl.BlockSpec((row_block, COUNT_LANES), lambda i: (i, 0))],
        out_shape=[jax.ShapeDtypeStruct((r, l), F32), jax.ShapeDtypeStruct((r, l), jnp.int32),
                   jax.ShapeDtypeStruct((r, COUNT_LANES), jnp.int32)],
        compiler_params=_cparams(("arbitrary",)),
        name=f"route_{l}",
    )(aff_rows)


def _stacked_hits(pos_ref, req, cap):
    l = pos_ref.shape[1]
    slot = lax.broadcasted_iota(jnp.int32, (cap, l), 0)
    return [pos_ref[req * N_EXPERTS + e:req * N_EXPERTS + e + 1, :] == slot for e in range(N_EXPERTS)]


def _gather_small_kernel(h_ref, pos_ref, xs_ref, *, cap):
    for req in range(h_ref.shape[0]):
        onehot = jnp.concatenate([h.astype(BF16) for h in _stacked_hits(pos_ref, req, cap)], axis=0)
        xs = _dot(onehot, h_ref[req]).astype(BF16)
        for e in range(N_EXPERTS):
            xs_ref[e, req * cap:(req + 1) * cap, :] = xs[e * cap:(e + 1) * cap]


def _gather_small(h2, pos, cap):
    b, l, d = h2.shape
    nr = SMALL_REQS if b % SMALL_REQS == 0 else 1
    return pl.pallas_call(
        functools.partial(_gather_small_kernel, cap=cap),
        grid=(b // nr,),
        in_specs=[pl.BlockSpec((nr, l, d), lambda bi: (bi, 0, 0)),
                  pl.BlockSpec((nr * N_EXPERTS, l), lambda bi: (bi, 0))],
        out_specs=pl.BlockSpec((N_EXPERTS, nr * cap, d), lambda bi: (0, bi, 0)),
        out_shape=jax.ShapeDtypeStruct((N_EXPERTS, b * cap, d), BF16),
        compiler_params=_cparams(("arbitrary",)),
        name="gather_small",
    )(h2, pos)


def _slot_windows(cnt_ref, row, tb, cap):
    lo = cnt_ref[row * COUNT_LANES + tb]
    hi = cnt_ref[row * COUNT_LANES + tb + 1]
    start = jnp.minimum(lo & jnp.int32(-SLOT_ALIGN), jnp.int32(cap - GATHER_SLOTS))
    n_win = lax.shift_right_logical(hi - start + jnp.int32(GATHER_SLOTS - 1),
                                    jnp.int32(GATHER_SLOTS.bit_length() - 1))
    return pl.multiple_of(start, SLOT_ALIGN), n_win


def _window_hits(pos_row, start, first_slot=None):
    slot = start + lax.broadcasted_iota(jnp.int32, (GATHER_SLOTS, pos_row.shape[1]), 0)
    hit = pos_row == slot
    return hit if first_slot is None else hit & (slot >= first_slot)


def _gather_big_kernel(cnt_ref, h_ref, pos_ref, xs_ref, acc_ref, *, cap):
    bi, e = pl.program_id(0), pl.program_id(1)
    n_tb = h_ref.shape[1] // GATHER_TOKENS
    row = bi * N_EXPERTS + e
    acc_ref[...] = jnp.zeros(acc_ref.shape, F32)

    def add_window(tb, start, first_slot=None):
        toks = slice(tb * GATHER_TOKENS, (tb + 1) * GATHER_TOKENS)
        onehot = _window_hits(pos_ref[pl.ds(e, 1), toks], start, first_slot).astype(BF16)
        acc_ref[pl.ds(start, GATHER_SLOTS), :] += _dot(onehot, h_ref[0, toks, :])

    windows = [_slot_windows(cnt_ref, row, tb, cap) for tb in range(n_tb)]
    for tb, (start, _) in enumerate(windows):
        add_window(tb, start)
    for tb, (start, n_win) in enumerate(windows):
        def extra(k, carry, tb=tb, start=start):
            first = start + k * GATHER_SLOTS
            add_window(tb, pl.multiple_of(jnp.minimum(first, cap - GATHER_SLOTS), SLOT_ALIGN), first)
            return carry
        lax.fori_loop(1, n_win, extra, 0)
    xs_ref[0] = acc_ref[...].astype(BF16)


def _gather_big(h2, pos, cnt_flat, cap):
    b, l, d = h2.shape
    grid_spec = pltpu.PrefetchScalarGridSpec(
        num_scalar_prefetch=1,
        grid=(b, N_EXPERTS),
        in_specs=[pl.BlockSpec((1, l, d), lambda bi, e, c: (bi, 0, 0)),
                  pl.BlockSpec((N_EXPERTS, l), lambda bi, e, c: (bi, 0))],
        out_specs=pl.BlockSpec((1, cap, d), lambda bi, e, c: (e, bi, 0)),
        scratch_shapes=[pltpu.VMEM((cap, d), F32)],
    )
    return pl.pallas_call(
        functools.partial(_gather_big_kernel, cap=cap),
        grid_spec=grid_spec,
        out_shape=jax.ShapeDtypeStruct((N_EXPERTS, b * cap, d), BF16),
        compiler_params=_cparams(("arbitrary", "arbitrary")),
        name="gather_big",
    )(cnt_flat, h2, pos)


def _ffn_kernel(xa_ref, xb_ref, wg_ref, wu_ref, wd_ref, ya_ref, yb_ref, acca_ref, accb_ref):
    f, nf = pl.program_id(1), pl.num_programs(1)

    def ff_tile(first, last):
        wg = wg_ref[0].astype(BF16)
        wu = wu_ref[0].astype(BF16)
        wd = wd_ref[0].astype(BF16)
        for x_ref, y_ref, acc_ref in ((xa_ref, ya_ref, acca_ref), (xb_ref, yb_ref, accb_ref)):
            m = x_ref.shape[1]
            step = min(FFN_ROWS, m)
            for r0 in range(0, m, step):
                rows = slice(r0, r0 + step)
                x = x_ref[0, rows, :]
                hid = (_silu(_dot(x, wg)) * _dot(x, wu)).astype(BF16)
                part = _dot(hid, wd)
                if first:
                    acc_ref[rows, :] = part
                elif last:
                    y_ref[0, rows, :] = (acc_ref[rows, :] + part).astype(BF16)
                else:
                    acc_ref[rows, :] += part

    pl.when(f == 0)(functools.partial(ff_tile, True, False))
    pl.when((f > 0) & (f < nf - 1))(functools.partial(ff_tile, False, False))
    pl.when(f == nf - 1)(functools.partial(ff_tile, False, True))


def _ffn(xs_a, xs_b, w_gate, w_up, w_down):
    e, ma, d = xs_a.shape
    mb = xs_b.shape[1]
    ff = w_gate.shape[2]
    nf = ff // FF_TILE
    assert ff % FF_TILE == 0 and nf >= 2
    return pl.pallas_call(
        _ffn_kernel,
        grid=(e, nf),
        in_specs=[pl.BlockSpec((1, ma, d), lambda ei, f: (ei, 0, 0)),
                  pl.BlockSpec((1, mb, d), lambda ei, f: (ei, 0, 0)),
                  pl.BlockSpec((1, d, FF_TILE), lambda ei, f: (ei, 0, f)),
                  pl.BlockSpec((1, d, FF_TILE), lambda ei, f: (ei, 0, f)),
                  pl.BlockSpec((1, FF_TILE, d), lambda ei, f: (ei, f, 0))],
        out_specs=[pl.BlockSpec((1, ma, d), lambda ei, f: (ei, 0, 0)),
                   pl.BlockSpec((1, mb, d), lambda ei, f: (ei, 0, 0))],
        out_shape=[jax.ShapeDtypeStruct((e, ma, d), BF16), jax.ShapeDtypeStruct((e, mb, d), BF16)],
        scratch_shapes=[pltpu.VMEM((ma, d), F32), pltpu.VMEM((mb, d), F32)],
        compiler_params=_cparams(("arbitrary", "arbitrary")),
        name="ffn",
    )(xs_a, xs_b, w_gate, w_up, w_down)


def _slot_gate(hit, w_row):
    return jnp.sum(jnp.where(hit, w_row, 0.0), axis=1, keepdims=True)


def _finish(x1, moe, gate2, fn_g):
    x = x1 + gate2 * moe
    return (x * lax.rsqrt(jnp.mean(x * x, axis=-1, keepdims=True) + EPS)) * fn_g


def _combine_small_kernel(x1_ref, y_ref, pos_ref, w_ref, mod_ref, fn_ref, o_ref, *, cap):
    for req in range(x1_ref.shape[0]):
        hits = _stacked_hits(pos_ref, req, cap)
        gated = [(y_ref[e, req * cap:(req + 1) * cap, :].astype(F32)
                  * _slot_gate(hits[e], w_ref[req * N_EXPERTS + e:req * N_EXPERTS + e + 1, :])).astype(BF16)
                 for e in range(N_EXPERTS)]
        onehot = jnp.concatenate([h.astype(BF16) for h in hits], axis=0)
        moe = _dot_tn(onehot, jnp.concatenate(gated, axis=0))
        o_ref[req] = _finish(x1_ref[req], moe, mod_ref[req, 5:6], fn_ref[...])


def _combine_small(x1, y, pos, wsel, mod, fn_g, cap):
    b, l, d = x1.shape
    nr = SMALL_REQS if b % SMALL_REQS == 0 else 1
    return pl.pallas_call(
        functools.partial(_combine_small_kernel, cap=cap),
        grid=(b // nr,),
        in_specs=[pl.BlockSpec((nr, l, d), lambda bi: (bi, 0, 0)),
                  pl.BlockSpec((N_EXPERTS, nr * cap, d), lambda bi: (0, bi, 0)),
                  pl.BlockSpec((nr * N_EXPERTS, l), lambda bi: (bi, 0)),
                  pl.BlockSpec((nr * N_EXPERTS, l), lambda bi: (bi, 0)),
                  pl.BlockSpec((nr, 8, d), lambda bi: (bi, 0, 0)),
                  pl.BlockSpec((1, d), lambda bi: (0, 0))],
        out_specs=pl.BlockSpec((nr, l, d), lambda bi: (bi, 0, 0)),
        out_shape=jax.ShapeDtypeStruct((b, l, d), F32),
        compiler_params=_cparams(("arbitrary",)),
        name="combine_small",
    )(x1, y, pos, wsel, mod, fn_g)


def _combine_big_kernel(cnt_ref, x1_ref, y_ref, pos_ref, w_ref, mod_ref, fn_ref, o_ref, acc_ref, *, cap):
    bi, tb = pl.program_id(0), pl.program_id(1)

    def window_terms(e, start, first_slot=None):
        hit = _window_hits(pos_ref[e:e + 1, :], start, first_slot)
        y = y_ref[e, pl.ds(start, GATHER_SLOTS), :].astype(F32)
        return hit.astype(BF16), (y * _slot_gate(hit, w_ref[e:e + 1, :])).astype(BF16)

    windows = [_slot_windows(cnt_ref, bi * N_EXPERTS + e, tb, cap) for e in range(N_EXPERTS)]
    terms = [window_terms(e, start) for e, (start, _) in enumerate(windows)]
    acc_ref[...] = _dot_tn(jnp.concatenate([t[0] for t in terms], axis=0),
                           jnp.concatenate([t[1] for t in terms], axis=0))
    for e, (start, n_win) in enumerate(windows):
        def extra(k, carry, e=e, start=start):
            first = start + k * GATHER_SLOTS
            clamped = pl.multiple_of(jnp.minimum(first, cap - GATHER_SLOTS), SLOT_ALIGN)
            hit, gated = window_terms(e, clamped, first)
            acc_ref[...] += _dot_tn(hit, gated)
            return carry
        lax.fori_loop(1, n_win, extra, 0)
    o_ref[0] = _finish(x1_ref[0], acc_ref[...], mod_ref[0, 5:6], fn_ref[...])


def _combine_big(x1, y, pos, wsel, cnt_flat, mod, fn_g, cap):
    b, l, d = x1.shape
    grid_spec = pltpu.PrefetchScalarGridSpec(
        num_scalar_prefetch=1,
        grid=(b, l // GATHER_TOKENS),
        in_specs=[pl.BlockSpec((1, GATHER_TOKENS, d), lambda bi, tb, c: (bi, tb, 0)),
                  pl.BlockSpec((N_EXPERTS, cap, d), lambda bi, tb, c: (0, bi, 0)),
                  pl.BlockSpec((N_EXPERTS, GATHER_TOKENS), lambda bi, tb, c: (bi, tb)),
                  pl.BlockSpec((N_EXPERTS, GATHER_TOKENS), lambda bi, tb, c: (bi, tb)),
                  pl.BlockSpec((1, 8, d), lambda bi, tb, c: (bi, 0, 0)),
                  pl.BlockSpec((1, d), lambda bi, tb, c: (0, 0))],
        out_specs=pl.BlockSpec((1, GATHER_TOKENS, d), lambda bi, tb, c: (bi, tb, 0)),
        scratch_shapes=[pltpu.VMEM((GATHER_TOKENS, d), F32)],
    )
    return pl.pallas_call(
        functools.partial(_combine_big_kernel, cap=cap),
        grid_spec=grid_spec,
        out_shape=jax.ShapeDtypeStruct((b, l, d), F32),
        compiler_params=_cparams(("arbitrary", "arbitrary")),
        name="combine_big",
    )(cnt_flat, x1, y, pos, wsel, mod, fn_g)


def _router_parts(w):
    hi = w.astype(BF16)
    lo = (w - hi.astype(F32)).astype(BF16)
    pad = ((0, 0), (0, ROUTER_LANES - w.shape[1]))
    return jnp.concatenate([jnp.pad(hi, pad), jnp.pad(lo, pad)], axis=1)


def _rope_tables(l):
    rows = l // GRID_W
    row = jnp.repeat(jnp.arange(rows, dtype=F32), GRID_W)
    col = jnp.tile(jnp.arange(GRID_W, dtype=F32), rows)
    nf = DK // 4
    inv = ROPE_BASE ** (-jnp.arange(nf, dtype=F32) / nf)
    ang = jnp.concatenate([row[:, None] * inv, col[:, None] * inv], axis=-1)
    cos = jnp.repeat(jnp.cos(ang), 2, axis=-1)
    sin = jnp.repeat(jnp.sin(ang), 2, axis=-1) * jnp.tile(jnp.asarray([-1.0, 1.0], F32), DK // 2)
    return cos, sin


def kernel(x_prompt, x_sample, c, state_ret, c_ctx, w_ada, b_ada, norm1_g, w_in, w_pool, pool_scale, w_pool_out,
           ret_decay, ret_gn_g, w_ret_out, w_branch_gate, b_branch_gate, w_out, norm2_g, w_router, w_exp_gate,
           w_exp_up, w_exp_down, final_norm_g):
    depth = w_ada.shape[0]
    assert depth == 1, "single trunk layer"
    bc, lc, d = x_prompt.shape
    bl, ll, _ = x_sample.shape
    assert lc % CHUNK == 0 and STEP_ROWS % lc == 0 and ll % STEP_ROWS == 0 and ll % GATHER_TOKENS == 0
    cap_c = max(1, CAPACITY_FACTOR * lc // N_EXPERTS)
    cap_l = max(1, CAPACITY_FACTOR * ll // N_EXPERTS)
    assert cap_l % GATHER_SLOTS == 0 and ll // GATHER_TOKENS < COUNT_LANES

    cond = jnp.zeros((8 * pl.cdiv(bl + 1, 8), d), F32).at[:bl].set(c).at[bl].set(c_ctx)
    ada = _adaln(cond, w_ada[0], b_ada[0]).reshape(-1, 6, d)
    ada = jnp.pad(ada, ((0, 0), (0, 2), (0, 0)))
    mod_l = ada[:bl]
    mod_c = jnp.broadcast_to(ada[bl:bl + 1], (bc, 8, d))

    p = dict(
        n1g=norm1_g[0][None], w_in=w_in[0].astype(BF16), w_bg=w_branch_gate[0].astype(BF16),
        b_bg=b_branch_gate[0][None], w_pool=w_pool[0].astype(BF16), pool_scale=pool_scale[0][None],
        w_pool_out=w_pool_out[0].astype(BF16), gn_g=ret_gn_g[0][None], w_ret_out=w_ret_out[0].astype(BF16),
        w_out=w_out[0].astype(BF16), n2g=norm2_g[0][None], w_router_parts=_router_parts(w_router[0]),
    )
    decay = ret_decay[0]
    fn_g = final_norm_g[None]

    x1_c, h2_c, aff_c, st_c = _mix(x_prompt, mod_c, None, None, decay, p)
    rope = _rope_tables(ll)
    sb, kv = _revscan(x_sample, mod_l, rope[0], rope[1], state_ret[:, 0, 1], decay, p["n1g"], p["w_in"])
    x1_l, h2_l, aff_l = _mix(x_sample, mod_l, rope, (state_ret[:, 0, 0], sb, kv), decay, p)

    wsel_c, pos_c, _ = _route(aff_c.reshape(bc * N_EXPERTS, lc), cap_c, min(ROUTE_ROWS, bc * N_EXPERTS))
    wsel_l, pos_l, cnt_l = _route(aff_l.reshape(bl * N_EXPERTS, ll), cap_l, min(ROUTE_ROWS, bl * N_EXPERTS))
    cnt_flat = cnt_l.reshape(-1)

    xs_c = _gather_small(h2_c, pos_c, cap_c)
    xs_l = _gather_big(h2_l, pos_l, cnt_flat, cap_l)
    y_c, y_l = _ffn(xs_c, xs_l, w_exp_gate[0], w_exp_up[0], w_exp_down[0])

    y_prompt = _combine_small(x1_c, y_c, pos_c, wsel_c, mod_c, fn_g, cap_c)
    y_sample = _combine_big(x1_l, y_l, pos_l, wsel_l, cnt_flat, mod_l, fn_g, cap_l)
    new_state = st_c[:, None].astype(x_prompt.dtype)
    return (y_prompt, y_sample, new_state)
```

```python
import functools

import jax
import jax.numpy as jnp
from jax import lax
from jax.experimental import pallas as pl
from jax.experimental.pallas import tpu as pltpu

F32 = jnp.float32
BF16 = jnp.bfloat16

N_HEADS = 4
DK = 128
DV = 256
CHUNK = 128
POOL_WINDOWS = (2, 4, 8, 16)
POOL_GROUP = 128
POOL_WIDTH = POOL_GROUP * len(POOL_WINDOWS)
QK_WIDTH = N_HEADS * DK
V_WIDTH = N_HEADS * DV
N_EXPERTS = 16
CAPACITY_FACTOR = 2
GRID_W = 64
ROPE_BASE = 10000.0
EPS = 1e-6

LANES = 128
STEP_ROWS = 512
HALO = 16
ROUTE_LANES = 256
ROUTE_ROWS = 512
SMALL_REQS = 4
GATHER_TOKENS = 512
GATHER_SLOTS = 128
GATHER_EXPERTS = 4
COUNT_LANES = 128
SLOT_ALIGN = 16
FF_TILE = 256
FFN_ROWS = 512
VMEM_LIMIT = 56 * 1024 * 1024


def _cparams(sem):
    return pltpu.CompilerParams(dimension_semantics=sem, vmem_limit_bytes=VMEM_LIMIT)


def _resident(shape):
    return pl.BlockSpec(shape, lambda *_: (0,) * len(shape), pipeline_mode=pl.Buffered(1))


def _sigmoid(x):
    return 0.5 * jnp.tanh(0.5 * x) + 0.5


def _silu(x):
    return x * _sigmoid(x)


def _norm_mod(x, g, scale, shift):
    y = x * lax.rsqrt(jnp.mean(x * x, axis=-1, keepdims=True) + EPS)
    return (y * g) * (1.0 + scale) + shift


def _hi_lo(x):
    hi = x.astype(BF16)
    return hi, (x - hi.astype(F32)).astype(BF16)


def _dot(a, b):
    return jnp.dot(a, b, preferred_element_type=F32)


def _dot_nt(a, b):
    return lax.dot_general(a, b, (((1,), (1,)), ((), ())), preferred_element_type=F32)


def _dot_tn(a, b):
    return lax.dot_general(a, b, (((0,), (0,)), ((), ())), preferred_element_type=F32)


def _adaln_kernel(c_ref, w_ref, b_ref, o_ref):
    rows = c_ref.shape[0]
    s_hi, s_lo = _hi_lo(_silu(c_ref[...]))
    w_hi, w_lo = _hi_lo(w_ref[...])
    by_hi = _dot(jnp.concatenate([s_hi, s_lo], axis=0), w_hi)
    o_ref[...] = by_hi[:rows] + by_hi[rows:] + _dot(s_hi, w_lo) + b_ref[...]


def _adaln(cond, w, b):
    rows, d = cond.shape
    n = w.shape[1]
    tn = d
    return pl.pallas_call(
        _adaln_kernel,
        grid=(n // tn,),
        in_specs=[pl.BlockSpec((rows, d), lambda i: (0, 0)),
                  pl.BlockSpec((d, tn), lambda i: (0, i)),
                  pl.BlockSpec((1, tn), lambda i: (0, i))],
        out_specs=pl.BlockSpec((rows, tn), lambda i: (0, i)),
        out_shape=jax.ShapeDtypeStruct((rows, n), F32),
        compiler_params=_cparams(("arbitrary",)),
        name="adaln",
    )(cond, w, b.reshape(1, n))


def _decay_tiles(decay_ref, head):
    lgf = -jnp.exp(jnp.full((CHUNK, DK), decay_ref[0, head], F32))
    lgb = -jnp.exp(jnp.full((CHUNK, DK), decay_ref[1, head], F32))
    i = lax.broadcasted_iota(jnp.int32, (CHUNK, DK), 0).astype(F32)
    m = lax.broadcasted_iota(jnp.int32, (CHUNK, DK), 1).astype(F32)
    diff = i - m
    return dict(
        mask=jnp.where(diff >= 0.0, jnp.exp(lgf * jnp.maximum(diff, 0.0)), jnp.exp(lgb * jnp.maximum(-diff, 0.0))),
        qdf=jnp.exp(lgf * (i + 1.0)),
        qdb=jnp.exp(lgb * (CHUNK - i)),
        kdf=jnp.exp(lgf * (CHUNK - 1.0 - i)),
        kdb=jnp.exp(lgb * i),
        cdf=jnp.exp(lgf * float(CHUNK)),
        cdb=jnp.exp(lgb * float(CHUNK)),
    )


def _wide(t):
    return jnp.concatenate([t, t], axis=1)


def _rope(x, cos, sin):
    even = (lax.broadcasted_iota(jnp.int32, x.shape, 1) % 2) == 0
    partner = jnp.where(even, pltpu.roll(x, x.shape[1] - 1, 1), pltpu.roll(x, 1, 1))
    return x * cos + partner * sin


def _state_spec(direction):
    return pl.BlockSpec((1, 1, 1, N_HEADS, DK, DV), lambda bi, j: (bi, 0, direction, 0, 0, 0))


def _revscan_kernel(x_ref, mod_ref, cos_ref, sin_ref, s0_ref, decay_ref, n1_ref, w_in_ref, sb_ref, kv_ref, s_scr):
    j = pl.program_id(1)

    @pl.when(j == 0)
    def _():
        s_scr[...] = s0_ref[0, 0, 0]

    mod = mod_ref[0]
    h = _norm_mod(x_ref[0], n1_ref[...], mod[1:2], mod[0:1]).astype(BF16)
    k_lo = POOL_WIDTH + QK_WIDTH
    kv = _dot(h, w_in_ref[:, k_lo:k_lo + QK_WIDTH + V_WIDTH])
    sb_ref[0, 0] = s_scr[...]
    kv_ref[0, :, QK_WIDTH:] = kv[:, QK_WIDTH:].astype(BF16)
    for head in range(N_HEADS):
        t = _decay_tiles(decay_ref, head)
        s = s_scr[head]
        for c in reversed(range(STEP_ROWS // CHUNK)):
            rows = slice(c * CHUNK, (c + 1) * CHUNK)
            cols = slice(head * DK, (head + 1) * DK)
            k = _rope(kv[rows, cols] * (DK ** -0.5), cos_ref[rows], sin_ref[rows])
            kv_ref[0, rows, cols] = k.astype(BF16)
            v = kv[rows, QK_WIDTH + head * DV:QK_WIDTH + (head + 1) * DV]
            s = _wide(t["cdb"]) * s + _dot_tn((k * t["kdb"]).astype(BF16), v.astype(BF16))
        s_scr[head] = s


def _revscan(x, mod, cos, sin, state_ret, decay, n1g, w_in):
    b, l, d = x.shape
    nblk = l // STEP_ROWS
    return pl.pallas_call(
        _revscan_kernel,
        grid=(b, nblk),
        in_specs=[pl.BlockSpec((1, STEP_ROWS, d), lambda bi, j: (bi, nblk - 1 - j, 0)),
                  pl.BlockSpec((1, 8, d), lambda bi, j: (bi, 0, 0)),
                  pl.BlockSpec((STEP_ROWS, DK), lambda bi, j: (nblk - 1 - j, 0)),
                  pl.BlockSpec((STEP_ROWS, DK), lambda bi, j: (nblk - 1 - j, 0)),
                  _state_spec(1),
                  pl.BlockSpec(memory_space=pltpu.SMEM),
                  _resident(n1g.shape),
                  _resident(w_in.shape)],
        out_specs=[pl.BlockSpec((1, 1, N_HEADS, DK, DV), lambda bi, j: (bi, nblk - 1 - j, 0, 0, 0)),
                   pl.BlockSpec((1, STEP_ROWS, QK_WIDTH + V_WIDTH), lambda bi, j: (bi, nblk - 1 - j, 0))],
        out_shape=[jax.ShapeDtypeStruct((b, nblk, N_HEADS, DK, DV), F32),
                   jax.ShapeDtypeStruct((b, l, QK_WIDTH + V_WIDTH), BF16)],
        scratch_shapes=[pltpu.VMEM((N_HEADS, DK, DV), F32)],
        compiler_params=_cparams(("arbitrary", "arbitrary")),
        name="revscan",
    )(x, mod, cos, sin, state_ret, decay, n1g, w_in)


def _mix_kernel(*refs, seq_len, n_seq, seq_rows, use_rope, has_state, emit_state):
    it = iter(refs)
    x_ref, xp_ref, xn_ref, mod_ref = next(it), next(it), next(it), next(it)
    cos_ref = sin_ref = s0f_ref = sb_ref = kv_ref = None
    if use_rope:
        cos_ref, sin_ref = next(it), next(it)
    if has_state:
        s0f_ref, sb_ref, kv_ref = next(it), next(it), next(it)
    (decay_ref, n1_ref, w_in_ref, w_bg_ref, b_bg_ref, w_pool_ref, pscale_ref, w_po_ref, gn_ref, w_ro_ref,
     w_out_ref, n2_ref, w_rt_ref) = (next(it) for _ in range(13))
    x1_ref, h2_ref, aff_ref = next(it), next(it), next(it)
    st_ref = next(it) if emit_state else None
    sf_scr = next(it)

    j = pl.program_id(1)
    d = x_ref.shape[2]
    n_chunks = seq_rows // CHUNK
    ext = seq_rows + 2 * HALO

    @pl.when(j == 0)
    def _():
        if has_state:
            sf_scr[...] = s0f_ref[:, 0, 0]
        else:
            sf_scr[...] = jnp.zeros(sf_scr.shape, F32)

    mods = [mod_ref[s] for s in range(n_seq)]

    def rows_of(vals):
        return jnp.concatenate([jnp.broadcast_to(v, (seq_rows, d)) for v in vals], axis=0) if n_seq > 1 else vals[0]

    he_parts = []
    for s in range(n_seq):
        xe = jnp.concatenate([xp_ref[s], x_ref[s], xn_ref[s]], axis=0)
        he_parts.append(_norm_mod(xe, n1_ref[...], mods[s][1:2], mods[s][0:1]).astype(BF16))
    he = jnp.concatenate(he_parts, axis=0) if n_seq > 1 else he_parts[0]
    hb_parts = [hp[HALO:HALO + seq_rows] for hp in he_parts]
    hb = jnp.concatenate(hb_parts, axis=0) if n_seq > 1 else hb_parts[0]
    x = jnp.concatenate([x_ref[s] for s in range(n_seq)], axis=0) if n_seq > 1 else x_ref[0]
    gates = _sigmoid(_dot(hb, w_bg_ref[...]) + b_bg_ref[...])

    ue_all = _dot(he, w_in_ref[:, :POOL_WIDTH])
    epos = j * seq_rows - HALO + lax.broadcasted_iota(jnp.int32, (ext, 1), 0)
    valid = (epos >= 0) & (epos < seq_len)
    tpos = j * seq_rows + lax.broadcasted_iota(jnp.int32, (seq_rows, 1), 0)
    pooled = [[] for _ in POOL_WINDOWS]
    for s in range(n_seq):
        ue = jnp.where(valid, ue_all[s * ext:(s + 1) * ext], 0.0)
        for gi, w in enumerate(POOL_WINDOWS):
            ug = ue[:, gi * POOL_GROUP:(gi + 1) * POOL_GROUP]
            acc, shift = ug, 1
            while shift < w:
                acc = acc + pltpu.roll(acc, shift, 0)
                shift *= 2
            if w // 2 > 1:
                acc = pltpu.roll(acc, ext - (w // 2 - 1), 0)
            cnt = (jnp.minimum(tpos + w // 2, seq_len) - jnp.maximum(tpos - w // 2, 0)).astype(F32)
            own = slice(HALO, HALO + seq_rows)
            pooled[gi].append((acc[own] / cnt - ug[own]).astype(BF16))
    pool_h = jnp.concatenate(
        [_dot(jnp.concatenate(pg, axis=0) if len(pg) > 1 else pg[0], w_pool_ref[gi]) for gi, pg in enumerate(pooled)],
        axis=1) * pscale_ref[...]
    pool_y = _dot(pool_h.astype(BF16), w_po_ref[...])

    g_lo = POOL_WIDTH + 2 * QK_WIDTH + V_WIDTH
    if has_state:
        zq = _dot(hb, w_in_ref[:, POOL_WIDTH:POOL_WIDTH + QK_WIDTH])
        g_all = _dot(hb, w_in_ref[:, g_lo:])
    else:
        zr = _dot(hb, w_in_ref[:, POOL_WIDTH:])
        zq, g_all = zr[:, :QK_WIDTH], zr[:, 2 * QK_WIDTH + V_WIDTH:]
    seq_out = []
    for s in range(n_seq):
        r0 = s * seq_rows
        head_out = []
        for head in range(N_HEADS):
            t = _decay_tiles(decay_ref, head)
            qs, ks, vs = [], [], []
            for c in range(n_chunks):
                rows = slice(r0 + c * CHUNK, r0 + (c + 1) * CHUNK)
                crow = slice(c * CHUNK, (c + 1) * CHUNK)
                q = zq[rows, head * DK:(head + 1) * DK]
                if use_rope:
                    q = _rope(q, cos_ref[crow], sin_ref[crow])
                if has_state:
                    k = kv_ref[s, crow, head * DK:(head + 1) * DK].astype(F32)
                    v = kv_ref[s, crow, QK_WIDTH + head * DV:QK_WIDTH + (head + 1) * DV]
                else:
                    k = zr[rows, QK_WIDTH + head * DK:QK_WIDTH + (head + 1) * DK] * (DK ** -0.5)
                    if use_rope:
                        k = _rope(k, cos_ref[crow], sin_ref[crow])
                    v = zr[rows, 2 * QK_WIDTH + head * DV:2 * QK_WIDTH + (head + 1) * DV].astype(BF16)
                qs.append(q)
                ks.append(k)
                vs.append(v)
            sf = [sf_scr[s, head]]
            for c in range(n_chunks):
                sf.append(_wide(t["cdf"]) * sf[c] + _dot_tn((ks[c] * t["kdf"]).astype(BF16), vs[c]))
            sf_scr[s, head] = sf[n_chunks]
            sb = [None] * (n_chunks + 1)
            sb[n_chunks] = sb_ref[s, 0, head] if has_state else jnp.zeros((DK, DV), F32)
            for c in reversed(range(n_chunks)):
                sb[c] = _wide(t["cdb"]) * sb[c + 1] + _dot_tn((ks[c] * t["kdb"]).astype(BF16), vs[c])
            if emit_state:
                st_ref[s, 0, head] = sf[n_chunks]
                st_ref[s, 1, head] = sb[0]
            outs = []
            for c in range(n_chunks):
                scores = _dot_nt(qs[c].astype(BF16), ks[c].astype(BF16)) * t["mask"]
                q_both = jnp.concatenate([(qs[c] * t["qdf"]).astype(BF16), (qs[c] * t["qdb"]).astype(BF16)], axis=1)
                s_both = jnp.concatenate([sf[c].astype(BF16), sb[c + 1].astype(BF16)], axis=0)
                outs.append(_dot(scores.astype(BF16), vs[c]) + _dot(q_both, s_both))
            o = jnp.concatenate(outs, axis=0)
            mu = jnp.mean(o, axis=-1, keepdims=True)
            oc = o - mu
            var = jnp.mean(oc * oc, axis=-1, keepdims=True)
            head_out.append(oc * lax.rsqrt(var + EPS))
        seq_out.append(jnp.concatenate(head_out, axis=1))
    o_n = (jnp.concatenate(seq_out, axis=0) if n_seq > 1 else seq_out[0]) * gn_ref[...]
    ret_y = _dot((_silu(g_all) * o_n).astype(BF16), w_ro_ref[...])

    merged = gates[:, :d] * pool_y + gates[:, d:] * ret_y
    x1 = x + rows_of([m[2:3] for m in mods]) * _dot(merged.astype(BF16), w_out_ref[...])

    h2 = _norm_mod(x1, n2_ref[...], rows_of([m[4:5] for m in mods]), rows_of([m[3:4] for m in mods]))
    h2_hi, h2_lo = _hi_lo(h2)
    n_rows = n_seq * seq_rows
    parts = _dot(jnp.concatenate([h2_hi, h2_lo], axis=0), w_rt_ref[...])
    logits_t = (parts[:n_rows, :LANES] + parts[n_rows:, :LANES] + parts[:n_rows, LANES:])
    for s in range(n_seq):
        rows = slice(s * seq_rows, (s + 1) * seq_rows)
        x1_ref[s] = x1[rows]
        h2_ref[s] = h2_hi[rows]
        logits = logits_t[rows].T[:N_EXPERTS]
        e = jnp.exp(logits - jnp.max(logits, axis=0, keepdims=True))
        aff_ref[s] = e / jnp.sum(e, axis=0, keepdims=True)


def _mix(x, mod, rope, states, decay, p):
    b, l, d = x.shape
    seq_rows = min(l, STEP_ROWS)
    n_seq = STEP_ROWS // seq_rows
    nblk = l // seq_rows
    hb = seq_rows // HALO
    n_halo = l // HALO
    use_rope, has_state = rope is not None, states is not None
    emit_state = not has_state
    assert b % n_seq == 0 and (n_seq == 1 or not (use_rope or has_state))
    in_specs = [pl.BlockSpec((n_seq, seq_rows, d), lambda bi, j: (bi, j, 0)),
                pl.BlockSpec((n_seq, HALO, d), lambda bi, j: (bi, jnp.maximum(j * hb - 1, 0), 0)),
                pl.BlockSpec((n_seq, HALO, d), lambda bi, j: (bi, jnp.minimum((j + 1) * hb, n_halo - 1), 0)),
                pl.BlockSpec((n_seq, 8, d), lambda bi, j: (bi, 0, 0))]
    args = [x, x, x, mod]
    if use_rope:
        in_specs += [pl.BlockSpec((seq_rows, DK), lambda bi, j: (j, 0))] * 2
        args += list(rope)
    if has_state:
        in_specs += [_state_spec(0),
                     pl.BlockSpec((1, 1, N_HEADS, DK, DV), lambda bi, j: (bi, j, 0, 0, 0)),
                     pl.BlockSpec((1, seq_rows, QK_WIDTH + V_WIDTH), lambda bi, j: (bi, j, 0))]
        args += list(states)
    weights = [p["n1g"], p["w_in"], p["w_bg"], p["b_bg"], p["w_pool"], p["pool_scale"], p["w_pool_out"],
               p["gn_g"], p["w_ret_out"], p["w_out"], p["n2g"], p["w_router_parts"]]
    in_specs.append(pl.BlockSpec(memory_space=pltpu.SMEM))
    args.append(decay)
    for w in weights:
        in_specs.append(_resident(w.shape))
        args.append(w)
    out_specs = [pl.BlockSpec((n_seq, seq_rows, d), lambda bi, j: (bi, j, 0)),
                 pl.BlockSpec((n_seq, seq_rows, d), lambda bi, j: (bi, j, 0)),
                 pl.BlockSpec((n_seq, N_EXPERTS, seq_rows), lambda bi, j: (bi, 0, j))]
    out_shape = [jax.ShapeDtypeStruct((b, l, d), F32), jax.ShapeDtypeStruct((b, l, d), BF16),
                 jax.ShapeDtypeStruct((b, N_EXPERTS, l), F32)]
    if emit_state:
        out_specs.append(pl.BlockSpec((n_seq, 2, N_HEADS, DK, DV), lambda bi, j: (bi, 0, 0, 0, 0)))
        out_shape.append(jax.ShapeDtypeStruct((b, 2, N_HEADS, DK, DV), F32))
    kern = functools.partial(_mix_kernel, seq_len=l, n_seq=n_seq, seq_rows=seq_rows, use_rope=use_rope,
                             has_state=has_state, emit_state=emit_state)
    return pl.pallas_call(
        kern,
        grid=(b // n_seq, nblk),
        in_specs=in_specs,
        out_specs=out_specs,
        out_shape=out_shape,
        scratch_shapes=[pltpu.VMEM((n_seq, N_HEADS, DK, DV), F32)],
        compiler_params=_cparams(("arbitrary", "arbitrary")),
        name="mix_rope" if use_rope else "mix",
    )(*args)


def _route_kernel(aff_ref, wsel_ref, pos_ref, cnt_ref, *, cap):
    aff = aff_ref[...]
    rows, l = aff.shape

    def as_value(bits):
        return pltpu.bitcast(bits, F32)

    def bisect(i, tau):
        cand = tau | jnp.left_shift(jnp.int32(1), 30 - i)
        cnt = jnp.sum((aff >= as_value(cand)).astype(F32), axis=1, keepdims=True)
        return jnp.where(cnt >= cap, cand, tau)

    tau = lax.fori_loop(0, 31, bisect, jnp.zeros((rows, 1), jnp.int32))
    gt = aff >= as_value(tau + 1)
    eq = (aff >= as_value(tau)) & jnp.logical_not(gt)
    need = cap - jnp.sum(gt.astype(F32), axis=1, keepdims=True)

    nlb = l // ROUTE_LANES
    tri = (lax.broadcasted_iota(jnp.int32, (ROUTE_LANES, ROUTE_LANES), 0)
           < lax.broadcasted_iota(jnp.int32, (ROUTE_LANES, ROUTE_LANES), 1)).astype(BF16)

    def prefix(mask):
        carry = jnp.zeros((rows, 1), F32)
        parts, starts = [], []
        for blk in range(nlb):
            mb = mask[:, blk * ROUTE_LANES:(blk + 1) * ROUTE_LANES].astype(F32)
            starts.append(carry)
            parts.append(_dot(mb.astype(BF16), tri) + carry)
            carry = carry + jnp.sum(mb, axis=1, keepdims=True)
        starts.append(carry)
        return jnp.concatenate(parts, axis=1), starts

    eq_rank, _ = prefix(eq)
    sel = gt | (eq & (eq_rank < need))
    pos, starts = prefix(sel)
    wsel_ref[...] = jnp.where(sel, aff, 0.0)
    pos_ref[...] = jnp.where(sel, pos.astype(jnp.int32), -1)
    lane = lax.broadcasted_iota(jnp.int32, (rows, COUNT_LANES), 1)
    table = jnp.zeros((rows, COUNT_LANES), jnp.int32)
    per = max(GATHER_TOKENS // ROUTE_LANES, 1)
    for tb in range(nlb // per + 1):
        table = jnp.where(lane == tb, starts[min(tb * per, nlb)].astype(jnp.int32), table)
    cnt_ref[...] = table


def _route(aff_rows, cap, row_block):
    r, l = aff_rows.shape
    spec = pl.BlockSpec((row_block, l), lambda i: (i, 0))
    return pl.pallas_call(
        functools.partial(_route_kernel, cap=cap),
        grid=(r // row_block,),
        in_specs=[spec],
        out_specs=[spec, spec, pl.BlockSpec((row_block, COUNT_LANES), lambda i: (i, 0))],
        out_shape=[jax.ShapeDtypeStruct((r, l), F32), jax.ShapeDtypeStruct((r, l), jnp.int32),
                   jax.ShapeDtypeStruct((r, COUNT_LANES), jnp.int32)],
        compiler_params=_cparams(("arbitrary",)),
        name=f"route_{l}",
    )(aff_rows)


def _stacked_hits(pos_ref, req, cap):
    l = pos_ref.shape[1]
    slot = lax.broadcasted_iota(jnp.int32, (cap, l), 0)
    return [pos_ref[req * N_EXPERTS + e:req * N_EXPERTS + e + 1, :] == slot for e in range(N_EXPERTS)]


def _gather_small_kernel(h_ref, pos_ref, xs_ref, *, cap):
    for req in range(h_ref.shape[0]):
        onehot = jnp.concatenate([h.astype(BF16) for h in _stacked_hits(pos_ref, req, cap)], axis=0)
        xs = _dot(onehot, h_ref[req]).astype(BF16)
        for e in range(N_EXPERTS):
            xs_ref[e, req * cap:(req + 1) * cap, :] = xs[e * cap:(e + 1) * cap]


def _gather_small(h2, pos, cap):
    b, l, d = h2.shape
    nr = SMALL_REQS if b % SMALL_REQS == 0 else 1
    return pl.pallas_call(
        functools.partial(_gather_small_kernel, cap=cap),
        grid=(b // nr,),
        in_specs=[pl.BlockSpec((nr, l, d), lambda bi: (bi, 0, 0)),
                  pl.BlockSpec((nr * N_EXPERTS, l), lambda bi: (bi, 0))],
        out_specs=pl.BlockSpec((N_EXPERTS, nr * cap, d), lambda bi: (0, bi, 0)),
        out_shape=jax.ShapeDtypeStruct((N_EXPERTS, b * cap, d), BF16),
        compiler_params=_cparams(("arbitrary",)),
        name="gather_small",
    )(h2, pos)


def _slot_windows(cnt_ref, row, tb, cap):
    lo = cnt_ref[row * COUNT_LANES + tb]
    hi = cnt_ref[row * COUNT_LANES + tb + 1]
    start = jnp.minimum(lo & jnp.int32(-SLOT_ALIGN), jnp.int32(cap - GATHER_SLOTS))
    n_win = lax.shift_right_logical(hi - start + jnp.int32(GATHER_SLOTS - 1),
                                    jnp.int32(GATHER_SLOTS.bit_length() - 1))
    return pl.multiple_of(start, SLOT_ALIGN), n_win


def _next_window(start, k, cap):
    first = start + k * GATHER_SLOTS
    return pl.multiple_of(jnp.minimum(first, cap - GATHER_SLOTS), SLOT_ALIGN), first


def _window_hits(pos_row, start, first_slot=None):
    slot = start + lax.broadcasted_iota(jnp.int32, (GATHER_SLOTS, pos_row.shape[1]), 0)
    hit = pos_row == slot
    return hit if first_slot is None else hit & (slot >= first_slot)


def _gather_big_kernel(cnt_ref, h_ref, pos_ref, xs_ref, acc_ref, *, cap):
    bi, eg = pl.program_id(0), pl.program_id(1)
    n_tb = h_ref.shape[1] // GATHER_TOKENS
    acc_ref[...] = jnp.zeros(acc_ref.shape, F32)

    def onehot(el, tb, start, first_slot=None):
        toks = slice(tb * GATHER_TOKENS, (tb + 1) * GATHER_TOKENS)
        return _window_hits(pos_ref[pl.ds(eg * GATHER_EXPERTS + el, 1), toks], start, first_slot).astype(BF16)

    windows = {(tb, el): _slot_windows(cnt_ref, bi * N_EXPERTS + eg * GATHER_EXPERTS + el, tb, cap)
               for tb in range(n_tb) for el in range(GATHER_EXPERTS)}
    for tb in range(n_tb):
        toks = slice(tb * GATHER_TOKENS, (tb + 1) * GATHER_TOKENS)
        stacked = jnp.concatenate([onehot(el, tb, windows[tb, el][0]) for el in range(GATHER_EXPERTS)], axis=0)
        rows = _dot(stacked, h_ref[0, toks, :])
        for el in range(GATHER_EXPERTS):
            acc_ref[el, pl.ds(windows[tb, el][0], GATHER_SLOTS), :] += rows[el * GATHER_SLOTS:(el + 1) * GATHER_SLOTS]
    for (tb, el), (start, n_win) in windows.items():
        def extra(k, carry, tb=tb, el=el, start=start):
            clamped, first = _next_window(start, k, cap)
            toks = slice(tb * GATHER_TOKENS, (tb + 1) * GATHER_TOKENS)
            acc_ref[el, pl.ds(clamped, GATHER_SLOTS), :] += _dot(onehot(el, tb, clamped, first), h_ref[0, toks, :])
            return carry
        lax.fori_loop(1, n_win, extra, 0)
    xs_ref[...] = acc_ref[...].astype(BF16)


def _gather_big(h2, pos, cnt_flat, cap):
    b, l, d = h2.shape
    grid_spec = pltpu.PrefetchScalarGridSpec(
        num_scalar_prefetch=1,
        grid=(b, N_EXPERTS // GATHER_EXPERTS),
        in_specs=[pl.BlockSpec((1, l, d), lambda bi, eg, c: (bi, 0, 0)),
                  pl.BlockSpec((N_EXPERTS, l), lambda bi, eg, c: (bi, 0))],
        out_specs=pl.BlockSpec((GATHER_EXPERTS, cap, d), lambda bi, eg, c: (eg, bi, 0)),
        scratch_shapes=[pltpu.VMEM((GATHER_EXPERTS, cap, d), F32)],
    )
    return pl.pallas_call(
        functools.partial(_gather_big_kernel, cap=cap),
        grid_spec=grid_spec,
        out_shape=jax.ShapeDtypeStruct((N_EXPERTS, b * cap, d), BF16),
        compiler_params=_cparams(("arbitrary", "arbitrary")),
        name="gather_big",
    )(cnt_flat, h2, pos)


def _ffn_kernel(xa_ref, xb_ref, wg_ref, wu_ref, wd_ref, ya_ref, yb_ref, acca_ref, accb_ref):
    f, nf = pl.program_id(1), pl.num_programs(1)

    def ff_tile(first, last):
        wg = wg_ref[0].astype(BF16)
        wu = wu_ref[0].astype(BF16)
        wd = wd_ref[0].astype(BF16)
        for x_ref, y_ref, acc_ref in ((xa_ref, ya_ref, acca_ref), (xb_ref, yb_ref, accb_ref)):
            m = x_ref.shape[1]
            step = min(FFN_ROWS, m)
            for r0 in range(0, m, step):
                rows = slice(r0, r0 + step)
                x = x_ref[0, rows, :]
                hid = (_silu(_dot(x, wg)) * _dot(x, wu)).astype(BF16)
                part = _dot(hid, wd)
                if first:
                    acc_ref[rows, :] = part
                elif last:
                    y_ref[0, rows, :] = (acc_ref[rows, :] + part).astype(BF16)
                else:
                    acc_ref[rows, :] += part

    pl.when(f == 0)(functools.partial(ff_tile, True, False))
    pl.when((f > 0) & (f < nf - 1))(functools.partial(ff_tile, False, False))
    pl.when(f == nf - 1)(functools.partial(ff_tile, False, True))


def _ffn(xs_a, xs_b, w_gate, w_up, w_down):
    e, ma, d = xs_a.shape
    mb = xs_b.shape[1]
    ff = w_gate.shape[2]
    nf = ff // FF_TILE
    assert ff % FF_TILE == 0 and nf >= 2
    return pl.pallas_call(
        _ffn_kernel,
        grid=(e, nf),
        in_specs=[pl.BlockSpec((1, ma, d), lambda ei, f: (ei, 0, 0)),
                  pl.BlockSpec((1, mb, d), lambda ei, f: (ei, 0, 0)),
                  pl.BlockSpec((1, d, FF_TILE), lambda ei, f: (ei, 0, f)),
                  pl.BlockSpec((1, d, FF_TILE), lambda ei, f: (ei, 0, f)),
                  pl.BlockSpec((1, FF_TILE, d), lambda ei, f: (ei, f, 0))],
        out_specs=[pl.BlockSpec((1, ma, d), lambda ei, f: (ei, 0, 0)),
                   pl.BlockSpec((1, mb, d), lambda ei, f: (ei, 0, 0))],
        out_shape=[jax.ShapeDtypeStruct((e, ma, d), BF16), jax.ShapeDtypeStruct((e, mb, d), BF16)],
        scratch_shapes=[pltpu.VMEM((ma, d), F32), pltpu.VMEM((mb, d), F32)],
        compiler_params=_cparams(("arbitrary", "arbitrary")),
        name="ffn",
    )(xs_a, xs_b, w_gate, w_up, w_down)


def _slot_gate(hit, w_row):
    return jnp.sum(jnp.where(hit, w_row, 0.0), axis=1, keepdims=True)


def _finish(x1, moe, gate2, fn_g):
    x = x1 + gate2 * moe
    return (x * lax.rsqrt(jnp.mean(x * x, axis=-1, keepdims=True) + EPS)) * fn_g


def _combine_small_kernel(x1_ref, y_ref, pos_ref, w_ref, mod_ref, fn_ref, o_ref, *, cap):
    for req in range(x1_ref.shape[0]):
        hits = _stacked_hits(pos_ref, req, cap)
        gated = [(y_ref[e, req * cap:(req + 1) * cap, :].astype(F32)
                  * _slot_gate(hits[e], w_ref[req * N_EXPERTS + e:req * N_EXPERTS + e + 1, :])).astype(BF16)
                 for e in range(N_EXPERTS)]
        onehot = jnp.concatenate([h.astype(BF16) for h in hits], axis=0)
        moe = _dot_tn(onehot, jnp.concatenate(gated, axis=0))
        o_ref[req] = _finish(x1_ref[req], moe, mod_ref[req, 5:6], fn_ref[...])


def _combine_small(x1, y, pos, wsel, mod, fn_g, cap):
    b, l, d = x1.shape
    nr = SMALL_REQS if b % SMALL_REQS == 0 else 1
    return pl.pallas_call(
        functools.partial(_combine_small_kernel, cap=cap),
        grid=(b // nr,),
        in_specs=[pl.BlockSpec((nr, l, d), lambda bi: (bi, 0, 0)),
                  pl.BlockSpec((N_EXPERTS, nr * cap, d), lambda bi: (0, bi, 0)),
                  pl.BlockSpec((nr * N_EXPERTS, l), lambda bi: (bi, 0)),
                  pl.BlockSpec((nr * N_EXPERTS, l), lambda bi: (bi, 0)),
                  pl.BlockSpec((nr, 8, d), lambda bi: (bi, 0, 0)),
                  pl.BlockSpec((1, d), lambda bi: (0, 0))],
        out_specs=pl.BlockSpec((nr, l, d), lambda bi: (bi, 0, 0)),
        out_shape=jax.ShapeDtypeStruct((b, l, d), F32),
        compiler_params=_cparams(("arbitrary",)),
        name="combine_small",
    )(x1, y, pos, wsel, mod, fn_g)


def _combine_big_kernel(cnt_ref, x1_ref, y_ref, pos_ref, w_ref, mod_ref, fn_ref, o_ref, acc_ref, *, cap):
    bi, tb = pl.program_id(0), pl.program_id(1)

    def window_terms(e, start, first_slot=None):
        hit = _window_hits(pos_ref[e:e + 1, :], start, first_slot)
        y = y_ref[e, pl.ds(start, GATHER_SLOTS), :].astype(F32)
        return hit.astype(BF16), (y * _slot_gate(hit, w_ref[e:e + 1, :])).astype(BF16)

    windows = [_slot_windows(cnt_ref, bi * N_EXPERTS + e, tb, cap) for e in range(N_EXPERTS)]
    terms = [window_terms(e, start) for e, (start, _) in enumerate(windows)]
    acc_ref[...] = _dot_tn(jnp.concatenate([t[0] for t in terms], axis=0),
                           jnp.concatenate([t[1] for t in terms], axis=0))
    for e, (start, n_win) in enumerate(windows):
        def extra(k, carry, e=e, start=start):
            hit, gated = window_terms(e, *_next_window(start, k, cap))
            acc_ref[...] += _dot_tn(hit, gated)
            return carry
        lax.fori_loop(1, n_win, extra, 0)
    o_ref[0] = _finish(x1_ref[0], acc_ref[...], mod_ref[0, 5:6], fn_ref[...])


def _combine_big(x1, y, pos, wsel, cnt_flat, mod, fn_g, cap):
    b, l, d = x1.shape
    grid_spec = pltpu.PrefetchScalarGridSpec(
        num_scalar_prefetch=1,
        grid=(b, l // GATHER_TOKENS),
        in_specs=[pl.BlockSpec((1, GATHER_TOKENS, d), lambda bi, tb, c: (bi, tb, 0)),
                  pl.BlockSpec((N_EXPERTS, cap, d), lambda bi, tb, c: (0, bi, 0)),
                  pl.BlockSpec((N_EXPERTS, GATHER_TOKENS), lambda bi, tb, c: (bi, tb)),
                  pl.BlockSpec((N_EXPERTS, GATHER_TOKENS), lambda bi, tb, c: (bi, tb)),
                  pl.BlockSpec((1, 8, d), lambda bi, tb, c: (bi, 0, 0)),
                  pl.BlockSpec((1, d), lambda bi, tb, c: (0, 0))],
        out_specs=pl.BlockSpec((1, GATHER_TOKENS, d), lambda bi, tb, c: (bi, tb, 0)),
        scratch_shapes=[pltpu.VMEM((GATHER_TOKENS, d), F32)],
    )
    return pl.pallas_call(
        functools.partial(_combine_big_kernel, cap=cap),
        grid_spec=grid_spec,
        out_shape=jax.ShapeDtypeStruct((b, l, d), F32),
        compiler_params=_cparams(("arbitrary", "arbitrary")),
        name="combine_big",
    )(cnt_flat, x1, y, pos, wsel, mod, fn_g)


def _router_parts(w):
    hi, lo = _hi_lo(w)
    pad = ((0, 0), (0, LANES - w.shape[1]))
    return jnp.concatenate([jnp.pad(hi, pad), jnp.pad(lo, pad)], axis=1)


def _rope_tables(l):
    rows = l // GRID_W
    row = jnp.repeat(jnp.arange(rows, dtype=F32), GRID_W)
    col = jnp.tile(jnp.arange(GRID_W, dtype=F32), rows)
    nf = DK // 4
    inv = ROPE_BASE ** (-jnp.arange(nf, dtype=F32) / nf)
    ang = jnp.concatenate([row[:, None] * inv, col[:, None] * inv], axis=-1)
    cos = jnp.repeat(jnp.cos(ang), 2, axis=-1)
    sin = jnp.repeat(jnp.sin(ang), 2, axis=-1) * jnp.tile(jnp.asarray([-1.0, 1.0], F32), DK // 2)
    return cos, sin


def kernel(x_prompt, x_sample, c, state_ret, c_ctx, w_ada, b_ada, norm1_g, w_in, w_pool, pool_scale, w_pool_out,
           ret_decay, ret_gn_g, w_ret_out, w_branch_gate, b_branch_gate, w_out, norm2_g, w_router, w_exp_gate,
           w_exp_up, w_exp_down, final_norm_g):
    depth = w_ada.shape[0]
    assert depth == 1, "single trunk layer"
    bc, lc, d = x_prompt.shape
    bl, ll, _ = x_sample.shape
    assert lc % CHUNK == 0 and STEP_ROWS % lc == 0 and ll % STEP_ROWS == 0 and ll % GATHER_TOKENS == 0
    cap_c = max(1, CAPACITY_FACTOR * lc // N_EXPERTS)
    cap_l = max(1, CAPACITY_FACTOR * ll // N_EXPERTS)
    assert cap_l % GATHER_SLOTS == 0 and ll // GATHER_TOKENS < COUNT_LANES

    cond = jnp.zeros((8 * pl.cdiv(bl + 1, 8), d), F32).at[:bl].set(c).at[bl].set(c_ctx)
    ada = _adaln(cond, w_ada[0], b_ada[0]).reshape(-1, 6, d)
    ada = jnp.pad(ada, ((0, 0), (0, 2), (0, 0)))
    mod_l = ada[:bl]
    mod_c = jnp.broadcast_to(ada[bl:bl + 1], (bc, 8, d))

    p = dict(
        n1g=norm1_g[0][None], w_in=w_in[0].astype(BF16), w_bg=w_branch_gate[0].astype(BF16),
        b_bg=b_branch_gate[0][None], w_pool=w_pool[0].astype(BF16), pool_scale=pool_scale[0][None],
        w_pool_out=w_pool_out[0].astype(BF16), gn_g=ret_gn_g[0][None], w_ret_out=w_ret_out[0].astype(BF16),
        w_out=w_out[0].astype(BF16), n2g=norm2_g[0][None], w_router_parts=_router_parts(w_router[0]),
    )
    decay = ret_decay[0]
    fn_g = final_norm_g[None]

    x1_c, h2_c, aff_c, st_c = _mix(x_prompt, mod_c, None, None, decay, p)
    rope = _rope_tables(ll)
    sb, kv = _revscan(x_sample, mod_l, rope[0], rope[1], state_ret, decay, p["n1g"], p["w_in"])
    x1_l, h2_l, aff_l = _mix(x_sample, mod_l, rope, (state_ret, sb, kv), decay, p)

    wsel_c, pos_c, _ = _route(aff_c.reshape(bc * N_EXPERTS, lc), cap_c, min(ROUTE_ROWS, bc * N_EXPERTS))
    wsel_l, pos_l, cnt_l = _route(aff_l.reshape(bl * N_EXPERTS, ll), cap_l, min(ROUTE_ROWS, bl * N_EXPERTS))
    cnt_flat = cnt_l.reshape(-1)

    xs_c = _gather_small(h2_c, pos_c, cap_c)
    xs_l = _gather_big(h2_l, pos_l, cnt_flat, cap_l)
    y_c, y_l = _ffn(xs_c, xs_l, w_exp_gate[0], w_exp_up[0], w_exp_down[0])

    y_prompt = _combine_small(x1_c, y_c, pos_c, wsel_c, mod_c, fn_g, cap_c)
    y_sample = _combine_big(x1_l, y_l, pos_l, wsel_l, cnt_flat, mod_l, fn_g, cap_l)
    new_state = st_c[:, None].astype(x_prompt.dtype)
    return (y_prompt, y_sample, new_state)
```

```python
import functools

import jax
import jax.numpy as jnp
from jax import lax
from jax.experimental import pallas as pl
from jax.experimental.pallas import tpu as pltpu

F32 = jnp.float32
BF16 = jnp.bfloat16

N_HEADS = 4
DK = 128
DV = 256
CHUNK = 128
POOL_WINDOWS = (2, 4, 8, 16)
POOL_GROUP = 128
POOL_WIDTH = POOL_GROUP * len(POOL_WINDOWS)
QK_WIDTH = N_HEADS * DK
V_WIDTH = N_HEADS * DV
N_EXPERTS = 16
CAPACITY_FACTOR = 2
GRID_W = 64
ROPE_BASE = 10000.0
EPS = 1e-6

LANES = 128
STEP_ROWS = 512
HALO = 16
ROUTE_LANES = 256
ROUTE_ROWS = 512
SMALL_REQS = 4
GATHER_TOKENS = 256
GATHER_SLOTS = 64
GATHER_EXPERTS = 4
COMBINE_BLOCKS = 2
COUNT_LANES = 128
SLOT_ALIGN = 16
FF_TILE = 256
FFN_ROWS = 512
VMEM_LIMIT = 56 * 1024 * 1024


def _cparams(sem):
    return pltpu.CompilerParams(dimension_semantics=sem, vmem_limit_bytes=VMEM_LIMIT)


def _resident(shape):
    return pl.BlockSpec(shape, lambda *_: (0,) * len(shape), pipeline_mode=pl.Buffered(1))


def _sigmoid(x):
    return 0.5 * jnp.tanh(0.5 * x) + 0.5


def _silu(x):
    return x * _sigmoid(x)


def _norm_mod(x, g, scale, shift):
    y = x * lax.rsqrt(jnp.mean(x * x, axis=-1, keepdims=True) + EPS)
    return (y * g) * (1.0 + scale) + shift


def _hi_lo(x):
    hi = x.astype(BF16)
    return hi, (x - hi.astype(F32)).astype(BF16)


def _dot(a, b):
    return jnp.dot(a, b, preferred_element_type=F32)


def _dot_nt(a, b):
    return lax.dot_general(a, b, (((1,), (1,)), ((), ())), preferred_element_type=F32)


def _dot_tn(a, b):
    return lax.dot_general(a, b, (((0,), (0,)), ((), ())), preferred_element_type=F32)


def _adaln_kernel(c_ref, w_ref, b_ref, o_ref):
    rows = c_ref.shape[0]
    s_hi, s_lo = _hi_lo(_silu(c_ref[...]))
    w_hi, w_lo = _hi_lo(w_ref[...])
    by_hi = _dot(jnp.concatenate([s_hi, s_lo], axis=0), w_hi)
    o_ref[...] = by_hi[:rows] + by_hi[rows:] + _dot(s_hi, w_lo) + b_ref[...]


def _adaln(cond, w, b):
    rows, d = cond.shape
    n = w.shape[1]
    tn = d
    return pl.pallas_call(
        _adaln_kernel,
        grid=(n // tn,),
        in_specs=[pl.BlockSpec((rows, d), lambda i: (0, 0)),
                  pl.BlockSpec((d, tn), lambda i: (0, i)),
                  pl.BlockSpec((1, tn), lambda i: (0, i))],
        out_specs=pl.BlockSpec((rows, tn), lambda i: (0, i)),
        out_shape=jax.ShapeDtypeStruct((rows, n), F32),
        compiler_params=_cparams(("arbitrary",)),
        name="adaln",
    )(cond, w, b.reshape(1, n))


def _decay_tiles(decay_ref, head):
    lgf = -jnp.exp(jnp.full((CHUNK, DK), decay_ref[0, head], F32))
    lgb = -jnp.exp(jnp.full((CHUNK, DK), decay_ref[1, head], F32))
    i = lax.broadcasted_iota(jnp.int32, (CHUNK, DK), 0).astype(F32)
    m = lax.broadcasted_iota(jnp.int32, (CHUNK, DK), 1).astype(F32)
    diff = i - m
    return dict(
        mask=jnp.where(diff >= 0.0, jnp.exp(lgf * jnp.maximum(diff, 0.0)), jnp.exp(lgb * jnp.maximum(-diff, 0.0))),
        qdf=jnp.exp(lgf * (i + 1.0)),
        qdb=jnp.exp(lgb * (CHUNK - i)),
        kdf=jnp.exp(lgf * (CHUNK - 1.0 - i)),
        kdb=jnp.exp(lgb * i),
        cdf=jnp.exp(lgf * float(CHUNK)),
        cdb=jnp.exp(lgb * float(CHUNK)),
    )


def _wide(t):
    return jnp.concatenate([t, t], axis=1)


def _rope(x, cos, sin):
    even = (lax.broadcasted_iota(jnp.int32, x.shape, 1) % 2) == 0
    partner = jnp.where(even, pltpu.roll(x, x.shape[1] - 1, 1), pltpu.roll(x, 1, 1))
    return x * cos + partner * sin


def _state_spec(direction):
    return pl.BlockSpec((1, 1, 1, N_HEADS, DK, DV), lambda bi, j: (bi, 0, direction, 0, 0, 0))


def _revscan_kernel(x_ref, mod_ref, cos_ref, sin_ref, s0_ref, decay_ref, n1_ref, w_in_ref, sb_ref, kv_ref, s_scr):
    j = pl.program_id(1)

    @pl.when(j == 0)
    def _():
        s_scr[...] = s0_ref[0, 0, 0]

    mod = mod_ref[0]
    h = _norm_mod(x_ref[0], n1_ref[...], mod[1:2], mod[0:1]).astype(BF16)
    k_lo = POOL_WIDTH + QK_WIDTH
    kv = _dot(h, w_in_ref[:, k_lo:k_lo + QK_WIDTH + V_WIDTH])
    sb_ref[0, 0] = s_scr[...]
    kv_ref[0, :, QK_WIDTH:] = kv[:, QK_WIDTH:].astype(BF16)
    for head in range(N_HEADS):
        t = _decay_tiles(decay_ref, head)
        s = s_scr[head]
        for c in reversed(range(STEP_ROWS // CHUNK)):
            rows = slice(c * CHUNK, (c + 1) * CHUNK)
            cols = slice(head * DK, (head + 1) * DK)
            k = _rope(kv[rows, cols] * (DK ** -0.5), cos_ref[rows], sin_ref[rows])
            kv_ref[0, rows, cols] = k.astype(BF16)
            v = kv[rows, QK_WIDTH + head * DV:QK_WIDTH + (head + 1) * DV]
            s = _wide(t["cdb"]) * s + _dot_tn((k * t["kdb"]).astype(BF16), v.astype(BF16))
        s_scr[head] = s


def _revscan(x, mod, cos, sin, state_ret, decay, n1g, w_in):
    b, l, d = x.shape
    nblk = l // STEP_ROWS
    return pl.pallas_call(
        _revscan_kernel,
        grid=(b, nblk),
        in_specs=[pl.BlockSpec((1, STEP_ROWS, d), lambda bi, j: (bi, nblk - 1 - j, 0)),
                  pl.BlockSpec((1, 8, d), lambda bi, j: (bi, 0, 0)),
                  pl.BlockSpec((STEP_ROWS, DK), lambda bi, j: (nblk - 1 - j, 0)),
                  pl.BlockSpec((STEP_ROWS, DK), lambda bi, j: (nblk - 1 - j, 0)),
                  _state_spec(1),
                  pl.BlockSpec(memory_space=pltpu.SMEM),
                  _resident(n1g.shape),
                  _resident(w_in.shape)],
        out_specs=[pl.BlockSpec((1, 1, N_HEADS, DK, DV), lambda bi, j: (bi, nblk - 1 - j, 0, 0, 0)),
                   pl.BlockSpec((1, STEP_ROWS, QK_WIDTH + V_WIDTH), lambda bi, j: (bi, nblk - 1 - j, 0))],
        out_shape=[jax.ShapeDtypeStruct((b, nblk, N_HEADS, DK, DV), F32),
                   jax.ShapeDtypeStruct((b, l, QK_WIDTH + V_WIDTH), BF16)],
        scratch_shapes=[pltpu.VMEM((N_HEADS, DK, DV), F32)],
        compiler_params=_cparams(("arbitrary", "arbitrary")),
        name="revscan",
    )(x, mod, cos, sin, state_ret, decay, n1g, w_in)


def _mix_kernel(*refs, seq_len, n_seq, seq_rows, use_rope, has_state, emit_state):
    it = iter(refs)
    x_ref, xp_ref, xn_ref, mod_ref = next(it), next(it), next(it), next(it)
    cos_ref = sin_ref = s0f_ref = sb_ref = kv_ref = None
    if use_rope:
        cos_ref, sin_ref = next(it), next(it)
    if has_state:
        s0f_ref, sb_ref, kv_ref = next(it), next(it), next(it)
    (decay_ref, n1_ref, w_in_ref, w_bg_ref, b_bg_ref, w_pool_ref, pscale_ref, w_po_ref, gn_ref, w_ro_ref,
     w_out_ref, n2_ref, w_rt_ref) = (next(it) for _ in range(13))
    x1_ref, h2_ref, aff_ref = next(it), next(it), next(it)
    st_ref = next(it) if emit_state else None
    sf_scr = next(it)

    j = pl.program_id(1)
    d = x_ref.shape[2]
    n_chunks = seq_rows // CHUNK
    ext = seq_rows + 2 * HALO

    @pl.when(j == 0)
    def _():
        if has_state:
            sf_scr[...] = s0f_ref[:, 0, 0]
        else:
            sf_scr[...] = jnp.zeros(sf_scr.shape, F32)

    mods = [mod_ref[s] for s in range(n_seq)]

    def rows_of(vals):
        return jnp.concatenate([jnp.broadcast_to(v, (seq_rows, d)) for v in vals], axis=0) if n_seq > 1 else vals[0]

    he_parts = []
    for s in range(n_seq):
        xe = jnp.concatenate([xp_ref[s], x_ref[s], xn_ref[s]], axis=0)
        he_parts.append(_norm_mod(xe, n1_ref[...], mods[s][1:2], mods[s][0:1]).astype(BF16))
    he = jnp.concatenate(he_parts, axis=0) if n_seq > 1 else he_parts[0]
    hb_parts = [hp[HALO:HALO + seq_rows] for hp in he_parts]
    hb = jnp.concatenate(hb_parts, axis=0) if n_seq > 1 else hb_parts[0]
    x = jnp.concatenate([x_ref[s] for s in range(n_seq)], axis=0) if n_seq > 1 else x_ref[0]
    gates = _sigmoid(_dot(hb, w_bg_ref[...]) + b_bg_ref[...])

    ue_all = _dot(he, w_in_ref[:, :POOL_WIDTH])
    epos = j * seq_rows - HALO + lax.broadcasted_iota(jnp.int32, (ext, 1), 0)
    valid = (epos >= 0) & (epos < seq_len)
    tpos = j * seq_rows + lax.broadcasted_iota(jnp.int32, (seq_rows, 1), 0)
    pooled = [[] for _ in POOL_WINDOWS]
    for s in range(n_seq):
        ue = jnp.where(valid, ue_all[s * ext:(s + 1) * ext], 0.0)
        for gi, w in enumerate(POOL_WINDOWS):
            ug = ue[:, gi * POOL_GROUP:(gi + 1) * POOL_GROUP]
            acc, shift = ug, 1
            while shift < w:
                acc = acc + pltpu.roll(acc, shift, 0)
                shift *= 2
            if w // 2 > 1:
                acc = pltpu.roll(acc, ext - (w // 2 - 1), 0)
            cnt = (jnp.minimum(tpos + w // 2, seq_len) - jnp.maximum(tpos - w // 2, 0)).astype(F32)
            own = slice(HALO, HALO + seq_rows)
            pooled[gi].append((acc[own] / cnt - ug[own]).astype(BF16))
    pool_h = jnp.concatenate(
        [_dot(jnp.concatenate(pg, axis=0) if len(pg) > 1 else pg[0], w_pool_ref[gi]) for gi, pg in enumerate(pooled)],
        axis=1) * pscale_ref[...]
    pool_y = _dot(pool_h.astype(BF16), w_po_ref[...])

    g_lo = POOL_WIDTH + 2 * QK_WIDTH + V_WIDTH
    if has_state:
        zq = _dot(hb, w_in_ref[:, POOL_WIDTH:POOL_WIDTH + QK_WIDTH])
        g_all = _dot(hb, w_in_ref[:, g_lo:])
    else:
        zr = _dot(hb, w_in_ref[:, POOL_WIDTH:])
        zq, g_all = zr[:, :QK_WIDTH], zr[:, 2 * QK_WIDTH + V_WIDTH:]
    seq_out = []
    for s in range(n_seq):
        r0 = s * seq_rows
        head_out = []
        for head in range(N_HEADS):
            t = _decay_tiles(decay_ref, head)
            qs, ks, vs = [], [], []
            for c in range(n_chunks):
                rows = slice(r0 + c * CHUNK, r0 + (c + 1) * CHUNK)
                crow = slice(c * CHUNK, (c + 1) * CHUNK)
                q = zq[rows, head * DK:(head + 1) * DK]
                if use_rope:
                    q = _rope(q, cos_ref[crow], sin_ref[crow])
                if has_state:
                    k = kv_ref[s, crow, head * DK:(head + 1) * DK].astype(F32)
                    v = kv_ref[s, crow, QK_WIDTH + head * DV:QK_WIDTH + (head + 1) * DV]
                else:
                    k = zr[rows, QK_WIDTH + head * DK:QK_WIDTH + (head + 1) * DK] * (DK ** -0.5)
                    if use_rope:
                        k = _rope(k, cos_ref[crow], sin_ref[crow])
                    v = zr[rows, 2 * QK_WIDTH + head * DV:2 * QK_WIDTH + (head + 1) * DV].astype(BF16)
                qs.append(q)
                ks.append(k)
                vs.append(v)
            sf = [sf_scr[s, head]]
            for c in range(n_chunks):
                sf.append(_wide(t["cdf"]) * sf[c] + _dot_tn((ks[c] * t["kdf"]).astype(BF16), vs[c]))
            sf_scr[s, head] = sf[n_chunks]
            sb = [None] * (n_chunks + 1)
            sb[n_chunks] = sb_ref[s, 0, head] if has_state else jnp.zeros((DK, DV), F32)
            for c in reversed(range(n_chunks)):
                sb[c] = _wide(t["cdb"]) * sb[c + 1] + _dot_tn((ks[c] * t["kdb"]).astype(BF16), vs[c])
            if emit_state:
                st_ref[s, 0, head] = sf[n_chunks]
                st_ref[s, 1, head] = sb[0]
            outs = []
            for c in range(n_chunks):
                scores = _dot_nt(qs[c].astype(BF16), ks[c].astype(BF16)) * t["mask"]
                q_both = jnp.concatenate([(qs[c] * t["qdf"]).astype(BF16), (qs[c] * t["qdb"]).astype(BF16)], axis=1)
                s_both = jnp.concatenate([sf[c].astype(BF16), sb[c + 1].astype(BF16)], axis=0)
                outs.append(_dot(scores.astype(BF16), vs[c]) + _dot(q_both, s_both))
            o = jnp.concatenate(outs, axis=0)
            mu = jnp.mean(o, axis=-1, keepdims=True)
            oc = o - mu
            var = jnp.mean(oc * oc, axis=-1, keepdims=True)
            head_out.append(oc * lax.rsqrt(var + EPS))
        seq_out.append(jnp.concatenate(head_out, axis=1))
    o_n = (jnp.concatenate(seq_out, axis=0) if n_seq > 1 else seq_out[0]) * gn_ref[...]
    ret_y = _dot((_silu(g_all) * o_n).astype(BF16), w_ro_ref[...])

    merged = gates[:, :d] * pool_y + gates[:, d:] * ret_y
    x1 = x + rows_of([m[2:3] for m in mods]) * _dot(merged.astype(BF16), w_out_ref[...])

    h2 = _norm_mod(x1, n2_ref[...], rows_of([m[4:5] for m in mods]), rows_of([m[3:4] for m in mods]))
    h2_hi, h2_lo = _hi_lo(h2)
    n_rows = n_seq * seq_rows
    parts = _dot(jnp.concatenate([h2_hi, h2_lo], axis=0), w_rt_ref[...])
    logits_t = (parts[:n_rows, :LANES] + parts[n_rows:, :LANES] + parts[:n_rows, LANES:])
    for s in range(n_seq):
        rows = slice(s * seq_rows, (s + 1) * seq_rows)
        x1_ref[s] = x1[rows]
        h2_ref[s] = h2_hi[rows]
        logits = logits_t[rows].T[:N_EXPERTS]
        e = jnp.exp(logits - jnp.max(logits, axis=0, keepdims=True))
        aff_ref[s] = e / jnp.sum(e, axis=0, keepdims=True)


def _mix(x, mod, rope, states, decay, p):
    b, l, d = x.shape
    seq_rows = min(l, STEP_ROWS)
    n_seq = STEP_ROWS // seq_rows
    nblk = l // seq_rows
    hb = seq_rows // HALO
    n_halo = l // HALO
    use_rope, has_state = rope is not None, states is not None
    emit_state = not has_state
    assert b % n_seq == 0 and (n_seq == 1 or not (use_rope or has_state))
    in_specs = [pl.BlockSpec((n_seq, seq_rows, d), lambda bi, j: (bi, j, 0)),
                pl.BlockSpec((n_seq, HALO, d), lambda bi, j: (bi, jnp.maximum(j * hb - 1, 0), 0)),
                pl.BlockSpec((n_seq, HALO, d), lambda bi, j: (bi, jnp.minimum((j + 1) * hb, n_halo - 1), 0)),
                pl.BlockSpec((n_seq, 8, d), lambda bi, j: (bi, 0, 0))]
    args = [x, x, x, mod]
    if use_rope:
        in_specs += [pl.BlockSpec((seq_rows, DK), lambda bi, j: (j, 0))] * 2
        args += list(rope)
    if has_state:
        in_specs += [_state_spec(0),
                     pl.BlockSpec((1, 1, N_HEADS, DK, DV), lambda bi, j: (bi, j, 0, 0, 0)),
                     pl.BlockSpec((1, seq_rows, QK_WIDTH + V_WIDTH), lambda bi, j: (bi, j, 0))]
        args += list(states)
    weights = [p["n1g"], p["w_in"], p["w_bg"], p["b_bg"], p["w_pool"], p["pool_scale"], p["w_pool_out"],
               p["gn_g"], p["w_ret_out"], p["w_out"], p["n2g"], p["w_router_parts"]]
    in_specs.append(pl.BlockSpec(memory_space=pltpu.SMEM))
    args.append(decay)
    for w in weights:
        in_specs.append(_resident(w.shape))
        args.append(w)
    out_specs = [pl.BlockSpec((n_seq, seq_rows, d), lambda bi, j: (bi, j, 0)),
                 pl.BlockSpec((n_seq, seq_rows, d), lambda bi, j: (bi, j, 0)),
                 pl.BlockSpec((n_seq, N_EXPERTS, seq_rows), lambda bi, j: (bi, 0, j))]
    out_shape = [jax.ShapeDtypeStruct((b, l, d), F32), jax.ShapeDtypeStruct((b, l, d), BF16),
                 jax.ShapeDtypeStruct((b, N_EXPERTS, l), F32)]
    if emit_state:
        out_specs.append(pl.BlockSpec((n_seq, 2, N_HEADS, DK, DV), lambda bi, j: (bi, 0, 0, 0, 0)))
        out_shape.append(jax.ShapeDtypeStruct((b, 2, N_HEADS, DK, DV), F32))
    kern = functools.partial(_mix_kernel, seq_len=l, n_seq=n_seq, seq_rows=seq_rows, use_rope=use_rope,
                             has_state=has_state, emit_state=emit_state)
    return pl.pallas_call(
        kern,
        grid=(b // n_seq, nblk),
        in_specs=in_specs,
        out_specs=out_specs,
        out_shape=out_shape,
        scratch_shapes=[pltpu.VMEM((n_seq, N_HEADS, DK, DV), F32)],
        compiler_params=_cparams(("arbitrary", "arbitrary")),
        name="mix_rope" if use_rope else "mix",
    )(*args)


def _route_kernel(aff_ref, wsel_ref, pos_ref, cnt_ref, *, cap):
    aff = aff_ref[...]
    rows, l = aff.shape

    def as_value(bits):
        return pltpu.bitcast(bits, F32)

    def bisect(i, tau):
        cand = tau | jnp.left_shift(jnp.int32(1), 30 - i)
        cnt = jnp.sum((aff >= as_value(cand)).astype(F32), axis=1, keepdims=True)
        return jnp.where(cnt >= cap, cand, tau)

    tau = lax.fori_loop(0, 31, bisect, jnp.zeros((rows, 1), jnp.int32))
    gt = aff >= as_value(tau + 1)
    eq = (aff >= as_value(tau)) & jnp.logical_not(gt)
    need = cap - jnp.sum(gt.astype(F32), axis=1, keepdims=True)

    nlb = l // ROUTE_LANES
    tri = (lax.broadcasted_iota(jnp.int32, (ROUTE_LANES, ROUTE_LANES), 0)
           < lax.broadcasted_iota(jnp.int32, (ROUTE_LANES, ROUTE_LANES), 1)).astype(BF16)

    def prefix(mask):
        carry = jnp.zeros((rows, 1), F32)
        parts, starts = [], []
        for blk in range(nlb):
            mb = mask[:, blk * ROUTE_LANES:(blk + 1) * ROUTE_LANES].astype(F32)
            starts.append(carry)
            parts.append(_dot(mb.astype(BF16), tri) + carry)
            carry = carry + jnp.sum(mb, axis=1, keepdims=True)
        starts.append(carry)
        return jnp.concatenate(parts, axis=1), starts

    eq_rank, _ = prefix(eq)
    sel = gt | (eq & (eq_rank < need))
    pos, starts = prefix(sel)
    wsel_ref[...] = jnp.where(sel, aff, 0.0)
    pos_ref[...] = jnp.where(sel, pos.astype(jnp.int32), -1)
    lane = lax.broadcasted_iota(jnp.int32, (rows, COUNT_LANES), 1)
    table = jnp.zeros((rows, COUNT_LANES), jnp.int32)
    per = max(GATHER_TOKENS // ROUTE_LANES, 1)
    for tb in range(nlb // per + 1):
        table = jnp.where(lane == tb, starts[min(tb * per, nlb)].astype(jnp.int32), table)
    cnt_ref[...] = table


def _route(aff_rows, cap, row_block):
    r, l = aff_rows.shape
    spec = pl.BlockSpec((row_block, l), lambda i: (i, 0))
    return pl.pallas_call(
        functools.partial(_route_kernel, cap=cap),
        grid=(r // row_block,),
        in_specs=[spec],
        out_specs=[spec, spec, pl.BlockSpec((row_block, COUNT_LANES), lambda i: (i, 0))],
        out_shape=[jax.ShapeDtypeStruct((r, l), F32), jax.ShapeDtypeStruct((r, l), jnp.int32),
                   jax.ShapeDtypeStruct((r, COUNT_LANES), jnp.int32)],
        compiler_params=_cparams(("arbitrary",)),
        name=f"route_{l}",
    )(aff_rows)


def _stacked_hits(pos_ref, req, cap):
    l = pos_ref.shape[1]
    slot = lax.broadcasted_iota(jnp.int32, (cap, l), 0)
    return [pos_ref[req * N_EXPERTS + e:req * N_EXPERTS + e + 1, :] == slot for e in range(N_EXPERTS)]


def _gather_small_kernel(h_ref, pos_ref, xs_ref, *, cap):
    for req in range(h_ref.shape[0]):
        onehot = jnp.concatenate([h.astype(BF16) for h in _stacked_hits(pos_ref, req, cap)], axis=0)
        xs = _dot(onehot, h_ref[req]).astype(BF16)
        for e in range(N_EXPERTS):
            xs_ref[e, req * cap:(req + 1) * cap, :] = xs[e * cap:(e + 1) * cap]


def _gather_small(h2, pos, cap):
    b, l, d = h2.shape
    nr = SMALL_REQS if b % SMALL_REQS == 0 else 1
    return pl.pallas_call(
        functools.partial(_gather_small_kernel, cap=cap),
        grid=(b // nr,),
        in_specs=[pl.BlockSpec((nr, l, d), lambda bi: (bi, 0, 0)),
                  pl.BlockSpec((nr * N_EXPERTS, l), lambda bi: (bi, 0))],
        out_specs=pl.BlockSpec((N_EXPERTS, nr * cap, d), lambda bi: (0, bi, 0)),
        out_shape=jax.ShapeDtypeStruct((N_EXPERTS, b * cap, d), BF16),
        compiler_params=_cparams(("arbitrary",)),
        name="gather_small",
    )(h2, pos)


def _slot_windows(cnt_ref, row, tb, cap):
    lo = cnt_ref[row * COUNT_LANES + tb]
    hi = cnt_ref[row * COUNT_LANES + tb + 1]
    start = jnp.minimum(lo & jnp.int32(-SLOT_ALIGN), jnp.int32(cap - GATHER_SLOTS))
    n_win = lax.shift_right_logical(hi - start + jnp.int32(GATHER_SLOTS - 1),
                                    jnp.int32(GATHER_SLOTS.bit_length() - 1))
    return pl.multiple_of(start, SLOT_ALIGN), n_win


def _next_window(start, k, cap):
    first = start + k * GATHER_SLOTS
    return pl.multiple_of(jnp.minimum(first, cap - GATHER_SLOTS), SLOT_ALIGN), first


def _window_hits(pos_row, start, first_slot=None):
    slot = start + lax.broadcasted_iota(jnp.int32, (GATHER_SLOTS, pos_row.shape[1]), 0)
    hit = pos_row == slot
    return hit if first_slot is None else hit & (slot >= first_slot)


def _gather_big_kernel(cnt_ref, h_ref, pos_ref, xs_ref, acc_ref, *, cap):
    bi, eg = pl.program_id(0), pl.program_id(1)
    n_tb = h_ref.shape[1] // GATHER_TOKENS
    acc_ref[...] = jnp.zeros(acc_ref.shape, F32)

    def onehot(el, tb, start, first_slot=None):
        toks = slice(tb * GATHER_TOKENS, (tb + 1) * GATHER_TOKENS)
        return _window_hits(pos_ref[pl.ds(eg * GATHER_EXPERTS + el, 1), toks], start, first_slot).astype(BF16)

    windows = {(tb, el): _slot_windows(cnt_ref, bi * N_EXPERTS + eg * GATHER_EXPERTS + el, tb, cap)
               for tb in range(n_tb) for el in range(GATHER_EXPERTS)}
    for tb in range(n_tb):
        toks = slice(tb * GATHER_TOKENS, (tb + 1) * GATHER_TOKENS)
        stacked = jnp.concatenate([onehot(el, tb, windows[tb, el][0]) for el in range(GATHER_EXPERTS)], axis=0)
        rows = _dot(stacked, h_ref[0, toks, :])
        for el in range(GATHER_EXPERTS):
            acc_ref[el, pl.ds(windows[tb, el][0], GATHER_SLOTS), :] += rows[el * GATHER_SLOTS:(el + 1) * GATHER_SLOTS]
    for (tb, el), (start, n_win) in windows.items():
        def extra(k, carry, tb=tb, el=el, start=start):
            clamped, first = _next_window(start, k, cap)
            toks = slice(tb * GATHER_TOKENS, (tb + 1) * GATHER_TOKENS)
            acc_ref[el, pl.ds(clamped, GATHER_SLOTS), :] += _dot(onehot(el, tb, clamped, first), h_ref[0, toks, :])
            return carry
        lax.fori_loop(1, n_win, extra, 0)
    xs_ref[...] = acc_ref[...].astype(BF16)


def _gather_big(h2, pos, cnt_flat, cap):
    b, l, d = h2.shape
    grid_spec = pltpu.PrefetchScalarGridSpec(
        num_scalar_prefetch=1,
        grid=(b, N_EXPERTS // GATHER_EXPERTS),
        in_specs=[pl.BlockSpec((1, l, d), lambda bi, eg, c: (bi, 0, 0)),
                  pl.BlockSpec((N_EXPERTS, l), lambda bi, eg, c: (bi, 0))],
        out_specs=pl.BlockSpec((GATHER_EXPERTS, cap, d), lambda bi, eg, c: (eg, bi, 0)),
        scratch_shapes=[pltpu.VMEM((GATHER_EXPERTS, cap, d), F32)],
    )
    return pl.pallas_call(
        functools.partial(_gather_big_kernel, cap=cap),
        grid_spec=grid_spec,
        out_shape=jax.ShapeDtypeStruct((N_EXPERTS, b * cap, d), BF16),
        compiler_params=_cparams(("arbitrary", "arbitrary")),
        name="gather_big",
    )(cnt_flat, h2, pos)


def _ffn_kernel(xa_ref, xb_ref, wg_ref, wu_ref, wd_ref, ya_ref, yb_ref, acca_ref, accb_ref):
    f, nf = pl.program_id(1), pl.num_programs(1)

    def ff_tile(first, last):
        wg = wg_ref[0].astype(BF16)
        wu = wu_ref[0].astype(BF16)
        wd = wd_ref[0].astype(BF16)
        for x_ref, y_ref, acc_ref in ((xa_ref, ya_ref, acca_ref), (xb_ref, yb_ref, accb_ref)):
            m = x_ref.shape[1]
            step = min(FFN_ROWS, m)
            for r0 in range(0, m, step):
                rows = slice(r0, r0 + step)
                x = x_ref[0, rows, :]
                hid = (_silu(_dot(x, wg)) * _dot(x, wu)).astype(BF16)
                part = _dot(hid, wd)
                if first:
                    acc_ref[rows, :] = part
                elif last:
                    y_ref[0, rows, :] = (acc_ref[rows, :] + part).astype(BF16)
                else:
                    acc_ref[rows, :] += part

    pl.when(f == 0)(functools.partial(ff_tile, True, False))
    pl.when((f > 0) & (f < nf - 1))(functools.partial(ff_tile, False, False))
    pl.when(f == nf - 1)(functools.partial(ff_tile, False, True))


def _ffn(xs_a, xs_b, w_gate, w_up, w_down):
    e, ma, d = xs_a.shape
    mb = xs_b.shape[1]
    ff = w_gate.shape[2]
    nf = ff // FF_TILE
    assert ff % FF_TILE == 0 and nf >= 2
    return pl.pallas_call(
        _ffn_kernel,
        grid=(e, nf),
        in_specs=[pl.BlockSpec((1, ma, d), lambda ei, f: (ei, 0, 0)),
                  pl.BlockSpec((1, mb, d), lambda ei, f: (ei, 0, 0)),
                  pl.BlockSpec((1, d, FF_TILE), lambda ei, f: (ei, 0, f)),
                  pl.BlockSpec((1, d, FF_TILE), lambda ei, f: (ei, 0, f)),
                  pl.BlockSpec((1, FF_TILE, d), lambda ei, f: (ei, f, 0))],
        out_specs=[pl.BlockSpec((1, ma, d), lambda ei, f: (ei, 0, 0)),
                   pl.BlockSpec((1, mb, d), lambda ei, f: (ei, 0, 0))],
        out_shape=[jax.ShapeDtypeStruct((e, ma, d), BF16), jax.ShapeDtypeStruct((e, mb, d), BF16)],
        scratch_shapes=[pltpu.VMEM((ma, d), F32), pltpu.VMEM((mb, d), F32)],
        compiler_params=_cparams(("arbitrary", "arbitrary")),
        name="ffn",
    )(xs_a, xs_b, w_gate, w_up, w_down)


def _slot_gate(hit, w_row):
    return jnp.sum(jnp.where(hit, w_row, 0.0), axis=1, keepdims=True)


def _finish(x1, moe, gate2, fn_g):
    x = x1 + gate2 * moe
    return (x * lax.rsqrt(jnp.mean(x * x, axis=-1, keepdims=True) + EPS)) * fn_g


def _combine_small_kernel(x1_ref, y_ref, pos_ref, w_ref, mod_ref, fn_ref, o_ref, *, cap):
    for req in range(x1_ref.shape[0]):
        hits = _stacked_hits(pos_ref, req, cap)
        gated = [(y_ref[e, req * cap:(req + 1) * cap, :].astype(F32)
                  * _slot_gate(hits[e], w_ref[req * N_EXPERTS + e:req * N_EXPERTS + e + 1, :])).astype(BF16)
                 for e in range(N_EXPERTS)]
        onehot = jnp.concatenate([h.astype(BF16) for h in hits], axis=0)
        moe = _dot_tn(onehot, jnp.concatenate(gated, axis=0))
        o_ref[req] = _finish(x1_ref[req], moe, mod_ref[req, 5:6], fn_ref[...])


def _combine_small(x1, y, pos, wsel, mod, fn_g, cap):
    b, l, d = x1.shape
    nr = SMALL_REQS if b % SMALL_REQS == 0 else 1
    return pl.pallas_call(
        functools.partial(_combine_small_kernel, cap=cap),
        grid=(b // nr,),
        in_specs=[pl.BlockSpec((nr, l, d), lambda bi: (bi, 0, 0)),
                  pl.BlockSpec((N_EXPERTS, nr * cap, d), lambda bi: (0, bi, 0)),
                  pl.BlockSpec((nr * N_EXPERTS, l), lambda bi: (bi, 0)),
                  pl.BlockSpec((nr * N_EXPERTS, l), lambda bi: (bi, 0)),
                  pl.BlockSpec((nr, 8, d), lambda bi: (bi, 0, 0)),
                  pl.BlockSpec((1, d), lambda bi: (0, 0))],
        out_specs=pl.BlockSpec((nr, l, d), lambda bi: (bi, 0, 0)),
        out_shape=jax.ShapeDtypeStruct((b, l, d), F32),
        compiler_params=_cparams(("arbitrary",)),
        name="combine_small",
    )(x1, y, pos, wsel, mod, fn_g)


def _combine_big_kernel(cnt_ref, x1_ref, y_ref, pos_ref, w_ref, mod_ref, fn_ref, o_ref, acc_ref, *, cap):
    bi, step = pl.program_id(0), pl.program_id(1)

    def window_terms(e, toks, start, first_slot=None):
        hit = _window_hits(pos_ref[e:e + 1, toks], start, first_slot)
        y = y_ref[e, pl.ds(start, GATHER_SLOTS), :].astype(F32)
        return hit.astype(BF16), (y * _slot_gate(hit, w_ref[e:e + 1, toks])).astype(BF16)

    blocks = []
    for sub in range(COMBINE_BLOCKS):
        toks = slice(sub * GATHER_TOKENS, (sub + 1) * GATHER_TOKENS)
        tb = step * COMBINE_BLOCKS + sub
        windows = [_slot_windows(cnt_ref, bi * N_EXPERTS + e, tb, cap) for e in range(N_EXPERTS)]
        terms = [window_terms(e, toks, start) for e, (start, _) in enumerate(windows)]
        acc_ref[toks, :] = _dot_tn(jnp.concatenate([t[0] for t in terms], axis=0),
                                   jnp.concatenate([t[1] for t in terms], axis=0))
        blocks.append((toks, windows))
    for toks, windows in blocks:
        for e, (start, n_win) in enumerate(windows):
            def extra(k, carry, e=e, toks=toks, start=start):
                hit, gated = window_terms(e, toks, *_next_window(start, k, cap))
                acc_ref[toks, :] += _dot_tn(hit, gated)
                return carry
            lax.fori_loop(1, n_win, extra, 0)
    o_ref[0] = _finish(x1_ref[0], acc_ref[...], mod_ref[0, 5:6], fn_ref[...])


def _combine_big(x1, y, pos, wsel, cnt_flat, mod, fn_g, cap):
    b, l, d = x1.shape
    span = COMBINE_BLOCKS * GATHER_TOKENS
    assert l % span == 0
    grid_spec = pltpu.PrefetchScalarGridSpec(
        num_scalar_prefetch=1,
        grid=(b, l // span),
        in_specs=[pl.BlockSpec((1, span, d), lambda bi, tb, c: (bi, tb, 0)),
                  pl.BlockSpec((N_EXPERTS, cap, d), lambda bi, tb, c: (0, bi, 0)),
                  pl.BlockSpec((N_EXPERTS, span), lambda bi, tb, c: (bi, tb)),
                  pl.BlockSpec((N_EXPERTS, span), lambda bi, tb, c: (bi, tb)),
                  pl.BlockSpec((1, 8, d), lambda bi, tb, c: (bi, 0, 0)),
                  pl.BlockSpec((1, d), lambda bi, tb, c: (0, 0))],
        out_specs=pl.BlockSpec((1, span, d), lambda bi, tb, c: (bi, tb, 0)),
        scratch_shapes=[pltpu.VMEM((span, d), F32)],
    )
    return pl.pallas_call(
        functools.partial(_combine_big_kernel, cap=cap),
        grid_spec=grid_spec,
        out_shape=jax.ShapeDtypeStruct((b, l, d), F32),
        compiler_params=_cparams(("arbitrary", "arbitrary")),
        name="combine_big",
    )(cnt_flat, x1, y, pos, wsel, mod, fn_g)


def _router_parts(w):
    hi, lo = _hi_lo(w)
    pad = ((0, 0), (0, LANES - w.shape[1]))
    return jnp.concatenate([jnp.pad(hi, pad), jnp.pad(lo, pad)], axis=1)


def _rope_tables(l):
    rows = l // GRID_W
    row = jnp.repeat(jnp.arange(rows, dtype=F32), GRID_W)
    col = jnp.tile(jnp.arange(GRID_W, dtype=F32), rows)
    nf = DK // 4
    inv = ROPE_BASE ** (-jnp.arange(nf, dtype=F32) / nf)
    ang = jnp.concatenate([row[:, None] * inv, col[:, None] * inv], axis=-1)
    cos = jnp.repeat(jnp.cos(ang), 2, axis=-1)
    sin = jnp.repeat(jnp.sin(ang), 2, axis=-1) * jnp.tile(jnp.asarray([-1.0, 1.0], F32), DK // 2)
    return cos, sin


def kernel(x_prompt, x_sample, c, state_ret, c_ctx, w_ada, b_ada, norm1_g, w_in, w_pool, pool_scale, w_pool_out,
           ret_decay, ret_gn_g, w_ret_out, w_branch_gate, b_branch_gate, w_out, norm2_g, w_router, w_exp_gate,
           w_exp_up, w_exp_down, final_norm_g):
    depth = w_ada.shape[0]
    assert depth == 1, "single trunk layer"
    bc, lc, d = x_prompt.shape
    bl, ll, _ = x_sample.shape
    assert lc % CHUNK == 0 and STEP_ROWS % lc == 0 and ll % STEP_ROWS == 0 and ll % GATHER_TOKENS == 0
    cap_c = max(1, CAPACITY_FACTOR * lc // N_EXPERTS)
    cap_l = max(1, CAPACITY_FACTOR * ll // N_EXPERTS)
    assert cap_l % GATHER_SLOTS == 0 and ll // GATHER_TOKENS < COUNT_LANES

    cond = jnp.zeros((8 * pl.cdiv(bl + 1, 8), d), F32).at[:bl].set(c).at[bl].set(c_ctx)
    ada = _adaln(cond, w_ada[0], b_ada[0]).reshape(-1, 6, d)
    ada = jnp.pad(ada, ((0, 0), (0, 2), (0, 0)))
    mod_l = ada[:bl]
    mod_c = jnp.broadcast_to(ada[bl:bl + 1], (bc, 8, d))

    p = dict(
        n1g=norm1_g[0][None], w_in=w_in[0].astype(BF16), w_bg=w_branch_gate[0].astype(BF16),
        b_bg=b_branch_gate[0][None], w_pool=w_pool[0].astype(BF16), pool_scale=pool_scale[0][None],
        w_pool_out=w_pool_out[0].astype(BF16), gn_g=ret_gn_g[0][None], w_ret_out=w_ret_out[0].astype(BF16),
        w_out=w_out[0].astype(BF16), n2g=norm2_g[0][None], w_router_parts=_router_parts(w_router[0]),
    )
    decay = ret_decay[0]
    fn_g = final_norm_g[None]

    x1_c, h2_c, aff_c, st_c = _mix(x_prompt, mod_c, None, None, decay, p)
    rope = _rope_tables(ll)
    sb, kv = _revscan(x_sample, mod_l, rope[0], rope[1], state_ret, decay, p["n1g"], p["w_in"])
    x1_l, h2_l, aff_l = _mix(x_sample, mod_l, rope, (state_ret, sb, kv), decay, p)

    wsel_c, pos_c, _ = _route(aff_c.reshape(bc * N_EXPERTS, lc), cap_c, min(ROUTE_ROWS, bc * N_EXPERTS))
    wsel_l, pos_l, cnt_l = _route(aff_l.reshape(bl * N_EXPERTS, ll), cap_l, min(ROUTE_ROWS, bl * N_EXPERTS))
    cnt_flat = cnt_l.reshape(-1)

    xs_c = _gather_small(h2_c, pos_c, cap_c)
    xs_l = _gather_big(h2_l, pos_l, cnt_flat, cap_l)
    y_c, y_l = _ffn(xs_c, xs_l, w_exp_gate[0], w_exp_up[0], w_exp_down[0])

    y_prompt = _combine_small(x1_c, y_c, pos_c, wsel_c, mod_c, fn_g, cap_c)
    y_sample = _combine_big(x1_l, y_l, pos_l, wsel_l, cnt_flat, mod_l, fn_g, cap_l)
    new_state = st_c[:, None].astype(x_prompt.dtype)
    return (y_prompt, y_sample, new_state)
```

```python
import functools

import jax
import jax.numpy as jnp
from jax import lax
from jax.experimental import pallas as pl
from jax.experimental.pallas import tpu as pltpu

F32 = jnp.float32
BF16 = jnp.bfloat16

N_HEADS = 4
DK = 128
DV = 256
POOL_WINDOWS = (2, 4, 8, 16)
POOL_GROUP = 128
POOL_WIDTH = POOL_GROUP * len(POOL_WINDOWS)
QK_WIDTH = N_HEADS * DK
V_WIDTH = N_HEADS * DV
N_EXPERTS = 16
CAPACITY_FACTOR = 2
GRID_W = 64
ROPE_BASE = 10000.0
EPS = 1e-6

LANES = 128
CHUNK = 256
STEP_ROWS = 512
HALO = 16
ROUTE_LANES = 256
ROUTE_ROWS = 512
SMALL_REQS = 4
GATHER_TOKENS = 256
GATHER_SLOTS = 64
GATHER_EXPERTS = 4
COMBINE_BLOCKS = 2
COUNT_LANES = 128
SLOT_ALIGN = 16
FF_TILE = 256
FFN_ROWS = 512
VMEM_LIMIT = 56 * 1024 * 1024


def _cparams(sem):
    return pltpu.CompilerParams(dimension_semantics=sem, vmem_limit_bytes=VMEM_LIMIT)


def _resident(shape):
    return pl.BlockSpec(shape, lambda *_: (0,) * len(shape), pipeline_mode=pl.Buffered(1))


def _sigmoid(x):
    return 0.5 * jnp.tanh(0.5 * x) + 0.5


def _silu(x):
    return x * _sigmoid(x)


def _norm_mod(x, g, scale, shift):
    y = x * lax.rsqrt(jnp.mean(x * x, axis=-1, keepdims=True) + EPS)
    return (y * g) * (1.0 + scale) + shift


def _hi_lo(x):
    hi = x.astype(BF16)
    return hi, (x - hi.astype(F32)).astype(BF16)


def _dot(a, b):
    return jnp.dot(a, b, preferred_element_type=F32)


def _dot_nt(a, b):
    return lax.dot_general(a, b, (((1,), (1,)), ((), ())), preferred_element_type=F32)


def _dot_tn(a, b):
    return lax.dot_general(a, b, (((0,), (0,)), ((), ())), preferred_element_type=F32)


def _adaln_kernel(c_ref, w_ref, b_ref, o_ref):
    rows = c_ref.shape[0]
    s_hi, s_lo = _hi_lo(_silu(c_ref[...]))
    w_hi, w_lo = _hi_lo(w_ref[...])
    by_hi = _dot(jnp.concatenate([s_hi, s_lo], axis=0), w_hi)
    o_ref[...] = by_hi[:rows] + by_hi[rows:] + _dot(s_hi, w_lo) + b_ref[...]


def _adaln(cond, w, b):
    rows, d = cond.shape
    n = w.shape[1]
    tn = d
    return pl.pallas_call(
        _adaln_kernel,
        grid=(n // tn,),
        in_specs=[pl.BlockSpec((rows, d), lambda i: (0, 0)),
                  pl.BlockSpec((d, tn), lambda i: (0, i)),
                  pl.BlockSpec((1, tn), lambda i: (0, i))],
        out_specs=pl.BlockSpec((rows, tn), lambda i: (0, i)),
        out_shape=jax.ShapeDtypeStruct((rows, n), F32),
        compiler_params=_cparams(("arbitrary",)),
        name="adaln",
    )(cond, w, b.reshape(1, n))


def _decay_tiles(decay_ref, head):
    def log_gamma(direction, shape):
        return -jnp.exp(jnp.full(shape, decay_ref[direction, head], F32))

    diff = (lax.broadcasted_iota(jnp.int32, (CHUNK, CHUNK), 0)
            - lax.broadcasted_iota(jnp.int32, (CHUNK, CHUNK), 1)).astype(F32)
    i = lax.broadcasted_iota(jnp.int32, (CHUNK, DK), 0).astype(F32)
    lgf, lgb = log_gamma(0, (CHUNK, DK)), log_gamma(1, (CHUNK, DK))
    return dict(
        mask=jnp.where(diff >= 0.0, jnp.exp(log_gamma(0, (CHUNK, CHUNK)) * jnp.maximum(diff, 0.0)),
                       jnp.exp(log_gamma(1, (CHUNK, CHUNK)) * jnp.maximum(-diff, 0.0))),
        qdf=jnp.exp(lgf * (i + 1.0)),
        qdb=jnp.exp(lgb * (CHUNK - i)),
        kdf=jnp.exp(lgf * (CHUNK - 1.0 - i)),
        kdb=jnp.exp(lgb * i),
        cdf=jnp.exp(log_gamma(0, (DK, DV)) * float(CHUNK)),
        cdb=jnp.exp(log_gamma(1, (DK, DV)) * float(CHUNK)),
    )


def _rope(x, cos, sin):
    even = (lax.broadcasted_iota(jnp.int32, x.shape, 1) % 2) == 0
    partner = jnp.where(even, pltpu.roll(x, x.shape[1] - 1, 1), pltpu.roll(x, 1, 1))
    return x * cos + partner * sin


def _state_spec(direction):
    return pl.BlockSpec((1, 1, 1, N_HEADS, DK, DV), lambda bi, j: (bi, 0, direction, 0, 0, 0))


def _revscan_kernel(x_ref, mod_ref, cos_ref, sin_ref, s0_ref, decay_ref, n1_ref, w_in_ref, sb_ref, kv_ref, s_scr):
    j = pl.program_id(1)

    @pl.when(j == 0)
    def _():
        s_scr[...] = s0_ref[0, 0, 0]

    mod = mod_ref[0]
    h = _norm_mod(x_ref[0], n1_ref[...], mod[1:2], mod[0:1]).astype(BF16)
    k_lo = POOL_WIDTH + QK_WIDTH
    kv = _dot(h, w_in_ref[:, k_lo:k_lo + QK_WIDTH + V_WIDTH])
    sb_ref[0, 0] = s_scr[...]
    kv_ref[0, :, QK_WIDTH:] = kv[:, QK_WIDTH:].astype(BF16)
    for head in range(N_HEADS):
        t = _decay_tiles(decay_ref, head)
        s = s_scr[head]
        for c in reversed(range(STEP_ROWS // CHUNK)):
            rows = slice(c * CHUNK, (c + 1) * CHUNK)
            cols = slice(head * DK, (head + 1) * DK)
            k = _rope(kv[rows, cols] * (DK ** -0.5), cos_ref[rows], sin_ref[rows])
            kv_ref[0, rows, cols] = k.astype(BF16)
            v = kv[rows, QK_WIDTH + head * DV:QK_WIDTH + (head + 1) * DV]
            s = t["cdb"] * s + _dot_tn((k * t["kdb"]).astype(BF16), v.astype(BF16))
        s_scr[head] = s


def _revscan(x, mod, cos, sin, state_ret, decay, n1g, w_in):
    b, l, d = x.shape
    nblk = l // STEP_ROWS
    return pl.pallas_call(
        _revscan_kernel,
        grid=(b, nblk),
        in_specs=[pl.BlockSpec((1, STEP_ROWS, d), lambda bi, j: (bi, nblk - 1 - j, 0)),
                  pl.BlockSpec((1, 8, d), lambda bi, j: (bi, 0, 0)),
                  pl.BlockSpec((STEP_ROWS, DK), lambda bi, j: (nblk - 1 - j, 0)),
                  pl.BlockSpec((STEP_ROWS, DK), lambda bi, j: (nblk - 1 - j, 0)),
                  _state_spec(1),
                  pl.BlockSpec(memory_space=pltpu.SMEM),
                  _resident(n1g.shape),
                  _resident(w_in.shape)],
        out_specs=[pl.BlockSpec((1, 1, N_HEADS, DK, DV), lambda bi, j: (bi, nblk - 1 - j, 0, 0, 0)),
                   pl.BlockSpec((1, STEP_ROWS, QK_WIDTH + V_WIDTH), lambda bi, j: (bi, nblk - 1 - j, 0))],
        out_shape=[jax.ShapeDtypeStruct((b, nblk, N_HEADS, DK, DV), F32),
                   jax.ShapeDtypeStruct((b, l, QK_WIDTH + V_WIDTH), BF16)],
        scratch_shapes=[pltpu.VMEM((N_HEADS, DK, DV), F32)],
        compiler_params=_cparams(("arbitrary", "arbitrary")),
        name="revscan",
    )(x, mod, cos, sin, state_ret, decay, n1g, w_in)


def _mix_kernel(*refs, seq_len, n_seq, seq_rows, use_rope, has_state, emit_state):
    it = iter(refs)
    x_ref, xp_ref, xn_ref, mod_ref = next(it), next(it), next(it), next(it)
    cos_ref = sin_ref = s0f_ref = sb_ref = kv_ref = None
    if use_rope:
        cos_ref, sin_ref = next(it), next(it)
    if has_state:
        s0f_ref, sb_ref, kv_ref = next(it), next(it), next(it)
    (decay_ref, n1_ref, w_in_ref, w_bg_ref, b_bg_ref, w_pool_ref, pscale_ref, w_po_ref, gn_ref, w_ro_ref,
     w_out_ref, n2_ref, w_rt_ref) = (next(it) for _ in range(13))
    x1_ref, h2_ref, aff_ref = next(it), next(it), next(it)
    st_ref = next(it) if emit_state else None
    sf_scr = next(it)

    j = pl.program_id(1)
    d = x_ref.shape[2]
    n_chunks = seq_rows // CHUNK
    ext = seq_rows + 2 * HALO

    @pl.when(j == 0)
    def _():
        if has_state:
            sf_scr[...] = s0f_ref[:, 0, 0]
        else:
            sf_scr[...] = jnp.zeros(sf_scr.shape, F32)

    mods = [mod_ref[s] for s in range(n_seq)]

    def rows_of(vals):
        return jnp.concatenate([jnp.broadcast_to(v, (seq_rows, d)) for v in vals], axis=0) if n_seq > 1 else vals[0]

    he_parts = []
    for s in range(n_seq):
        xe = jnp.concatenate([xp_ref[s], x_ref[s], xn_ref[s]], axis=0)
        he_parts.append(_norm_mod(xe, n1_ref[...], mods[s][1:2], mods[s][0:1]).astype(BF16))
    he = jnp.concatenate(he_parts, axis=0) if n_seq > 1 else he_parts[0]
    hb_parts = [hp[HALO:HALO + seq_rows] for hp in he_parts]
    hb = jnp.concatenate(hb_parts, axis=0) if n_seq > 1 else hb_parts[0]
    x = jnp.concatenate([x_ref[s] for s in range(n_seq)], axis=0) if n_seq > 1 else x_ref[0]
    gates = _sigmoid(_dot(hb, w_bg_ref[...]) + b_bg_ref[...])

    ue_all = _dot(he, w_in_ref[:, :POOL_WIDTH])
    epos = j * seq_rows - HALO + lax.broadcasted_iota(jnp.int32, (ext, 1), 0)
    valid = (epos >= 0) & (epos < seq_len)
    tpos = j * seq_rows + lax.broadcasted_iota(jnp.int32, (seq_rows, 1), 0)
    pooled = [[] for _ in POOL_WINDOWS]
    for s in range(n_seq):
        ue = jnp.where(valid, ue_all[s * ext:(s + 1) * ext], 0.0)
        for gi, w in enumerate(POOL_WINDOWS):
            ug = ue[:, gi * POOL_GROUP:(gi + 1) * POOL_GROUP]
            acc, shift = ug, 1
            while shift < w:
                acc = acc + pltpu.roll(acc, shift, 0)
                shift *= 2
            if w // 2 > 1:
                acc = pltpu.roll(acc, ext - (w // 2 - 1), 0)
            cnt = (jnp.minimum(tpos + w // 2, seq_len) - jnp.maximum(tpos - w // 2, 0)).astype(F32)
            own = slice(HALO, HALO + seq_rows)
            pooled[gi].append((acc[own] / cnt - ug[own]).astype(BF16))
    pool_h = jnp.concatenate(
        [_dot(jnp.concatenate(pg, axis=0) if len(pg) > 1 else pg[0], w_pool_ref[gi]) for gi, pg in enumerate(pooled)],
        axis=1) * pscale_ref[...]
    pool_y = _dot(pool_h.astype(BF16), w_po_ref[...])

    g_lo = POOL_WIDTH + 2 * QK_WIDTH + V_WIDTH
    if has_state:
        zq = _dot(hb, w_in_ref[:, POOL_WIDTH:POOL_WIDTH + QK_WIDTH])
        g_all = _dot(hb, w_in_ref[:, g_lo:])
    else:
        zr = _dot(hb, w_in_ref[:, POOL_WIDTH:])
        zq, g_all = zr[:, :QK_WIDTH], zr[:, 2 * QK_WIDTH + V_WIDTH:]
    seq_out = []
    for s in range(n_seq):
        r0 = s * seq_rows
        head_out = []
        for head in range(N_HEADS):
            t = _decay_tiles(decay_ref, head)
            qs, ks, vs = [], [], []
            for c in range(n_chunks):
                rows = slice(r0 + c * CHUNK, r0 + (c + 1) * CHUNK)
                crow = slice(c * CHUNK, (c + 1) * CHUNK)
                q = zq[rows, head * DK:(head + 1) * DK]
                if use_rope:
                    q = _rope(q, cos_ref[crow], sin_ref[crow])
                if has_state:
                    k = kv_ref[s, crow, head * DK:(head + 1) * DK].astype(F32)
                    v = kv_ref[s, crow, QK_WIDTH + head * DV:QK_WIDTH + (head + 1) * DV]
                else:
                    k = zr[rows, QK_WIDTH + head * DK:QK_WIDTH + (head + 1) * DK] * (DK ** -0.5)
                    if use_rope:
                        k = _rope(k, cos_ref[crow], sin_ref[crow])
                    v = zr[rows, 2 * QK_WIDTH + head * DV:2 * QK_WIDTH + (head + 1) * DV].astype(BF16)
                qs.append(q)
                ks.append(k)
                vs.append(v)
            sf = [sf_scr[s, head]]
            for c in range(n_chunks):
                sf.append(t["cdf"] * sf[c] + _dot_tn((ks[c] * t["kdf"]).astype(BF16), vs[c]))
            sf_scr[s, head] = sf[n_chunks]
            sb = [None] * (n_chunks + 1)
            sb[n_chunks] = sb_ref[s, 0, head] if has_state else jnp.zeros((DK, DV), F32)
            for c in reversed(range(n_chunks)):
                sb[c] = t["cdb"] * sb[c + 1] + _dot_tn((ks[c] * t["kdb"]).astype(BF16), vs[c])
            if emit_state:
                st_ref[s, 0, head] = sf[n_chunks]
                st_ref[s, 1, head] = sb[0]
            outs = []
            for c in range(n_chunks):
                scores = _dot_nt(qs[c].astype(BF16), ks[c].astype(BF16)) * t["mask"]
                q_both = jnp.concatenate([(qs[c] * t["qdf"]).astype(BF16), (qs[c] * t["qdb"]).astype(BF16)], axis=1)
                s_both = jnp.concatenate([sf[c].astype(BF16), sb[c + 1].astype(BF16)], axis=0)
                outs.append(_dot(scores.astype(BF16), vs[c]) + _dot(q_both, s_both))
            o = jnp.concatenate(outs, axis=0)
            mu = jnp.mean(o, axis=-1, keepdims=True)
            oc = o - mu
            var = jnp.mean(oc * oc, axis=-1, keepdims=True)
            head_out.append(oc * lax.rsqrt(var + EPS))
        seq_out.append(jnp.concatenate(head_out, axis=1))
    o_n = (jnp.concatenate(seq_out, axis=0) if n_seq > 1 else seq_out[0]) * gn_ref[...]
    ret_y = _dot((_silu(g_all) * o_n).astype(BF16), w_ro_ref[...])

    merged = gates[:, :d] * pool_y + gates[:, d:] * ret_y
    x1 = x + rows_of([m[2:3] for m in mods]) * _dot(merged.astype(BF16), w_out_ref[...])

    h2 = _norm_mod(x1, n2_ref[...], rows_of([m[4:5] for m in mods]), rows_of([m[3:4] for m in mods]))
    h2_hi, h2_lo = _hi_lo(h2)
    n_rows = n_seq * seq_rows
    parts = _dot(jnp.concatenate([h2_hi, h2_lo], axis=0), w_rt_ref[...])
    logits_t = (parts[:n_rows, :LANES] + parts[n_rows:, :LANES] + parts[:n_rows, LANES:])
    for s in range(n_seq):
        rows = slice(s * seq_rows, (s + 1) * seq_rows)
        x1_ref[s] = x1[rows]
        h2_ref[s] = h2_hi[rows]
        logits = logits_t[rows].T[:N_EXPERTS]
        e = jnp.exp(logits - jnp.max(logits, axis=0, keepdims=True))
        aff_ref[s] = e / jnp.sum(e, axis=0, keepdims=True)


def _mix(x, mod, rope, states, decay, p):
    b, l, d = x.shape
    seq_rows = min(l, STEP_ROWS)
    n_seq = STEP_ROWS // seq_rows
    nblk = l // seq_rows
    hb = seq_rows // HALO
    n_halo = l // HALO
    use_rope, has_state = rope is not None, states is not None
    emit_state = not has_state
    assert b % n_seq == 0 and (n_seq == 1 or not (use_rope or has_state))
    in_specs = [pl.BlockSpec((n_seq, seq_rows, d), lambda bi, j: (bi, j, 0)),
                pl.BlockSpec((n_seq, HALO, d), lambda bi, j: (bi, jnp.maximum(j * hb - 1, 0), 0)),
                pl.BlockSpec((n_seq, HALO, d), lambda bi, j: (bi, jnp.minimum((j + 1) * hb, n_halo - 1), 0)),
                pl.BlockSpec((n_seq, 8, d), lambda bi, j: (bi, 0, 0))]
    args = [x, x, x, mod]
    if use_rope:
        in_specs += [pl.BlockSpec((seq_rows, DK), lambda bi, j: (j, 0))] * 2
        args += list(rope)
    if has_state:
        in_specs += [_state_spec(0),
                     pl.BlockSpec((1, 1, N_HEADS, DK, DV), lambda bi, j: (bi, j, 0, 0, 0)),
                     pl.BlockSpec((1, seq_rows, QK_WIDTH + V_WIDTH), lambda bi, j: (bi, j, 0))]
        args += list(states)
    weights = [p["n1g"], p["w_in"], p["w_bg"], p["b_bg"], p["w_pool"], p["pool_scale"], p["w_pool_out"],
               p["gn_g"], p["w_ret_out"], p["w_out"], p["n2g"], p["w_router_parts"]]
    in_specs.append(pl.BlockSpec(memory_space=pltpu.SMEM))
    args.append(decay)
    for w in weights:
        in_specs.append(_resident(w.shape))
        args.append(w)
    out_specs = [pl.BlockSpec((n_seq, seq_rows, d), lambda bi, j: (bi, j, 0)),
                 pl.BlockSpec((n_seq, seq_rows, d), lambda bi, j: (bi, j, 0)),
                 pl.BlockSpec((n_seq, N_EXPERTS, seq_rows), lambda bi, j: (bi, 0, j))]
    out_shape = [jax.ShapeDtypeStruct((b, l, d), F32), jax.ShapeDtypeStruct((b, l, d), BF16),
                 jax.ShapeDtypeStruct((b, N_EXPERTS, l), F32)]
    if emit_state:
        out_specs.append(pl.BlockSpec((n_seq, 2, N_HEADS, DK, DV), lambda bi, j: (bi, 0, 0, 0, 0)))
        out_shape.append(jax.ShapeDtypeStruct((b, 2, N_HEADS, DK, DV), F32))
    kern = functools.partial(_mix_kernel, seq_len=l, n_seq=n_seq, seq_rows=seq_rows, use_rope=use_rope,
                             has_state=has_state, emit_state=emit_state)
    return pl.pallas_call(
        kern,
        grid=(b // n_seq, nblk),
        in_specs=in_specs,
        out_specs=out_specs,
        out_shape=out_shape,
        scratch_shapes=[pltpu.VMEM((n_seq, N_HEADS, DK, DV), F32)],
        compiler_params=_cparams(("arbitrary", "arbitrary")),
        name="mix_rope" if use_rope else "mix",
    )(*args)


def _route_kernel(aff_ref, wsel_ref, pos_ref, cnt_ref, *, cap):
    aff = aff_ref[...]
    rows, l = aff.shape

    def as_value(bits):
        return pltpu.bitcast(bits, F32)

    def bisect(i, tau):
        cand = tau | jnp.left_shift(jnp.int32(1), 30 - i)
        cnt = jnp.sum((aff >= as_value(cand)).astype(F32), axis=1, keepdims=True)
        return jnp.where(cnt >= cap, cand, tau)

    tau = lax.fori_loop(0, 31, bisect, jnp.zeros((rows, 1), jnp.int32))
    gt = aff >= as_value(tau + 1)
    eq = (aff >= as_value(tau)) & jnp.logical_not(gt)
    need = cap - jnp.sum(gt.astype(F32), axis=1, keepdims=True)

    nlb = l // ROUTE_LANES
    tri = (lax.broadcasted_iota(jnp.int32, (ROUTE_LANES, ROUTE_LANES), 0)
           < lax.broadcasted_iota(jnp.int32, (ROUTE_LANES, ROUTE_LANES), 1)).astype(BF16)

    def prefix(mask):
        carry = jnp.zeros((rows, 1), F32)
        parts, starts = [], []
        for blk in range(nlb):
            mb = mask[:, blk * ROUTE_LANES:(blk + 1) * ROUTE_LANES].astype(F32)
            starts.append(carry)
            parts.append(_dot(mb.astype(BF16), tri) + carry)
            carry = carry + jnp.sum(mb, axis=1, keepdims=True)
        starts.append(carry)
        return jnp.concatenate(parts, axis=1), starts

    eq_rank, _ = prefix(eq)
    sel = gt | (eq & (eq_rank < need))
    pos, starts = prefix(sel)
    wsel_ref[...] = jnp.where(sel, aff, 0.0)
    pos_ref[...] = jnp.where(sel, pos.astype(jnp.int32), -1)
    lane = lax.broadcasted_iota(jnp.int32, (rows, COUNT_LANES), 1)
    table = jnp.zeros((rows, COUNT_LANES), jnp.int32)
    per = max(GATHER_TOKENS // ROUTE_LANES, 1)
    for tb in range(nlb // per + 1):
        table = jnp.where(lane == tb, starts[min(tb * per, nlb)].astype(jnp.int32), table)
    cnt_ref[...] = table


def _route(aff_rows, cap, row_block):
    r, l = aff_rows.shape
    spec = pl.BlockSpec((row_block, l), lambda i: (i, 0))
    return pl.pallas_call(
        functools.partial(_route_kernel, cap=cap),
        grid=(r // row_block,),
        in_specs=[spec],
        out_specs=[spec, spec, pl.BlockSpec((row_block, COUNT_LANES), lambda i: (i, 0))],
        out_shape=[jax.ShapeDtypeStruct((r, l), F32), jax.ShapeDtypeStruct((r, l), jnp.int32),
                   jax.ShapeDtypeStruct((r, COUNT_LANES), jnp.int32)],
        compiler_params=_cparams(("arbitrary",)),
        name=f"route_{l}",
    )(aff_rows)


def _stacked_hits(pos_ref, req, cap):
    l = pos_ref.shape[1]
    slot = lax.broadcasted_iota(jnp.int32, (cap, l), 0)
    return [pos_ref[req * N_EXPERTS + e:req * N_EXPERTS + e + 1, :] == slot for e in range(N_EXPERTS)]


def _gather_small_kernel(h_ref, pos_ref, xs_ref, *, cap):
    for req in range(h_ref.shape[0]):
        onehot = jnp.concatenate([h.astype(BF16) for h in _stacked_hits(pos_ref, req, cap)], axis=0)
        xs = _dot(onehot, h_ref[req]).astype(BF16)
        for e in range(N_EXPERTS):
            xs_ref[e, req * cap:(req + 1) * cap, :] = xs[e * cap:(e + 1) * cap]


def _gather_small(h2, pos, cap):
    b, l, d = h2.shape
    nr = SMALL_REQS if b % SMALL_REQS == 0 else 1
    return pl.pallas_call(
        functools.partial(_gather_small_kernel, cap=cap),
        grid=(b // nr,),
        in_specs=[pl.BlockSpec((nr, l, d), lambda bi: (bi, 0, 0)),
                  pl.BlockSpec((nr * N_EXPERTS, l), lambda bi: (bi, 0))],
        out_specs=pl.BlockSpec((N_EXPERTS, nr * cap, d), lambda bi: (0, bi, 0)),
        out_shape=jax.ShapeDtypeStruct((N_EXPERTS, b * cap, d), BF16),
        compiler_params=_cparams(("arbitrary",)),
        name="gather_small",
    )(h2, pos)


def _slot_windows(cnt_ref, row, tb, cap):
    lo = cnt_ref[row * COUNT_LANES + tb]
    hi = cnt_ref[row * COUNT_LANES + tb + 1]
    start = jnp.minimum(lo & jnp.int32(-SLOT_ALIGN), jnp.int32(cap - GATHER_SLOTS))
    n_win = lax.shift_right_logical(hi - start + jnp.int32(GATHER_SLOTS - 1),
                                    jnp.int32(GATHER_SLOTS.bit_length() - 1))
    return pl.multiple_of(start, SLOT_ALIGN), n_win


def _next_window(start, k, cap):
    first = start + k * GATHER_SLOTS
    return pl.multiple_of(jnp.minimum(first, cap - GATHER_SLOTS), SLOT_ALIGN), first


def _window_hits(pos_row, start, first_slot=None):
    slot = start + lax.broadcasted_iota(jnp.int32, (GATHER_SLOTS, pos_row.shape[1]), 0)
    hit = pos_row == slot
    return hit if first_slot is None else hit & (slot >= first_slot)


def _gather_big_kernel(cnt_ref, h_ref, pos_ref, xs_ref, acc_ref, *, cap):
    bi, eg = pl.program_id(0), pl.program_id(1)
    n_tb = h_ref.shape[1] // GATHER_TOKENS
    acc_ref[...] = jnp.zeros(acc_ref.shape, F32)

    def onehot(el, tb, start, first_slot=None):
        toks = slice(tb * GATHER_TOKENS, (tb + 1) * GATHER_TOKENS)
        return _window_hits(pos_ref[pl.ds(eg * GATHER_EXPERTS + el, 1), toks], start, first_slot).astype(BF16)

    windows = {(tb, el): _slot_windows(cnt_ref, bi * N_EXPERTS + eg * GATHER_EXPERTS + el, tb, cap)
               for tb in range(n_tb) for el in range(GATHER_EXPERTS)}
    for tb in range(n_tb):
        toks = slice(tb * GATHER_TOKENS, (tb + 1) * GATHER_TOKENS)
        stacked = jnp.concatenate([onehot(el, tb, windows[tb, el][0]) for el in range(GATHER_EXPERTS)], axis=0)
        rows = _dot(stacked, h_ref[0, toks, :])
        for el in range(GATHER_EXPERTS):
            acc_ref[el, pl.ds(windows[tb, el][0], GATHER_SLOTS), :] += rows[el * GATHER_SLOTS:(el + 1) * GATHER_SLOTS]
    for (tb, el), (start, n_win) in windows.items():
        def extra(k, carry, tb=tb, el=el, start=start):
            clamped, first = _next_window(start, k, cap)
            toks = slice(tb * GATHER_TOKENS, (tb + 1) * GATHER_TOKENS)
            acc_ref[el, pl.ds(clamped, GATHER_SLOTS), :] += _dot(onehot(el, tb, clamped, first), h_ref[0, toks, :])
            return carry
        lax.fori_loop(1, n_win, extra, 0)
    xs_ref[...] = acc_ref[...].astype(BF16)


def _gather_big(h2, pos, cnt_flat, cap):
    b, l, d = h2.shape
    grid_spec = pltpu.PrefetchScalarGridSpec(
        num_scalar_prefetch=1,
        grid=(b, N_EXPERTS // GATHER_EXPERTS),
        in_specs=[pl.BlockSpec((1, l, d), lambda bi, eg, c: (bi, 0, 0)),
                  pl.BlockSpec((N_EXPERTS, l), lambda bi, eg, c: (bi, 0))],
        out_specs=pl.BlockSpec((GATHER_EXPERTS, cap, d), lambda bi, eg, c: (eg, bi, 0)),
        scratch_shapes=[pltpu.VMEM((GATHER_EXPERTS, cap, d), F32)],
    )
    return pl.pallas_call(
        functools.partial(_gather_big_kernel, cap=cap),
        grid_spec=grid_spec,
        out_shape=jax.ShapeDtypeStruct((N_EXPERTS, b * cap, d), BF16),
        compiler_params=_cparams(("arbitrary", "arbitrary")),
        name="gather_big",
    )(cnt_flat, h2, pos)


def _ffn_kernel(xa_ref, xb_ref, wg_ref, wu_ref, wd_ref, ya_ref, yb_ref, acca_ref, accb_ref):
    f, nf = pl.program_id(1), pl.num_programs(1)

    def ff_tile(first, last):
        wg = wg_ref[0].astype(BF16)
        wu = wu_ref[0].astype(BF16)
        wd = wd_ref[0].astype(BF16)
        for x_ref, y_ref, acc_ref in ((xa_ref, ya_ref, acca_ref), (xb_ref, yb_ref, accb_ref)):
            m = x_ref.shape[1]
            step = min(FFN_ROWS, m)
            for r0 in range(0, m, step):
                rows = slice(r0, r0 + step)
                x = x_ref[0, rows, :]
                hid = (_silu(_dot(x, wg)) * _dot(x, wu)).astype(BF16)
                part = _dot(hid, wd)
                if first:
                    acc_ref[rows, :] = part
                elif last:
                    y_ref[0, rows, :] = (acc_ref[rows, :] + part).astype(BF16)
                else:
                    acc_ref[rows, :] += part

    pl.when(f == 0)(functools.partial(ff_tile, True, False))
    pl.when((f > 0) & (f < nf - 1))(functools.partial(ff_tile, False, False))
    pl.when(f == nf - 1)(functools.partial(ff_tile, False, True))


def _ffn(xs_a, xs_b, w_gate, w_up, w_down):
    e, ma, d = xs_a.shape
    mb = xs_b.shape[1]
    ff = w_gate.shape[2]
    nf = ff // FF_TILE
    assert ff % FF_TILE == 0 and nf >= 2
    return pl.pallas_call(
        _ffn_kernel,
        grid=(e, nf),
        in_specs=[pl.BlockSpec((1, ma, d), lambda ei, f: (ei, 0, 0)),
                  pl.BlockSpec((1, mb, d), lambda ei, f: (ei, 0, 0)),
                  pl.BlockSpec((1, d, FF_TILE), lambda ei, f: (ei, 0, f)),
                  pl.BlockSpec((1, d, FF_TILE), lambda ei, f: (ei, 0, f)),
                  pl.BlockSpec((1, FF_TILE, d), lambda ei, f: (ei, f, 0))],
        out_specs=[pl.BlockSpec((1, ma, d), lambda ei, f: (ei, 0, 0)),
                   pl.BlockSpec((1, mb, d), lambda ei, f: (ei, 0, 0))],
        out_shape=[jax.ShapeDtypeStruct((e, ma, d), BF16), jax.ShapeDtypeStruct((e, mb, d), BF16)],
        scratch_shapes=[pltpu.VMEM((ma, d), F32), pltpu.VMEM((mb, d), F32)],
        compiler_params=_cparams(("arbitrary", "arbitrary")),
        name="ffn",
    )(xs_a, xs_b, w_gate, w_up, w_down)


def _slot_gate(hit, w_row):
    return jnp.sum(jnp.where(hit, w_row, 0.0), axis=1, keepdims=True)


def _finish(x1, moe, gate2, fn_g):
    x = x1 + gate2 * moe
    return (x * lax.rsqrt(jnp.mean(x * x, axis=-1, keepdims=True) + EPS)) * fn_g


def _combine_small_kernel(x1_ref, y_ref, pos_ref, w_ref, mod_ref, fn_ref, o_ref, *, cap):
    for req in range(x1_ref.shape[0]):
        hits = _stacked_hits(pos_ref, req, cap)
        gated = [(y_ref[e, req * cap:(req + 1) * cap, :].astype(F32)
                  * _slot_gate(hits[e], w_ref[req * N_EXPERTS + e:req * N_EXPERTS + e + 1, :])).astype(BF16)
                 for e in range(N_EXPERTS)]
        onehot = jnp.concatenate([h.astype(BF16) for h in hits], axis=0)
        moe = _dot_tn(onehot, jnp.concatenate(gated, axis=0))
        o_ref[req] = _finish(x1_ref[req], moe, mod_ref[req, 5:6], fn_ref[...])


def _combine_small(x1, y, pos, wsel, mod, fn_g, cap):
    b, l, d = x1.shape
    nr = SMALL_REQS if b % SMALL_REQS == 0 else 1
    return pl.pallas_call(
        functools.partial(_combine_small_kernel, cap=cap),
        grid=(b // nr,),
        in_specs=[pl.BlockSpec((nr, l, d), lambda bi: (bi, 0, 0)),
                  pl.BlockSpec((N_EXPERTS, nr * cap, d), lambda bi: (0, bi, 0)),
                  pl.BlockSpec((nr * N_EXPERTS, l), lambda bi: (bi, 0)),
                  pl.BlockSpec((nr * N_EXPERTS, l), lambda bi: (bi, 0)),
                  pl.BlockSpec((nr, 8, d), lambda bi: (bi, 0, 0)),
                  pl.BlockSpec((1, d), lambda bi: (0, 0))],
        out_specs=pl.BlockSpec((nr, l, d), lambda bi: (bi, 0, 0)),
        out_shape=jax.ShapeDtypeStruct((b, l, d), F32),
        compiler_params=_cparams(("arbitrary",)),
        name="combine_small",
    )(x1, y, pos, wsel, mod, fn_g)


def _combine_big_kernel(cnt_ref, x1_ref, y_ref, pos_ref, w_ref, mod_ref, fn_ref, o_ref, acc_ref, *, cap):
    bi, step = pl.program_id(0), pl.program_id(1)

    def window_terms(e, toks, start, first_slot=None):
        hit = _window_hits(pos_ref[e:e + 1, toks], start, first_slot)
        y = y_ref[e, pl.ds(start, GATHER_SLOTS), :].astype(F32)
        return hit.astype(BF16), (y * _slot_gate(hit, w_ref[e:e + 1, toks])).astype(BF16)

    blocks = []
    for sub in range(COMBINE_BLOCKS):
        toks = slice(sub * GATHER_TOKENS, (sub + 1) * GATHER_TOKENS)
        tb = step * COMBINE_BLOCKS + sub
        windows = [_slot_windows(cnt_ref, bi * N_EXPERTS + e, tb, cap) for e in range(N_EXPERTS)]
        terms = [window_terms(e, toks, start) for e, (start, _) in enumerate(windows)]
        acc_ref[toks, :] = _dot_tn(jnp.concatenate([t[0] for t in terms], axis=0),
                                   jnp.concatenate([t[1] for t in terms], axis=0))
        blocks.append((toks, windows))
    for toks, windows in blocks:
        for e, (start, n_win) in enumerate(windows):
            def extra(k, carry, e=e, toks=toks, start=start):
                hit, gated = window_terms(e, toks, *_next_window(start, k, cap))
                acc_ref[toks, :] += _dot_tn(hit, gated)
                return carry
            lax.fori_loop(1, n_win, extra, 0)
    o_ref[0] = _finish(x1_ref[0], acc_ref[...], mod_ref[0, 5:6], fn_ref[...])


def _combine_big(x1, y, pos, wsel, cnt_flat, mod, fn_g, cap):
    b, l, d = x1.shape
    span = COMBINE_BLOCKS * GATHER_TOKENS
    assert l % span == 0
    grid_spec = pltpu.PrefetchScalarGridSpec(
        num_scalar_prefetch=1,
        grid=(b, l // span),
        in_specs=[pl.BlockSpec((1, span, d), lambda bi, tb, c: (bi, tb, 0)),
                  pl.BlockSpec((N_EXPERTS, cap, d), lambda bi, tb, c: (0, bi, 0)),
                  pl.BlockSpec((N_EXPERTS, span), lambda bi, tb, c: (bi, tb)),
                  pl.BlockSpec((N_EXPERTS, span), lambda bi, tb, c: (bi, tb)),
                  pl.BlockSpec((1, 8, d), lambda bi, tb, c: (bi, 0, 0)),
                  pl.BlockSpec((1, d), lambda bi, tb, c: (0, 0))],
        out_specs=pl.BlockSpec((1, span, d), lambda bi, tb, c: (bi, tb, 0)),
        scratch_shapes=[pltpu.VMEM((span, d), F32)],
    )
    return pl.pallas_call(
        functools.partial(_combine_big_kernel, cap=cap),
        grid_spec=grid_spec,
        out_shape=jax.ShapeDtypeStruct((b, l, d), F32),
        compiler_params=_cparams(("arbitrary", "arbitrary")),
        name="combine_big",
    )(cnt_flat, x1, y, pos, wsel, mod, fn_g)


def _router_parts(w):
    hi, lo = _hi_lo(w)
    pad = ((0, 0), (0, LANES - w.shape[1]))
    return jnp.concatenate([jnp.pad(hi, pad), jnp.pad(lo, pad)], axis=1)


def _rope_tables(l):
    rows = l // GRID_W
    row = jnp.repeat(jnp.arange(rows, dtype=F32), GRID_W)
    col = jnp.tile(jnp.arange(GRID_W, dtype=F32), rows)
    nf = DK // 4
    inv = ROPE_BASE ** (-jnp.arange(nf, dtype=F32) / nf)
    ang = jnp.concatenate([row[:, None] * inv, col[:, None] * inv], axis=-1)
    cos = jnp.repeat(jnp.cos(ang), 2, axis=-1)
    sin = jnp.repeat(jnp.sin(ang), 2, axis=-1) * jnp.tile(jnp.asarray([-1.0, 1.0], F32), DK // 2)
    return cos, sin


def kernel(x_prompt, x_sample, c, state_ret, c_ctx, w_ada, b_ada, norm1_g, w_in, w_pool, pool_scale, w_pool_out,
           ret_decay, ret_gn_g, w_ret_out, w_branch_gate, b_branch_gate, w_out, norm2_g, w_router, w_exp_gate,
           w_exp_up, w_exp_down, final_norm_g):
    depth = w_ada.shape[0]
    assert depth == 1, "single trunk layer"
    bc, lc, d = x_prompt.shape
    bl, ll, _ = x_sample.shape
    assert lc % CHUNK == 0 and STEP_ROWS % lc == 0 and ll % STEP_ROWS == 0 and ll % GATHER_TOKENS == 0
    cap_c = max(1, CAPACITY_FACTOR * lc // N_EXPERTS)
    cap_l = max(1, CAPACITY_FACTOR * ll // N_EXPERTS)
    assert cap_l % GATHER_SLOTS == 0 and ll // GATHER_TOKENS < COUNT_LANES

    cond = jnp.zeros((8 * pl.cdiv(bl + 1, 8), d), F32).at[:bl].set(c).at[bl].set(c_ctx)
    ada = _adaln(cond, w_ada[0], b_ada[0]).reshape(-1, 6, d)
    ada = jnp.pad(ada, ((0, 0), (0, 2), (0, 0)))
    mod_l = ada[:bl]
    mod_c = jnp.broadcast_to(ada[bl:bl + 1], (bc, 8, d))

    p = dict(
        n1g=norm1_g[0][None], w_in=w_in[0].astype(BF16), w_bg=w_branch_gate[0].astype(BF16),
        b_bg=b_branch_gate[0][None], w_pool=w_pool[0].astype(BF16), pool_scale=pool_scale[0][None],
        w_pool_out=w_pool_out[0].astype(BF16), gn_g=ret_gn_g[0][None], w_ret_out=w_ret_out[0].astype(BF16),
        w_out=w_out[0].astype(BF16), n2g=norm2_g[0][None], w_router_parts=_router_parts(w_router[0]),
    )
    decay = ret_decay[0]
    fn_g = final_norm_g[None]

    x1_c, h2_c, aff_c, st_c = _mix(x_prompt, mod_c, None, None, decay, p)
    rope = _rope_tables(ll)
    sb, kv = _revscan(x_sample, mod_l, rope[0], rope[1], state_ret, decay, p["n1g"], p["w_in"])
    x1_l, h2_l, aff_l = _mix(x_sample, mod_l, rope, (state_ret, sb, kv), decay, p)

    wsel_c, pos_c, _ = _route(aff_c.reshape(bc * N_EXPERTS, lc), cap_c, min(ROUTE_ROWS, bc * N_EXPERTS))
    wsel_l, pos_l, cnt_l = _route(aff_l.reshape(bl * N_EXPERTS, ll), cap_l, min(ROUTE_ROWS, bl * N_EXPERTS))
    cnt_flat = cnt_l.reshape(-1)

    xs_c = _gather_small(h2_c, pos_c, cap_c)
    xs_l = _gather_big(h2_l, pos_l, cnt_flat, cap_l)
    y_c, y_l = _ffn(xs_c, xs_l, w_exp_gate[0], w_exp_up[0], w_exp_down[0])

    y_prompt = _combine_small(x1_c, y_c, pos_c, wsel_c, mod_c, fn_g, cap_c)
    y_sample = _combine_big(x1_l, y_l, pos_l, wsel_l, cnt_flat, mod_l, fn_g, cap_l)
    new_state = st_c[:, None].astype(x_prompt.dtype)
    return (y_prompt, y_sample, new_state)
```

```python
import functools

import jax
import jax.numpy as jnp
from jax import lax
from jax.experimental import pallas as pl
from jax.experimental.pallas import tpu as pltpu

F32 = jnp.float32
BF16 = jnp.bfloat16

N_HEADS = 4
DK = 128
DV = 256
POOL_WINDOWS = (2, 4, 8, 16)
POOL_GROUP = 128
POOL_WIDTH = POOL_GROUP * len(POOL_WINDOWS)
QK_WIDTH = N_HEADS * DK
V_WIDTH = N_HEADS * DV
N_EXPERTS = 16
CAPACITY_FACTOR = 2
GRID_W = 64
ROPE_BASE = 10000.0
EPS = 1e-6

LANES = 128
CHUNK = 256
STEP_ROWS = 512
HALO = 16
ROUTE_LANES = 256
ROUTE_ROWS = 512
SMALL_REQS = 4
GATHER_TOKENS = 256
GATHER_SLOTS = 64
GATHER_EXPERTS = 4
COMBINE_BLOCKS = 2
COUNT_LANES = 128
SLOT_ALIGN = 16
FF_TILE = 256
FFN_ROWS = 512
VMEM_LIMIT = 56 * 1024 * 1024


def _cparams(sem):
    return pltpu.CompilerParams(dimension_semantics=sem, vmem_limit_bytes=VMEM_LIMIT)


def _resident(shape):
    return pl.BlockSpec(shape, lambda *_: (0,) * len(shape), pipeline_mode=pl.Buffered(1))


def _sigmoid(x):
    return 0.5 * jnp.tanh(0.5 * x) + 0.5


def _silu(x):
    return x * _sigmoid(x)


def _norm_mod(x, g, scale, shift):
    y = x * lax.rsqrt(jnp.mean(x * x, axis=-1, keepdims=True) + EPS)
    return (y * g) * (1.0 + scale) + shift


def _hi_lo(x):
    hi = x.astype(BF16)
    return hi, (x - hi.astype(F32)).astype(BF16)


def _dot(a, b):
    return jnp.dot(a, b, preferred_element_type=F32)


def _dot_nt(a, b):
    return lax.dot_general(a, b, (((1,), (1,)), ((), ())), preferred_element_type=F32)


def _dot_tn(a, b):
    return lax.dot_general(a, b, (((0,), (0,)), ((), ())), preferred_element_type=F32)


def _adaln_kernel(c_ref, w_ref, b_ref, o_ref):
    rows = c_ref.shape[0]
    s_hi, s_lo = _hi_lo(_silu(c_ref[...]))
    w_hi, w_lo = _hi_lo(w_ref[...])
    by_hi = _dot(jnp.concatenate([s_hi, s_lo], axis=0), w_hi)
    o_ref[...] = by_hi[:rows] + by_hi[rows:] + _dot(s_hi, w_lo) + b_ref[...]


def _adaln(cond, w, b):
    rows, d = cond.shape
    n = w.shape[1]
    tn = d
    return pl.pallas_call(
        _adaln_kernel,
        grid=(n // tn,),
        in_specs=[pl.BlockSpec((rows, d), lambda i: (0, 0)),
                  pl.BlockSpec((d, tn), lambda i: (0, i)),
                  pl.BlockSpec((1, tn), lambda i: (0, i))],
        out_specs=pl.BlockSpec((rows, tn), lambda i: (0, i)),
        out_shape=jax.ShapeDtypeStruct((rows, n), F32),
        compiler_params=_cparams(("arbitrary",)),
        name="adaln",
    )(cond, w, b.reshape(1, n))


def _decay_tiles(decay_ref, head):
    def log_gamma(direction, shape):
        return -jnp.exp(jnp.full(shape, decay_ref[direction, head], F32))

    diff = (lax.broadcasted_iota(jnp.int32, (CHUNK, CHUNK), 0)
            - lax.broadcasted_iota(jnp.int32, (CHUNK, CHUNK), 1)).astype(F32)
    i = lax.broadcasted_iota(jnp.int32, (CHUNK, DK), 0).astype(F32)
    lgf, lgb = log_gamma(0, (CHUNK, DK)), log_gamma(1, (CHUNK, DK))
    return dict(
        mask=jnp.where(diff >= 0.0, jnp.exp(log_gamma(0, (CHUNK, CHUNK)) * jnp.maximum(diff, 0.0)),
                       jnp.exp(log_gamma(1, (CHUNK, CHUNK)) * jnp.maximum(-diff, 0.0))),
        qdf=jnp.exp(lgf * (i + 1.0)),
        qdb=jnp.exp(lgb * (CHUNK - i)),
        kdf=jnp.exp(lgf * (CHUNK - 1.0 - i)),
        kdb=jnp.exp(lgb * i),
        cdf=jnp.exp(log_gamma(0, (DK, DV)) * float(CHUNK)),
        cdb=jnp.exp(log_gamma(1, (DK, DV)) * float(CHUNK)),
    )


def _rope(x, cos, sin):
    even = (lax.broadcasted_iota(jnp.int32, x.shape, 1) % 2) == 0
    partner = jnp.where(even, pltpu.roll(x, x.shape[1] - 1, 1), pltpu.roll(x, 1, 1))
    return x * cos + partner * sin


def _state_spec(direction):
    return pl.BlockSpec((1, 1, 1, N_HEADS, DK, DV), lambda bi, j: (bi, 0, direction, 0, 0, 0))


def _revscan_kernel(x_ref, mod_ref, cos_ref, sin_ref, s0_ref, decay_ref, n1_ref, w_in_ref, sb_ref, kv_ref, s_scr):
    j = pl.program_id(1)

    @pl.when(j == 0)
    def _():
        s_scr[...] = s0_ref[0, 0, 0]

    mod = mod_ref[0]
    h = _norm_mod(x_ref[0], n1_ref[...], mod[1:2], mod[0:1]).astype(BF16)
    k_lo = POOL_WIDTH + QK_WIDTH
    kv = _dot(h, w_in_ref[:, k_lo:k_lo + QK_WIDTH + V_WIDTH])
    sb_ref[0, 0] = s_scr[...]
    kv_ref[0, :, QK_WIDTH:] = kv[:, QK_WIDTH:].astype(BF16)
    for head in range(N_HEADS):
        t = _decay_tiles(decay_ref, head)
        s = s_scr[head]
        for c in reversed(range(STEP_ROWS // CHUNK)):
            rows = slice(c * CHUNK, (c + 1) * CHUNK)
            cols = slice(head * DK, (head + 1) * DK)
            k = _rope(kv[rows, cols] * (DK ** -0.5), cos_ref[rows], sin_ref[rows])
            kv_ref[0, rows, cols] = k.astype(BF16)
            v = kv[rows, QK_WIDTH + head * DV:QK_WIDTH + (head + 1) * DV]
            s = t["cdb"] * s + _dot_tn((k * t["kdb"]).astype(BF16), v.astype(BF16))
        s_scr[head] = s


def _revscan(x, mod, cos, sin, state_ret, decay, n1g, w_in):
    b, l, d = x.shape
    nblk = l // STEP_ROWS
    return pl.pallas_call(
        _revscan_kernel,
        grid=(b, nblk),
        in_specs=[pl.BlockSpec((1, STEP_ROWS, d), lambda bi, j: (bi, nblk - 1 - j, 0)),
                  pl.BlockSpec((1, 8, d), lambda bi, j: (bi, 0, 0)),
                  pl.BlockSpec((STEP_ROWS, DK), lambda bi, j: (nblk - 1 - j, 0)),
                  pl.BlockSpec((STEP_ROWS, DK), lambda bi, j: (nblk - 1 - j, 0)),
                  _state_spec(1),
                  pl.BlockSpec(memory_space=pltpu.SMEM),
                  _resident(n1g.shape),
                  _resident(w_in.shape)],
        out_specs=[pl.BlockSpec((1, 1, N_HEADS, DK, DV), lambda bi, j: (bi, nblk - 1 - j, 0, 0, 0)),
                   pl.BlockSpec((1, STEP_ROWS, QK_WIDTH + V_WIDTH), lambda bi, j: (bi, nblk - 1 - j, 0))],
        out_shape=[jax.ShapeDtypeStruct((b, nblk, N_HEADS, DK, DV), F32),
                   jax.ShapeDtypeStruct((b, l, QK_WIDTH + V_WIDTH), BF16)],
        scratch_shapes=[pltpu.VMEM((N_HEADS, DK, DV), F32)],
        compiler_params=_cparams(("arbitrary", "arbitrary")),
        name="revscan",
    )(x, mod, cos, sin, state_ret, decay, n1g, w_in)


def _mix_kernel(*refs, seq_len, n_seq, seq_rows, use_rope, has_state, emit_state):
    it = iter(refs)
    x_ref, xp_ref, xn_ref, mod_ref = next(it), next(it), next(it), next(it)
    cos_ref = sin_ref = s0f_ref = sb_ref = kv_ref = None
    if use_rope:
        cos_ref, sin_ref = next(it), next(it)
    if has_state:
        s0f_ref, sb_ref, kv_ref = next(it), next(it), next(it)
    (decay_ref, n1_ref, w_in_ref, w_bg_ref, b_bg_ref, w_pool_ref, pscale_ref, w_po_ref, gn_ref, w_ro_ref,
     w_out_ref, n2_ref, w_rt_ref) = (next(it) for _ in range(13))
    x1_ref, h2_ref, aff_ref = next(it), next(it), next(it)
    st_ref = next(it) if emit_state else None
    sf_scr = next(it)

    j = pl.program_id(1)
    d = x_ref.shape[2]
    n_chunks = seq_rows // CHUNK
    ext = seq_rows + 2 * HALO

    @pl.when(j == 0)
    def _():
        if has_state:
            sf_scr[...] = s0f_ref[:, 0, 0]
        else:
            sf_scr[...] = jnp.zeros(sf_scr.shape, F32)

    mods = [mod_ref[s] for s in range(n_seq)]

    def rows_of(vals):
        return jnp.concatenate([jnp.broadcast_to(v, (seq_rows, d)) for v in vals], axis=0) if n_seq > 1 else vals[0]

    he_parts = []
    for s in range(n_seq):
        xe = jnp.concatenate([xp_ref[s], x_ref[s], xn_ref[s]], axis=0)
        he_parts.append(_norm_mod(xe, n1_ref[...], mods[s][1:2], mods[s][0:1]).astype(BF16))
    he = jnp.concatenate(he_parts, axis=0) if n_seq > 1 else he_parts[0]
    hb_parts = [hp[HALO:HALO + seq_rows] for hp in he_parts]
    hb = jnp.concatenate(hb_parts, axis=0) if n_seq > 1 else hb_parts[0]
    x = jnp.concatenate([x_ref[s] for s in range(n_seq)], axis=0) if n_seq > 1 else x_ref[0]
    gates = _sigmoid(_dot(hb, w_bg_ref[...]) + b_bg_ref[...])

    ue_all = _dot(he, w_in_ref[:, :POOL_WIDTH])
    epos = j * seq_rows - HALO + lax.broadcasted_iota(jnp.int32, (ext, 1), 0)
    valid = (epos >= 0) & (epos < seq_len)
    tpos = j * seq_rows + lax.broadcasted_iota(jnp.int32, (seq_rows, 1), 0)
    pooled = [[] for _ in POOL_WINDOWS]
    for s in range(n_seq):
        ue = jnp.where(valid, ue_all[s * ext:(s + 1) * ext], 0.0)
        for gi, w in enumerate(POOL_WINDOWS):
            ug = ue[:, gi * POOL_GROUP:(gi + 1) * POOL_GROUP]
            acc, shift = ug, 1
            while shift < w:
                acc = acc + pltpu.roll(acc, shift, 0)
                shift *= 2
            if w // 2 > 1:
                acc = pltpu.roll(acc, ext - (w // 2 - 1), 0)
            cnt = (jnp.minimum(tpos + w // 2, seq_len) - jnp.maximum(tpos - w // 2, 0)).astype(F32)
            own = slice(HALO, HALO + seq_rows)
            pooled[gi].append((acc[own] / cnt - ug[own]).astype(BF16))
    pool_h = jnp.concatenate(
        [_dot(jnp.concatenate(pg, axis=0) if len(pg) > 1 else pg[0], w_pool_ref[gi]) for gi, pg in enumerate(pooled)],
        axis=1) * pscale_ref[...]
    pool_y = _dot(pool_h.astype(BF16), w_po_ref[...])

    g_lo = POOL_WIDTH + 2 * QK_WIDTH + V_WIDTH
    if has_state:
        zq = _dot(hb, w_in_ref[:, POOL_WIDTH:POOL_WIDTH + QK_WIDTH])
        g_all = _dot(hb, w_in_ref[:, g_lo:])
    else:
        zr = _dot(hb, w_in_ref[:, POOL_WIDTH:])
        zq, g_all = zr[:, :QK_WIDTH], zr[:, 2 * QK_WIDTH + V_WIDTH:]
    seq_out = []
    for s in range(n_seq):
        r0 = s * seq_rows
        head_out = []
        for head in range(N_HEADS):
            t = _decay_tiles(decay_ref, head)
            qs, ks, vs = [], [], []
            for c in range(n_chunks):
                rows = slice(r0 + c * CHUNK, r0 + (c + 1) * CHUNK)
                crow = slice(c * CHUNK, (c + 1) * CHUNK)
                q = zq[rows, head * DK:(head + 1) * DK]
                if use_rope:
                    q = _rope(q, cos_ref[crow], sin_ref[crow])
                if has_state:
                    k = kv_ref[s, crow, head * DK:(head + 1) * DK].astype(F32)
                    v = kv_ref[s, crow, QK_WIDTH + head * DV:QK_WIDTH + (head + 1) * DV]
                else:
                    k = zr[rows, QK_WIDTH + head * DK:QK_WIDTH + (head + 1) * DK] * (DK ** -0.5)
                    if use_rope:
                        k = _rope(k, cos_ref[crow], sin_ref[crow])
                    v = zr[rows, 2 * QK_WIDTH + head * DV:2 * QK_WIDTH + (head + 1) * DV].astype(BF16)
                qs.append(q)
                ks.append(k)
                vs.append(v)
            sf = [sf_scr[s, head]]
            for c in range(n_chunks):
                sf.append(t["cdf"] * sf[c] + _dot_tn((ks[c] * t["kdf"]).astype(BF16), vs[c]))
            sf_scr[s, head] = sf[n_chunks]
            sb = [None] * (n_chunks + 1)
            sb[n_chunks] = sb_ref[s, 0, head] if has_state else jnp.zeros((DK, DV), F32)
            for c in reversed(range(n_chunks)):
                sb[c] = t["cdb"] * sb[c + 1] + _dot_tn((ks[c] * t["kdb"]).astype(BF16), vs[c])
            if emit_state:
                st_ref[s, 0, head] = sf[n_chunks]
                st_ref[s, 1, head] = sb[0]
            outs = []
            for c in range(n_chunks):
                scores = _dot_nt(qs[c].astype(BF16), ks[c].astype(BF16)) * t["mask"]
                q_both = jnp.concatenate([(qs[c] * t["qdf"]).astype(BF16), (qs[c] * t["qdb"]).astype(BF16)], axis=1)
                s_both = jnp.concatenate([sf[c].astype(BF16), sb[c + 1].astype(BF16)], axis=0)
                outs.append(_dot(scores.astype(BF16), vs[c]) + _dot(q_both, s_both))
            o = jnp.concatenate(outs, axis=0)
            mu = jnp.mean(o, axis=-1, keepdims=True)
            oc = o - mu
            var = jnp.mean(oc * oc, axis=-1, keepdims=True)
            head_out.append(oc * lax.rsqrt(var + EPS))
        seq_out.append(jnp.concatenate(head_out, axis=1))
    o_n = (jnp.concatenate(seq_out, axis=0) if n_seq > 1 else seq_out[0]) * gn_ref[...]
    ret_y = _dot((_silu(g_all) * o_n).astype(BF16), w_ro_ref[...])

    merged = gates[:, :d] * pool_y + gates[:, d:] * ret_y
    x1 = x + rows_of([m[2:3] for m in mods]) * _dot(merged.astype(BF16), w_out_ref[...])

    h2 = _norm_mod(x1, n2_ref[...], rows_of([m[4:5] for m in mods]), rows_of([m[3:4] for m in mods]))
    h2_hi, h2_lo = _hi_lo(h2)
    n_rows = n_seq * seq_rows
    parts = _dot(jnp.concatenate([h2_hi, h2_lo], axis=0), w_rt_ref[...])
    logits_t = (parts[:n_rows, :LANES] + parts[n_rows:, :LANES] + parts[:n_rows, LANES:])
    for s in range(n_seq):
        rows = slice(s * seq_rows, (s + 1) * seq_rows)
        x1_ref[s] = x1[rows]
        h2_ref[s] = h2_hi[rows]
        logits = logits_t[rows].T[:N_EXPERTS]
        e = jnp.exp(logits - jnp.max(logits, axis=0, keepdims=True))
        aff_ref[s] = e / jnp.sum(e, axis=0, keepdims=True)


def _mix(x, mod, rope, states, decay, p):
    b, l, d = x.shape
    seq_rows = min(l, STEP_ROWS)
    n_seq = STEP_ROWS // seq_rows
    nblk = l // seq_rows
    hb = seq_rows // HALO
    n_halo = l // HALO
    use_rope, has_state = rope is not None, states is not None
    emit_state = not has_state
    assert b % n_seq == 0 and (n_seq == 1 or not (use_rope or has_state))
    in_specs = [pl.BlockSpec((n_seq, seq_rows, d), lambda bi, j: (bi, j, 0)),
                pl.BlockSpec((n_seq, HALO, d), lambda bi, j: (bi, jnp.maximum(j * hb - 1, 0), 0)),
                pl.BlockSpec((n_seq, HALO, d), lambda bi, j: (bi, jnp.minimum((j + 1) * hb, n_halo - 1), 0)),
                pl.BlockSpec((n_seq, 8, d), lambda bi, j: (bi, 0, 0))]
    args = [x, x, x, mod]
    if use_rope:
        in_specs += [pl.BlockSpec((seq_rows, DK), lambda bi, j: (j, 0))] * 2
        args += list(rope)
    if has_state:
        in_specs += [_state_spec(0),
                     pl.BlockSpec((1, 1, N_HEADS, DK, DV), lambda bi, j: (bi, j, 0, 0, 0)),
                     pl.BlockSpec((1, seq_rows, QK_WIDTH + V_WIDTH), lambda bi, j: (bi, j, 0))]
        args += list(states)
    weights = [p["n1g"], p["w_in"], p["w_bg"], p["b_bg"], p["w_pool"], p["pool_scale"], p["w_pool_out"],
               p["gn_g"], p["w_ret_out"], p["w_out"], p["n2g"], p["w_router_parts"]]
    in_specs.append(pl.BlockSpec(memory_space=pltpu.SMEM))
    args.append(decay)
    for w in weights:
        in_specs.append(_resident(w.shape))
        args.append(w)
    out_specs = [pl.BlockSpec((n_seq, seq_rows, d), lambda bi, j: (bi, j, 0)),
                 pl.BlockSpec((n_seq, seq_rows, d), lambda bi, j: (bi, j, 0)),
                 pl.BlockSpec((n_seq, N_EXPERTS, seq_rows), lambda bi, j: (bi, 0, j))]
    out_shape = [jax.ShapeDtypeStruct((b, l, d), F32), jax.ShapeDtypeStruct((b, l, d), BF16),
                 jax.ShapeDtypeStruct((b, N_EXPERTS, l), F32)]
    if emit_state:
        out_specs.append(pl.BlockSpec((n_seq, 2, N_HEADS, DK, DV), lambda bi, j: (bi, 0, 0, 0, 0)))
        out_shape.append(jax.ShapeDtypeStruct((b, 2, N_HEADS, DK, DV), F32))
    kern = functools.partial(_mix_kernel, seq_len=l, n_seq=n_seq, seq_rows=seq_rows, use_rope=use_rope,
                             has_state=has_state, emit_state=emit_state)
    return pl.pallas_call(
        kern,
        grid=(b // n_seq, nblk),
        in_specs=in_specs,
        out_specs=out_specs,
        out_shape=out_shape,
        scratch_shapes=[pltpu.VMEM((n_seq, N_HEADS, DK, DV), F32)],
        compiler_params=_cparams(("arbitrary", "arbitrary")),
        name="mix_rope" if use_rope else "mix",
    )(*args)


def _route_kernel(aff_ref, wsel_ref, pos_ref, cnt_ref, *, cap):
    aff = aff_ref[...]
    rows, l = aff.shape

    def as_value(bits):
        return pltpu.bitcast(bits, F32)

    def bisect(i, tau):
        cand = tau | jnp.left_shift(jnp.int32(1), 30 - i)
        cnt = jnp.sum((aff >= as_value(cand)).astype(F32), axis=1, keepdims=True)
        return jnp.where(cnt >= cap, cand, tau)

    tau = lax.fori_loop(0, 31, bisect, jnp.zeros((rows, 1), jnp.int32))
    gt = aff >= as_value(tau + 1)
    eq = (aff >= as_value(tau)) & jnp.logical_not(gt)
    need = cap - jnp.sum(gt.astype(F32), axis=1, keepdims=True)

    nlb = l // ROUTE_LANES
    tri = (lax.broadcasted_iota(jnp.int32, (ROUTE_LANES, ROUTE_LANES), 0)
           < lax.broadcasted_iota(jnp.int32, (ROUTE_LANES, ROUTE_LANES), 1)).astype(BF16)

    def prefix(mask):
        carry = jnp.zeros((rows, 1), F32)
        parts, starts = [], []
        for blk in range(nlb):
            mb = mask[:, blk * ROUTE_LANES:(blk + 1) * ROUTE_LANES].astype(F32)
            starts.append(carry)
            parts.append(_dot(mb.astype(BF16), tri) + carry)
            carry = carry + jnp.sum(mb, axis=1, keepdims=True)
        starts.append(carry)
        return jnp.concatenate(parts, axis=1), starts

    eq_rank, _ = prefix(eq)
    sel = gt | (eq & (eq_rank < need))
    pos, starts = prefix(sel)
    wsel_ref[...] = jnp.where(sel, aff, 0.0)
    pos_ref[...] = jnp.where(sel, pos.astype(jnp.int32), -1)
    lane = lax.broadcasted_iota(jnp.int32, (rows, COUNT_LANES), 1)
    table = jnp.zeros((rows, COUNT_LANES), jnp.int32)
    per = max(GATHER_TOKENS // ROUTE_LANES, 1)
    for tb in range(nlb // per + 1):
        table = jnp.where(lane == tb, starts[min(tb * per, nlb)].astype(jnp.int32), table)
    cnt_ref[...] = table


def _route(aff_rows, cap, row_block):
    r, l = aff_rows.shape
    spec = pl.BlockSpec((row_block, l), lambda i: (i, 0))
    return pl.pallas_call(
        functools.partial(_route_kernel, cap=cap),
        grid=(r // row_block,),
        in_specs=[spec],
        out_specs=[spec, spec, pl.BlockSpec((row_block, COUNT_LANES), lambda i: (i, 0))],
        out_shape=[jax.ShapeDtypeStruct((r, l), F32), jax.ShapeDtypeStruct((r, l), jnp.int32),
                   jax.ShapeDtypeStruct((r, COUNT_LANES), jnp.int32)],
        compiler_params=_cparams(("arbitrary",)),
        name=f"route_{l}",
    )(aff_rows)


def _stacked_hits(pos_ref, req, cap):
    l = pos_ref.shape[1]
    slot = lax.broadcasted_iota(jnp.int32, (cap, l), 0)
    return [pos_ref[req * N_EXPERTS + e:req * N_EXPERTS + e + 1, :] == slot for e in range(N_EXPERTS)]


def _gather_small_kernel(h_ref, pos_ref, xs_ref, *, cap):
    for req in range(h_ref.shape[0]):
        onehot = jnp.concatenate([h.astype(BF16) for h in _stacked_hits(pos_ref, req, cap)], axis=0)
        xs = _dot(onehot, h_ref[req]).astype(BF16)
        for e in range(N_EXPERTS):
            xs_ref[e, req * cap:(req + 1) * cap, :] = xs[e * cap:(e + 1) * cap]


def _gather_small(h2, pos, cap):
    b, l, d = h2.shape
    nr = SMALL_REQS if b % SMALL_REQS == 0 else 1
    return pl.pallas_call(
        functools.partial(_gather_small_kernel, cap=cap),
        grid=(b // nr,),
        in_specs=[pl.BlockSpec((nr, l, d), lambda bi: (bi, 0, 0)),
                  pl.BlockSpec((nr * N_EXPERTS, l), lambda bi: (bi, 0))],
        out_specs=pl.BlockSpec((N_EXPERTS, nr * cap, d), lambda bi: (0, bi, 0)),
        out_shape=jax.ShapeDtypeStruct((N_EXPERTS, b * cap, d), BF16),
        compiler_params=_cparams(("arbitrary",)),
        name="gather_small",
    )(h2, pos)


def _slot_windows(cnt_ref, row, tb, cap):
    lo = cnt_ref[row * COUNT_LANES + tb]
    hi = cnt_ref[row * COUNT_LANES + tb + 1]
    start = jnp.minimum(lo & jnp.int32(-SLOT_ALIGN), jnp.int32(cap - GATHER_SLOTS))
    n_win = lax.shift_right_logical(hi - start + jnp.int32(GATHER_SLOTS - 1),
                                    jnp.int32(GATHER_SLOTS.bit_length() - 1))
    return pl.multiple_of(start, SLOT_ALIGN), n_win


def _next_window(start, k, cap):
    first = start + k * GATHER_SLOTS
    return pl.multiple_of(jnp.minimum(first, cap - GATHER_SLOTS), SLOT_ALIGN), first


def _window_hits(pos_row, start, first_slot=None):
    slot = start + lax.broadcasted_iota(jnp.int32, (GATHER_SLOTS, pos_row.shape[1]), 0)
    hit = pos_row == slot
    return hit if first_slot is None else hit & (slot >= first_slot)


def _gather_big_kernel(cnt_ref, h_ref, pos_ref, xs_ref, *, cap):
    bi, eg = pl.program_id(0), pl.program_id(1)
    n_tb = h_ref.shape[1] // GATHER_TOKENS
    xs_ref[...] = jnp.zeros(xs_ref.shape, BF16)

    def onehot(el, tb, start, first_slot=None):
        toks = slice(tb * GATHER_TOKENS, (tb + 1) * GATHER_TOKENS)
        return _window_hits(pos_ref[pl.ds(eg * GATHER_EXPERTS + el, 1), toks], start, first_slot).astype(BF16)

    windows = {(tb, el): _slot_windows(cnt_ref, bi * N_EXPERTS + eg * GATHER_EXPERTS + el, tb, cap)
               for tb in range(n_tb) for el in range(GATHER_EXPERTS)}
    for tb in range(n_tb):
        toks = slice(tb * GATHER_TOKENS, (tb + 1) * GATHER_TOKENS)
        stacked = jnp.concatenate([onehot(el, tb, windows[tb, el][0]) for el in range(GATHER_EXPERTS)], axis=0)
        rows = _dot(stacked, h_ref[0, toks, :]).astype(BF16)
        for el in range(GATHER_EXPERTS):
            xs_ref[el, pl.ds(windows[tb, el][0], GATHER_SLOTS), :] += rows[el * GATHER_SLOTS:(el + 1) * GATHER_SLOTS]
    for (tb, el), (start, n_win) in windows.items():
        def extra(k, carry, tb=tb, el=el, start=start):
            clamped, first = _next_window(start, k, cap)
            toks = slice(tb * GATHER_TOKENS, (tb + 1) * GATHER_TOKENS)
            xs_ref[el, pl.ds(clamped, GATHER_SLOTS), :] += _dot(onehot(el, tb, clamped, first),
                                                                h_ref[0, toks, :]).astype(BF16)
            return carry
        lax.fori_loop(1, n_win, extra, 0)


def _gather_big(h2, pos, cnt_flat, cap):
    b, l, d = h2.shape
    grid_spec = pltpu.PrefetchScalarGridSpec(
        num_scalar_prefetch=1,
        grid=(b, N_EXPERTS // GATHER_EXPERTS),
        in_specs=[pl.BlockSpec((1, l, d), lambda bi, eg, c: (bi, 0, 0)),
                  pl.BlockSpec((N_EXPERTS, l), lambda bi, eg, c: (bi, 0))],
        out_specs=pl.BlockSpec((GATHER_EXPERTS, cap, d), lambda bi, eg, c: (eg, bi, 0)),
    )
    return pl.pallas_call(
        functools.partial(_gather_big_kernel, cap=cap),
        grid_spec=grid_spec,
        out_shape=jax.ShapeDtypeStruct((N_EXPERTS, b * cap, d), BF16),
        compiler_params=_cparams(("arbitrary", "arbitrary")),
        name="gather_big",
    )(cnt_flat, h2, pos)


def _ffn_kernel(xa_ref, xb_ref, wg_ref, wu_ref, wd_ref, ya_ref, yb_ref, acca_ref, accb_ref):
    f, nf = pl.program_id(1), pl.num_programs(1)

    def ff_tile(first, last):
        wg = wg_ref[0].astype(BF16)
        wu = wu_ref[0].astype(BF16)
        wd = wd_ref[0].astype(BF16)
        for x_ref, y_ref, acc_ref in ((xa_ref, ya_ref, acca_ref), (xb_ref, yb_ref, accb_ref)):
            m = x_ref.shape[1]
            step = min(FFN_ROWS, m)
            for r0 in range(0, m, step):
                rows = slice(r0, r0 + step)
                x = x_ref[0, rows, :]
                hid = (_silu(_dot(x, wg)) * _dot(x, wu)).astype(BF16)
                part = _dot(hid, wd)
                if first:
                    acc_ref[rows, :] = part
                elif last:
                    y_ref[0, rows, :] = (acc_ref[rows, :] + part).astype(BF16)
                else:
                    acc_ref[rows, :] += part

    pl.when(f == 0)(functools.partial(ff_tile, True, False))
    pl.when((f > 0) & (f < nf - 1))(functools.partial(ff_tile, False, False))
    pl.when(f == nf - 1)(functools.partial(ff_tile, False, True))


def _ffn(xs_a, xs_b, w_gate, w_up, w_down):
    e, ma, d = xs_a.shape
    mb = xs_b.shape[1]
    ff = w_gate.shape[2]
    nf = ff // FF_TILE
    assert ff % FF_TILE == 0 and nf >= 2
    return pl.pallas_call(
        _ffn_kernel,
        grid=(e, nf),
        in_specs=[pl.BlockSpec((1, ma, d), lambda ei, f: (ei, 0, 0)),
                  pl.BlockSpec((1, mb, d), lambda ei, f: (ei, 0, 0)),
                  pl.BlockSpec((1, d, FF_TILE), lambda ei, f: (ei, 0, f)),
                  pl.BlockSpec((1, d, FF_TILE), lambda ei, f: (ei, 0, f)),
                  pl.BlockSpec((1, FF_TILE, d), lambda ei, f: (ei, f, 0))],
        out_specs=[pl.BlockSpec((1, ma, d), lambda ei, f: (ei, 0, 0)),
                   pl.BlockSpec((1, mb, d), lambda ei, f: (ei, 0, 0))],
        out_shape=[jax.ShapeDtypeStruct((e, ma, d), BF16), jax.ShapeDtypeStruct((e, mb, d), BF16)],
        scratch_shapes=[pltpu.VMEM((ma, d), F32), pltpu.VMEM((mb, d), F32)],
        compiler_params=_cparams(("arbitrary", "arbitrary")),
        name="ffn",
    )(xs_a, xs_b, w_gate, w_up, w_down)


def _slot_gate(hit, w_row):
    return jnp.sum(jnp.where(hit, w_row, 0.0), axis=1, keepdims=True)


def _finish(x1, moe, gate2, fn_g):
    x = x1 + gate2 * moe
    return (x * lax.rsqrt(jnp.mean(x * x, axis=-1, keepdims=True) + EPS)) * fn_g


def _combine_small_kernel(x1_ref, y_ref, pos_ref, w_ref, mod_ref, fn_ref, o_ref, *, cap):
    for req in range(x1_ref.shape[0]):
        hits = _stacked_hits(pos_ref, req, cap)
        gated = [(y_ref[e, req * cap:(req + 1) * cap, :].astype(F32)
                  * _slot_gate(hits[e], w_ref[req * N_EXPERTS + e:req * N_EXPERTS + e + 1, :])).astype(BF16)
                 for e in range(N_EXPERTS)]
        onehot = jnp.concatenate([h.astype(BF16) for h in hits], axis=0)
        moe = _dot_tn(onehot, jnp.concatenate(gated, axis=0))
        o_ref[req] = _finish(x1_ref[req], moe, mod_ref[req, 5:6], fn_ref[...])


def _combine_small(x1, y, pos, wsel, mod, fn_g, cap):
    b, l, d = x1.shape
    nr = SMALL_REQS if b % SMALL_REQS == 0 else 1
    return pl.pallas_call(
        functools.partial(_combine_small_kernel, cap=cap),
        grid=(b // nr,),
        in_specs=[pl.BlockSpec((nr, l, d), lambda bi: (bi, 0, 0)),
                  pl.BlockSpec((N_EXPERTS, nr * cap, d), lambda bi: (0, bi, 0)),
                  pl.BlockSpec((nr * N_EXPERTS, l), lambda bi: (bi, 0)),
                  pl.BlockSpec((nr * N_EXPERTS, l), lambda bi: (bi, 0)),
                  pl.BlockSpec((nr, 8, d), lambda bi: (bi, 0, 0)),
                  pl.BlockSpec((1, d), lambda bi: (0, 0))],
        out_specs=pl.BlockSpec((nr, l, d), lambda bi: (bi, 0, 0)),
        out_shape=jax.ShapeDtypeStruct((b, l, d), F32),
        compiler_params=_cparams(("arbitrary",)),
        name="combine_small",
    )(x1, y, pos, wsel, mod, fn_g)


def _combine_big_kernel(cnt_ref, x1_ref, y_ref, pos_ref, w_ref, mod_ref, fn_ref, o_ref, acc_ref, *, cap):
    bi, step = pl.program_id(0), pl.program_id(1)

    def window_terms(e, toks, start, first_slot=None):
        hit = _window_hits(pos_ref[e:e + 1, toks], start, first_slot)
        y = y_ref[e, pl.ds(start, GATHER_SLOTS), :].astype(F32)
        return hit.astype(BF16), (y * _slot_gate(hit, w_ref[e:e + 1, toks])).astype(BF16)

    blocks = []
    for sub in range(COMBINE_BLOCKS):
        toks = slice(sub * GATHER_TOKENS, (sub + 1) * GATHER_TOKENS)
        tb = step * COMBINE_BLOCKS + sub
        windows = [_slot_windows(cnt_ref, bi * N_EXPERTS + e, tb, cap) for e in range(N_EXPERTS)]
        terms = [window_terms(e, toks, start) for e, (start, _) in enumerate(windows)]
        acc_ref[toks, :] = _dot_tn(jnp.concatenate([t[0] for t in terms], axis=0),
                                   jnp.concatenate([t[1] for t in terms], axis=0))
        blocks.append((toks, windows))
    for toks, windows in blocks:
        for e, (start, n_win) in enumerate(windows):
            def extra(k, carry, e=e, toks=toks, start=start):
                hit, gated = window_terms(e, toks, *_next_window(start, k, cap))
                acc_ref[toks, :] += _dot_tn(hit, gated)
                return carry
            lax.fori_loop(1, n_win, extra, 0)
    o_ref[0] = _finish(x1_ref[0], acc_ref[...], mod_ref[0, 5:6], fn_ref[...])


def _combine_big(x1, y, pos, wsel, cnt_flat, mod, fn_g, cap):
    b, l, d = x1.shape
    span = COMBINE_BLOCKS * GATHER_TOKENS
    assert l % span == 0
    grid_spec = pltpu.PrefetchScalarGridSpec(
        num_scalar_prefetch=1,
        grid=(b, l // span),
        in_specs=[pl.BlockSpec((1, span, d), lambda bi, tb, c: (bi, tb, 0)),
                  pl.BlockSpec((N_EXPERTS, cap, d), lambda bi, tb, c: (0, bi, 0)),
                  pl.BlockSpec((N_EXPERTS, span), lambda bi, tb, c: (bi, tb)),
                  pl.BlockSpec((N_EXPERTS, span), lambda bi, tb, c: (bi, tb)),
                  pl.BlockSpec((1, 8, d), lambda bi, tb, c: (bi, 0, 0)),
                  pl.BlockSpec((1, d), lambda bi, tb, c: (0, 0))],
        out_specs=pl.BlockSpec((1, span, d), lambda bi, tb, c: (bi, tb, 0)),
        scratch_shapes=[pltpu.VMEM((span, d), F32)],
    )
    return pl.pallas_call(
        functools.partial(_combine_big_kernel, cap=cap),
        grid_spec=grid_spec,
        out_shape=jax.ShapeDtypeStruct((b, l, d), F32),
        compiler_params=_cparams(("arbitrary", "arbitrary")),
        name="combine_big",
    )(cnt_flat, x1, y, pos, wsel, mod, fn_g)


def _router_parts(w):
    hi, lo = _hi_lo(w)
    pad = ((0, 0), (0, LANES - w.shape[1]))
    return jnp.concatenate([jnp.pad(hi, pad), jnp.pad(lo, pad)], axis=1)


def _rope_tables(l):
    rows = l // GRID_W
    row = jnp.repeat(jnp.arange(rows, dtype=F32), GRID_W)
    col = jnp.tile(jnp.arange(GRID_W, dtype=F32), rows)
    nf = DK // 4
    inv = ROPE_BASE ** (-jnp.arange(nf, dtype=F32) / nf)
    ang = jnp.concatenate([row[:, None] * inv, col[:, None] * inv], axis=-1)
    cos = jnp.repeat(jnp.cos(ang), 2, axis=-1)
    sin = jnp.repeat(jnp.sin(ang), 2, axis=-1) * jnp.tile(jnp.asarray([-1.0, 1.0], F32), DK // 2)
    return cos, sin


def kernel(x_prompt, x_sample, c, state_ret, c_ctx, w_ada, b_ada, norm1_g, w_in, w_pool, pool_scale, w_pool_out,
           ret_decay, ret_gn_g, w_ret_out, w_branch_gate, b_branch_gate, w_out, norm2_g, w_router, w_exp_gate,
           w_exp_up, w_exp_down, final_norm_g):
    depth = w_ada.shape[0]
    assert depth == 1, "single trunk layer"
    bc, lc, d = x_prompt.shape
    bl, ll, _ = x_sample.shape
    assert lc % CHUNK == 0 and STEP_ROWS % lc == 0 and ll % STEP_ROWS == 0 and ll % GATHER_TOKENS == 0
    cap_c = max(1, CAPACITY_FACTOR * lc // N_EXPERTS)
    cap_l = max(1, CAPACITY_FACTOR * ll // N_EXPERTS)
    assert cap_l % GATHER_SLOTS == 0 and ll // GATHER_TOKENS < COUNT_LANES

    cond = jnp.zeros((8 * pl.cdiv(bl + 1, 8), d), F32).at[:bl].set(c).at[bl].set(c_ctx)
    ada = _adaln(cond, w_ada[0], b_ada[0]).reshape(-1, 6, d)
    ada = jnp.pad(ada, ((0, 0), (0, 2), (0, 0)))
    mod_l = ada[:bl]
    mod_c = jnp.broadcast_to(ada[bl:bl + 1], (bc, 8, d))

    p = dict(
        n1g=norm1_g[0][None], w_in=w_in[0].astype(BF16), w_bg=w_branch_gate[0].astype(BF16),
        b_bg=b_branch_gate[0][None], w_pool=w_pool[0].astype(BF16), pool_scale=pool_scale[0][None],
        w_pool_out=w_pool_out[0].astype(BF16), gn_g=ret_gn_g[0][None], w_ret_out=w_ret_out[0].astype(BF16),
        w_out=w_out[0].astype(BF16), n2g=norm2_g[0][None], w_router_parts=_router_parts(w_router[0]),
    )
    decay = ret_decay[0]
    fn_g = final_norm_g[None]

    x1_c, h2_c, aff_c, st_c = _mix(x_prompt, mod_c, None, None, decay, p)
    rope = _rope_tables(ll)
    sb, kv = _revscan(x_sample, mod_l, rope[0], rope[1], state_ret, decay, p["n1g"], p["w_in"])
    x1_l, h2_l, aff_l = _mix(x_sample, mod_l, rope, (state_ret, sb, kv), decay, p)

    wsel_c, pos_c, _ = _route(aff_c.reshape(bc * N_EXPERTS, lc), cap_c, min(ROUTE_ROWS, bc * N_EXPERTS))
    wsel_l, pos_l, cnt_l = _route(aff_l.reshape(bl * N_EXPERTS, ll), cap_l, min(ROUTE_ROWS, bl * N_EXPERTS))
    cnt_flat = cnt_l.reshape(-1)

    xs_c = _gather_small(h2_c, pos_c, cap_c)
    xs_l = _gather_big(h2_l, pos_l, cnt_flat, cap_l)
    y_c, y_l = _ffn(xs_c, xs_l, w_exp_gate[0], w_exp_up[0], w_exp_down[0])

    y_prompt = _combine_small(x1_c, y_c, pos_c, wsel_c, mod_c, fn_g, cap_c)
    y_sample = _combine_big(x1_l, y_l, pos_l, wsel_l, cnt_flat, mod_l, fn_g, cap_l)
    new_state = st_c[:, None].astype(x_prompt.dtype)
    return (y_prompt, y_sample, new_state)
```

```python
import functools

import jax
import jax.numpy as jnp
from jax import lax
from jax.experimental import pallas as pl
from jax.experimental.pallas import tpu as pltpu

F32 = jnp.float32
BF16 = jnp.bfloat16

N_HEADS = 4
DK = 128
DV = 256
POOL_WINDOWS = (2, 4, 8, 16)
POOL_GROUP = 128
POOL_WIDTH = POOL_GROUP * len(POOL_WINDOWS)
QK_WIDTH = N_HEADS * DK
V_WIDTH = N_HEADS * DV
N_EXPERTS = 16
CAPACITY_FACTOR = 2
GRID_W = 64
ROPE_BASE = 10000.0
EPS = 1e-6

LANES = 128
CHUNK = 256
STEP_ROWS = 512
HALO = 16
ROUTE_LANES = 256
ROUTE_ROWS = 512
SMALL_REQS = 4
GATHER_TOKENS = 256
GATHER_SLOTS = 64
GATHER_EXPERTS = 4
COMBINE_BLOCKS = 2
COUNT_LANES = 128
SLOT_ALIGN = 16
FF_TILE = 256
FFN_ROWS = 512
VMEM_LIMIT = 56 * 1024 * 1024


def _cparams(sem):
    return pltpu.CompilerParams(dimension_semantics=sem, vmem_limit_bytes=VMEM_LIMIT)


def _resident(shape):
    return pl.BlockSpec(shape, lambda *_: (0,) * len(shape), pipeline_mode=pl.Buffered(1))


def _sigmoid(x):
    return 0.5 * jnp.tanh(0.5 * x) + 0.5


def _silu(x):
    return x * _sigmoid(x)


def _norm_mod(x, g, scale, shift):
    y = x * lax.rsqrt(jnp.mean(x * x, axis=-1, keepdims=True) + EPS)
    return (y * g) * (1.0 + scale) + shift


def _hi_lo(x):
    hi = x.astype(BF16)
    return hi, (x - hi.astype(F32)).astype(BF16)


def _dot(a, b):
    return jnp.dot(a, b, preferred_element_type=F32)


def _dot_nt(a, b):
    return lax.dot_general(a, b, (((1,), (1,)), ((), ())), preferred_element_type=F32)


def _dot_tn(a, b):
    return lax.dot_general(a, b, (((0,), (0,)), ((), ())), preferred_element_type=F32)


def _adaln_kernel(c_ref, w_ref, b_ref, o_ref):
    rows = c_ref.shape[0]
    s_hi, s_lo = _hi_lo(_silu(c_ref[...]))
    w_hi, w_lo = _hi_lo(w_ref[...])
    by_hi = _dot(jnp.concatenate([s_hi, s_lo], axis=0), w_hi)
    o_ref[...] = by_hi[:rows] + by_hi[rows:] + _dot(s_hi, w_lo) + b_ref[...]


def _adaln(cond, w, b):
    rows, d = cond.shape
    n = w.shape[1]
    tn = d
    return pl.pallas_call(
        _adaln_kernel,
        grid=(n // tn,),
        in_specs=[pl.BlockSpec((rows, d), lambda i: (0, 0)),
                  pl.BlockSpec((d, tn), lambda i: (0, i)),
                  pl.BlockSpec((1, tn), lambda i: (0, i))],
        out_specs=pl.BlockSpec((rows, tn), lambda i: (0, i)),
        out_shape=jax.ShapeDtypeStruct((rows, n), F32),
        compiler_params=_cparams(("arbitrary",)),
        name="adaln",
    )(cond, w, b.reshape(1, n))


def _decay_tiles(decay_ref, head):
    def log_gamma(direction, shape):
        return -jnp.exp(jnp.full(shape, decay_ref[direction, head], F32))

    diff = (lax.broadcasted_iota(jnp.int32, (CHUNK, CHUNK), 0)
            - lax.broadcasted_iota(jnp.int32, (CHUNK, CHUNK), 1)).astype(F32)
    i = lax.broadcasted_iota(jnp.int32, (CHUNK, DK), 0).astype(F32)
    lgf, lgb = log_gamma(0, (CHUNK, DK)), log_gamma(1, (CHUNK, DK))
    return dict(
        mask=jnp.where(diff >= 0.0, jnp.exp(log_gamma(0, (CHUNK, CHUNK)) * jnp.maximum(diff, 0.0)),
                       jnp.exp(log_gamma(1, (CHUNK, CHUNK)) * jnp.maximum(-diff, 0.0))),
        qdf=jnp.exp(lgf * (i + 1.0)),
        qdb=jnp.exp(lgb * (CHUNK - i)),
        kdf=jnp.exp(lgf * (CHUNK - 1.0 - i)),
        kdb=jnp.exp(lgb * i),
        cdf=jnp.exp(log_gamma(0, (DK, DV)) * float(CHUNK)),
        cdb=jnp.exp(log_gamma(1, (DK, DV)) * float(CHUNK)),
    )


def _rope(x, cos, sin):
    even = (lax.broadcasted_iota(jnp.int32, x.shape, 1) % 2) == 0
    partner = jnp.where(even, pltpu.roll(x, x.shape[1] - 1, 1), pltpu.roll(x, 1, 1))
    return x * cos + partner * sin


def _state_spec(direction):
    return pl.BlockSpec((1, 1, 1, N_HEADS, DK, DV), lambda bi, j: (bi, 0, direction, 0, 0, 0))


def _revscan_kernel(x_ref, mod_ref, cos_ref, sin_ref, s0_ref, decay_ref, n1_ref, w_in_ref, sb_ref, kv_ref, s_scr):
    j = pl.program_id(1)

    @pl.when(j == 0)
    def _():
        s_scr[...] = s0_ref[0, 0, 0]

    mod = mod_ref[0]
    h = _norm_mod(x_ref[0], n1_ref[...], mod[1:2], mod[0:1]).astype(BF16)
    k_lo = POOL_WIDTH + QK_WIDTH
    kv = _dot(h, w_in_ref[:, k_lo:k_lo + QK_WIDTH + V_WIDTH])
    sb_ref[0, 0] = s_scr[...]
    kv_ref[0, :, QK_WIDTH:] = kv[:, QK_WIDTH:].astype(BF16)
    for head in range(N_HEADS):
        t = _decay_tiles(decay_ref, head)
        s = s_scr[head]
        for c in reversed(range(STEP_ROWS // CHUNK)):
            rows = slice(c * CHUNK, (c + 1) * CHUNK)
            cols = slice(head * DK, (head + 1) * DK)
            k = _rope(kv[rows, cols] * (DK ** -0.5), cos_ref[rows], sin_ref[rows])
            kv_ref[0, rows, cols] = k.astype(BF16)
            v = kv[rows, QK_WIDTH + head * DV:QK_WIDTH + (head + 1) * DV]
            s = t["cdb"] * s + _dot_tn((k * t["kdb"]).astype(BF16), v.astype(BF16))
        s_scr[head] = s


def _revscan(x, mod, cos, sin, state_ret, decay, n1g, w_in):
    b, l, d = x.shape
    nblk = l // STEP_ROWS
    return pl.pallas_call(
        _revscan_kernel,
        grid=(b, nblk),
        in_specs=[pl.BlockSpec((1, STEP_ROWS, d), lambda bi, j: (bi, nblk - 1 - j, 0)),
                  pl.BlockSpec((1, 8, d), lambda bi, j: (bi, 0, 0)),
                  pl.BlockSpec((STEP_ROWS, DK), lambda bi, j: (nblk - 1 - j, 0)),
                  pl.BlockSpec((STEP_ROWS, DK), lambda bi, j: (nblk - 1 - j, 0)),
                  _state_spec(1),
                  pl.BlockSpec(memory_space=pltpu.SMEM),
                  _resident(n1g.shape),
                  _resident(w_in.shape)],
        out_specs=[pl.BlockSpec((1, 1, N_HEADS, DK, DV), lambda bi, j: (bi, nblk - 1 - j, 0, 0, 0)),
                   pl.BlockSpec((1, STEP_ROWS, QK_WIDTH + V_WIDTH), lambda bi, j: (bi, nblk - 1 - j, 0))],
        out_shape=[jax.ShapeDtypeStruct((b, nblk, N_HEADS, DK, DV), F32),
                   jax.ShapeDtypeStruct((b, l, QK_WIDTH + V_WIDTH), BF16)],
        scratch_shapes=[pltpu.VMEM((N_HEADS, DK, DV), F32)],
        compiler_params=_cparams(("arbitrary", "arbitrary")),
        name="revscan",
    )(x, mod, cos, sin, state_ret, decay, n1g, w_in)


def _mix_kernel(*refs, seq_len, n_seq, seq_rows, use_rope, has_state, emit_state):
    it = iter(refs)
    x_ref, xp_ref, xn_ref, mod_ref = next(it), next(it), next(it), next(it)
    cos_ref = sin_ref = s0f_ref = sb_ref = kv_ref = None
    if use_rope:
        cos_ref, sin_ref = next(it), next(it)
    if has_state:
        s0f_ref, sb_ref, kv_ref = next(it), next(it), next(it)
    (decay_ref, n1_ref, w_in_ref, w_bg_ref, b_bg_ref, w_pool_ref, pscale_ref, w_po_ref, gn_ref, w_ro_ref,
     w_out_ref, n2_ref, w_rt_ref) = (next(it) for _ in range(13))
    x1_ref, h2_ref, aff_ref = next(it), next(it), next(it)
    st_ref = next(it) if emit_state else None
    sf_scr = next(it)

    j = pl.program_id(1)
    d = x_ref.shape[2]
    n_chunks = seq_rows // CHUNK
    ext = seq_rows + 2 * HALO

    @pl.when(j == 0)
    def _():
        if has_state:
            sf_scr[...] = s0f_ref[:, 0, 0]
        else:
            sf_scr[...] = jnp.zeros(sf_scr.shape, F32)

    mods = [mod_ref[s] for s in range(n_seq)]

    def rows_of(vals):
        return jnp.concatenate([jnp.broadcast_to(v, (seq_rows, d)) for v in vals], axis=0) if n_seq > 1 else vals[0]

    he_parts = []
    for s in range(n_seq):
        xe = jnp.concatenate([xp_ref[s], x_ref[s], xn_ref[s]], axis=0)
        he_parts.append(_norm_mod(xe, n1_ref[...], mods[s][1:2], mods[s][0:1]).astype(BF16))
    he = jnp.concatenate(he_parts, axis=0) if n_seq > 1 else he_parts[0]
    hb_parts = [hp[HALO:HALO + seq_rows] for hp in he_parts]
    hb = jnp.concatenate(hb_parts, axis=0) if n_seq > 1 else hb_parts[0]
    x = jnp.concatenate([x_ref[s] for s in range(n_seq)], axis=0) if n_seq > 1 else x_ref[0]
    gates = _sigmoid(_dot(hb, w_bg_ref[...]) + b_bg_ref[...])

    ue_all = _dot(he, w_in_ref[:, :POOL_WIDTH])
    epos = j * seq_rows - HALO + lax.broadcasted_iota(jnp.int32, (ext, 1), 0)
    valid = (epos >= 0) & (epos < seq_len)
    tpos = j * seq_rows + lax.broadcasted_iota(jnp.int32, (seq_rows, 1), 0)
    pooled = [[] for _ in POOL_WINDOWS]
    for s in range(n_seq):
        ue = jnp.where(valid, ue_all[s * ext:(s + 1) * ext], 0.0)
        for gi, w in enumerate(POOL_WINDOWS):
            ug = ue[:, gi * POOL_GROUP:(gi + 1) * POOL_GROUP]
            acc, shift = ug, 1
            while shift < w:
                acc = acc + pltpu.roll(acc, shift, 0)
                shift *= 2
            if w // 2 > 1:
                acc = pltpu.roll(acc, ext - (w // 2 - 1), 0)
            cnt = (jnp.minimum(tpos + w // 2, seq_len) - jnp.maximum(tpos - w // 2, 0)).astype(F32)
            own = slice(HALO, HALO + seq_rows)
            pooled[gi].append((acc[own] / cnt - ug[own]).astype(BF16))
    pool_h = jnp.concatenate(
        [_dot(jnp.concatenate(pg, axis=0) if len(pg) > 1 else pg[0], w_pool_ref[gi]) for gi, pg in enumerate(pooled)],
        axis=1) * pscale_ref[...]
    pool_y = _dot(pool_h.astype(BF16), w_po_ref[...])

    g_lo = POOL_WIDTH + 2 * QK_WIDTH + V_WIDTH
    if has_state:
        zq = _dot(hb, w_in_ref[:, POOL_WIDTH:POOL_WIDTH + QK_WIDTH])
        g_all = _dot(hb, w_in_ref[:, g_lo:])
    else:
        zr = _dot(hb, w_in_ref[:, POOL_WIDTH:])
        zq, g_all = zr[:, :QK_WIDTH], zr[:, 2 * QK_WIDTH + V_WIDTH:]
    seq_out = []
    for s in range(n_seq):
        r0 = s * seq_rows
        head_out = []
        for head in range(N_HEADS):
            t = _decay_tiles(decay_ref, head)
            qs, ks, vs = [], [], []
            for c in range(n_chunks):
                rows = slice(r0 + c * CHUNK, r0 + (c + 1) * CHUNK)
                crow = slice(c * CHUNK, (c + 1) * CHUNK)
                q = zq[rows, head * DK:(head + 1) * DK]
                if use_rope:
                    q = _rope(q, cos_ref[crow], sin_ref[crow])
                if has_state:
                    k = kv_ref[s, crow, head * DK:(head + 1) * DK].astype(F32)
                    v = kv_ref[s, crow, QK_WIDTH + head * DV:QK_WIDTH + (head + 1) * DV]
                else:
                    k = zr[rows, QK_WIDTH + head * DK:QK_WIDTH + (head + 1) * DK] * (DK ** -0.5)
                    if use_rope:
                        k = _rope(k, cos_ref[crow], sin_ref[crow])
                    v = zr[rows, 2 * QK_WIDTH + head * DV:2 * QK_WIDTH + (head + 1) * DV].astype(BF16)
                qs.append(q)
                ks.append(k)
                vs.append(v)
            sf = [sf_scr[s, head]]
            for c in range(n_chunks):
                sf.append(t["cdf"] * sf[c] + _dot_tn((ks[c] * t["kdf"]).astype(BF16), vs[c]))
            sf_scr[s, head] = sf[n_chunks]
            sb = [None] * (n_chunks + 1)
            sb[n_chunks] = sb_ref[s, 0, head] if has_state else jnp.zeros((DK, DV), F32)
            for c in reversed(range(n_chunks)):
                sb[c] = t["cdb"] * sb[c + 1] + _dot_tn((ks[c] * t["kdb"]).astype(BF16), vs[c])
            if emit_state:
                st_ref[s, 0, head] = sf[n_chunks]
                st_ref[s, 1, head] = sb[0]
            outs = []
            for c in range(n_chunks):
                scores = _dot_nt(qs[c].astype(BF16), ks[c].astype(BF16)) * t["mask"]
                q_both = jnp.concatenate([(qs[c] * t["qdf"]).astype(BF16), (qs[c] * t["qdb"]).astype(BF16)], axis=1)
                s_both = jnp.concatenate([sf[c].astype(BF16), sb[c + 1].astype(BF16)], axis=0)
                outs.append(_dot(scores.astype(BF16), vs[c]) + _dot(q_both, s_both))
            o = jnp.concatenate(outs, axis=0)
            mu = jnp.mean(o, axis=-1, keepdims=True)
            oc = o - mu
            var = jnp.mean(oc * oc, axis=-1, keepdims=True)
            head_out.append(oc * lax.rsqrt(var + EPS))
        seq_out.append(jnp.concatenate(head_out, axis=1))
    o_n = (jnp.concatenate(seq_out, axis=0) if n_seq > 1 else seq_out[0]) * gn_ref[...]
    ret_y = _dot((_silu(g_all) * o_n).astype(BF16), w_ro_ref[...])

    merged = gates[:, :d] * pool_y + gates[:, d:] * ret_y
    x1 = x + rows_of([m[2:3] for m in mods]) * _dot(merged.astype(BF16), w_out_ref[...])

    h2 = _norm_mod(x1, n2_ref[...], rows_of([m[4:5] for m in mods]), rows_of([m[3:4] for m in mods]))
    h2_hi, h2_lo = _hi_lo(h2)
    n_rows = n_seq * seq_rows
    parts = _dot(jnp.concatenate([h2_hi, h2_lo], axis=0), w_rt_ref[...])
    logits_t = (parts[:n_rows, :LANES] + parts[n_rows:, :LANES] + parts[:n_rows, LANES:])
    for s in range(n_seq):
        rows = slice(s * seq_rows, (s + 1) * seq_rows)
        x1_ref[s] = x1[rows]
        h2_ref[s] = h2_hi[rows]
        logits = logits_t[rows].T[:N_EXPERTS]
        e = jnp.exp(logits - jnp.max(logits, axis=0, keepdims=True))
        aff_ref[s] = e / jnp.sum(e, axis=0, keepdims=True)


def _mix(x, mod, rope, states, decay, p):
    b, l, d = x.shape
    seq_rows = min(l, STEP_ROWS)
    n_seq = STEP_ROWS // seq_rows
    nblk = l // seq_rows
    hb = seq_rows // HALO
    n_halo = l // HALO
    use_rope, has_state = rope is not None, states is not None
    emit_state = not has_state
    assert b % n_seq == 0 and (n_seq == 1 or not (use_rope or has_state))
    in_specs = [pl.BlockSpec((n_seq, seq_rows, d), lambda bi, j: (bi, j, 0)),
                pl.BlockSpec((n_seq, HALO, d), lambda bi, j: (bi, jnp.maximum(j * hb - 1, 0), 0)),
                pl.BlockSpec((n_seq, HALO, d), lambda bi, j: (bi, jnp.minimum((j + 1) * hb, n_halo - 1), 0)),
                pl.BlockSpec((n_seq, 8, d), lambda bi, j: (bi, 0, 0))]
    args = [x, x, x, mod]
    if use_rope:
        in_specs += [pl.BlockSpec((seq_rows, DK), lambda bi, j: (j, 0))] * 2
        args += list(rope)
    if has_state:
        in_specs += [_state_spec(0),
                     pl.BlockSpec((1, 1, N_HEADS, DK, DV), lambda bi, j: (bi, j, 0, 0, 0)),
                     pl.BlockSpec((1, seq_rows, QK_WIDTH + V_WIDTH), lambda bi, j: (bi, j, 0))]
        args += list(states)
    weights = [p["n1g"], p["w_in"], p["w_bg"], p["b_bg"], p["w_pool"], p["pool_scale"], p["w_pool_out"],
               p["gn_g"], p["w_ret_out"], p["w_out"], p["n2g"], p["w_router_parts"]]
    in_specs.append(pl.BlockSpec(memory_space=pltpu.SMEM))
    args.append(decay)
    for w in weights:
        in_specs.append(_resident(w.shape))
        args.append(w)
    out_specs = [pl.BlockSpec((n_seq, seq_rows, d), lambda bi, j: (bi, j, 0)),
                 pl.BlockSpec((n_seq, seq_rows, d), lambda bi, j: (bi, j, 0)),
                 pl.BlockSpec((n_seq, N_EXPERTS, seq_rows), lambda bi, j: (bi, 0, j))]
    out_shape = [jax.ShapeDtypeStruct((b, l, d), F32), jax.ShapeDtypeStruct((b, l, d), BF16),
                 jax.ShapeDtypeStruct((b, N_EXPERTS, l), F32)]
    if emit_state:
        out_specs.append(pl.BlockSpec((n_seq, 2, N_HEADS, DK, DV), lambda bi, j: (bi, 0, 0, 0, 0)))
        out_shape.append(jax.ShapeDtypeStruct((b, 2, N_HEADS, DK, DV), F32))
    kern = functools.partial(_mix_kernel, seq_len=l, n_seq=n_seq, seq_rows=seq_rows, use_rope=use_rope,
                             has_state=has_state, emit_state=emit_state)
    return pl.pallas_call(
        kern,
        grid=(b // n_seq, nblk),
        in_specs=in_specs,
        out_specs=out_specs,
        out_shape=out_shape,
        scratch_shapes=[pltpu.VMEM((n_seq, N_HEADS, DK, DV), F32)],
        compiler_params=_cparams(("arbitrary", "arbitrary")),
        name="mix_rope" if use_rope else "mix",
    )(*args)


def _route_kernel(aff_ref, wsel_ref, pos_ref, cnt_ref, *, cap):
    aff = aff_ref[...]
    rows, l = aff.shape

    def as_value(bits):
        return pltpu.bitcast(bits, F32)

    def bisect(i, tau):
        cand = tau | jnp.left_shift(jnp.int32(1), 30 - i)
        cnt = jnp.sum((aff >= as_value(cand)).astype(F32), axis=1, keepdims=True)
        return jnp.where(cnt >= cap, cand, tau)

    tau = lax.fori_loop(0, 31, bisect, jnp.zeros((rows, 1), jnp.int32))
    gt = aff >= as_value(tau + 1)
    eq = (aff >= as_value(tau)) & jnp.logical_not(gt)
    need = cap - jnp.sum(gt.astype(F32), axis=1, keepdims=True)

    nlb = l // ROUTE_LANES
    tri = (lax.broadcasted_iota(jnp.int32, (ROUTE_LANES, ROUTE_LANES), 0)
           < lax.broadcasted_iota(jnp.int32, (ROUTE_LANES, ROUTE_LANES), 1)).astype(BF16)

    def prefix(mask):
        carry = jnp.zeros((rows, 1), F32)
        parts, starts = [], []
        for blk in range(nlb):
            mb = mask[:, blk * ROUTE_LANES:(blk + 1) * ROUTE_LANES].astype(F32)
            starts.append(carry)
            parts.append(_dot(mb.astype(BF16), tri) + carry)
            carry = carry + jnp.sum(mb, axis=1, keepdims=True)
        starts.append(carry)
        return jnp.concatenate(parts, axis=1), starts

    eq_rank, _ = prefix(eq)
    sel = gt | (eq & (eq_rank < need))
    pos, starts = prefix(sel)
    wsel_ref[...] = jnp.where(sel, aff, 0.0)
    pos_ref[...] = jnp.where(sel, pos.astype(jnp.int32), -1)
    lane = lax.broadcasted_iota(jnp.int32, (rows, COUNT_LANES), 1)
    table = jnp.zeros((rows, COUNT_LANES), jnp.int32)
    per = max(GATHER_TOKENS // ROUTE_LANES, 1)
    for tb in range(nlb // per + 1):
        table = jnp.where(lane == tb, starts[min(tb * per, nlb)].astype(jnp.int32), table)
    cnt_ref[...] = table


def _route(aff_rows, cap, row_block):
    r, l = aff_rows.shape
    spec = pl.BlockSpec((row_block, l), lambda i: (i, 0))
    return pl.pallas_call(
        functools.partial(_route_kernel, cap=cap),
        grid=(r // row_block,),
        in_specs=[spec],
        out_specs=[spec, spec, pl.BlockSpec((row_block, COUNT_LANES), lambda i: (i, 0))],
        out_shape=[jax.ShapeDtypeStruct((r, l), F32), jax.ShapeDtypeStruct((r, l), jnp.int32),
                   jax.ShapeDtypeStruct((r, COUNT_LANES), jnp.int32)],
        compiler_params=_cparams(("arbitrary",)),
        name=f"route_{l}",
    )(aff_rows)


def _stacked_hits(pos_ref, req, cap):
    l = pos_ref.shape[1]
    slot = lax.broadcasted_iota(jnp.int32, (cap, l), 0)
    return [pos_ref[req * N_EXPERTS + e:req * N_EXPERTS + e + 1, :] == slot for e in range(N_EXPERTS)]


def _gather_small_kernel(h_ref, pos_ref, xs_ref, *, cap):
    for req in range(h_ref.shape[0]):
        onehot = jnp.concatenate([h.astype(BF16) for h in _stacked_hits(pos_ref, req, cap)], axis=0)
        xs = _dot(onehot, h_ref[req]).astype(BF16)
        for e in range(N_EXPERTS):
            xs_ref[e, req * cap:(req + 1) * cap, :] = xs[e * cap:(e + 1) * cap]


def _gather_small(h2, pos, cap):
    b, l, d = h2.shape
    nr = SMALL_REQS if b % SMALL_REQS == 0 else 1
    return pl.pallas_call(
        functools.partial(_gather_small_kernel, cap=cap),
        grid=(b // nr,),
        in_specs=[pl.BlockSpec((nr, l, d), lambda bi: (bi, 0, 0)),
                  pl.BlockSpec((nr * N_EXPERTS, l), lambda bi: (bi, 0))],
        out_specs=pl.BlockSpec((N_EXPERTS, nr * cap, d), lambda bi: (0, bi, 0)),
        out_shape=jax.ShapeDtypeStruct((N_EXPERTS, b * cap, d), BF16),
        compiler_params=_cparams(("arbitrary",)),
        name="gather_small",
    )(h2, pos)


def _slot_windows(cnt_ref, row, tb, cap):
    lo = cnt_ref[row * COUNT_LANES + tb]
    hi = cnt_ref[row * COUNT_LANES + tb + 1]
    start = jnp.minimum(lo & jnp.int32(-SLOT_ALIGN), jnp.int32(cap - GATHER_SLOTS))
    n_win = lax.shift_right_logical(hi - start + jnp.int32(GATHER_SLOTS - 1),
                                    jnp.int32(GATHER_SLOTS.bit_length() - 1))
    return pl.multiple_of(start, SLOT_ALIGN), n_win


def _next_window(start, k, cap):
    first = start + k * GATHER_SLOTS
    return pl.multiple_of(jnp.minimum(first, cap - GATHER_SLOTS), SLOT_ALIGN), first


def _window_hits(pos_row, start, first_slot=None):
    slot = start + lax.broadcasted_iota(jnp.int32, (GATHER_SLOTS, pos_row.shape[1]), 0)
    hit = pos_row == slot
    return hit if first_slot is None else hit & (slot >= first_slot)


def _gather_big_kernel(cnt_ref, h_ref, pos_ref, xs_ref, *, cap):
    bi, eg = pl.program_id(0), pl.program_id(1)
    n_tb = h_ref.shape[1] // GATHER_TOKENS
    xs_ref[...] = jnp.zeros(xs_ref.shape, BF16)

    def onehot(el, tb, start, first_slot=None):
        toks = slice(tb * GATHER_TOKENS, (tb + 1) * GATHER_TOKENS)
        return _window_hits(pos_ref[pl.ds(eg * GATHER_EXPERTS + el, 1), toks], start, first_slot).astype(BF16)

    windows = {(tb, el): _slot_windows(cnt_ref, bi * N_EXPERTS + eg * GATHER_EXPERTS + el, tb, cap)
               for tb in range(n_tb) for el in range(GATHER_EXPERTS)}
    for tb in range(n_tb):
        toks = slice(tb * GATHER_TOKENS, (tb + 1) * GATHER_TOKENS)
        stacked = jnp.concatenate([onehot(el, tb, windows[tb, el][0]) for el in range(GATHER_EXPERTS)], axis=0)
        rows = _dot(stacked, h_ref[0, toks, :]).astype(BF16)
        for el in range(GATHER_EXPERTS):
            xs_ref[el, pl.ds(windows[tb, el][0], GATHER_SLOTS), :] += rows[el * GATHER_SLOTS:(el + 1) * GATHER_SLOTS]
    for (tb, el), (start, n_win) in windows.items():
        def extra(k, carry, tb=tb, el=el, start=start):
            clamped, first = _next_window(start, k, cap)
            toks = slice(tb * GATHER_TOKENS, (tb + 1) * GATHER_TOKENS)
            xs_ref[el, pl.ds(clamped, GATHER_SLOTS), :] += _dot(onehot(el, tb, clamped, first),
                                                                h_ref[0, toks, :]).astype(BF16)
            return carry
        lax.fori_loop(1, n_win, extra, 0)


def _gather_big(h2, pos, cnt_flat, cap):
    b, l, d = h2.shape
    grid_spec = pltpu.PrefetchScalarGridSpec(
        num_scalar_prefetch=1,
        grid=(b, N_EXPERTS // GATHER_EXPERTS),
        in_specs=[pl.BlockSpec((1, l, d), lambda bi, eg, c: (bi, 0, 0)),
                  pl.BlockSpec((N_EXPERTS, l), lambda bi, eg, c: (bi, 0))],
        out_specs=pl.BlockSpec((GATHER_EXPERTS, cap, d), lambda bi, eg, c: (eg, bi, 0)),
    )
    return pl.pallas_call(
        functools.partial(_gather_big_kernel, cap=cap),
        grid_spec=grid_spec,
        out_shape=jax.ShapeDtypeStruct((N_EXPERTS, b * cap, d), BF16),
        compiler_params=_cparams(("arbitrary", "arbitrary")),
        name="gather_big",
    )(cnt_flat, h2, pos)


def _ffn_kernel(xa_ref, xb_ref, wg_ref, wu_ref, wd_ref, ya_ref, yb_ref, acca_ref, accb_ref):
    f, nf = pl.program_id(1), pl.num_programs(1)

    def ff_tile(first, last):
        wg = wg_ref[0].astype(BF16)
        wu = wu_ref[0].astype(BF16)
        wd = wd_ref[0].astype(BF16)
        for x_ref, y_ref, acc_ref in ((xa_ref, ya_ref, acca_ref), (xb_ref, yb_ref, accb_ref)):
            m = x_ref.shape[1]
            step = min(FFN_ROWS, m)
            for r0 in range(0, m, step):
                rows = slice(r0, r0 + step)
                x = x_ref[0, rows, :]
                hid = (_silu(_dot(x, wg)) * _dot(x, wu)).astype(BF16)
                part = _dot(hid, wd)
                if first:
                    acc_ref[rows, :] = part
                elif last:
                    y_ref[0, rows, :] = (acc_ref[rows, :] + part).astype(BF16)
                else:
                    acc_ref[rows, :] += part

    pl.when(f == 0)(functools.partial(ff_tile, True, False))
    pl.when((f > 0) & (f < nf - 1))(functools.partial(ff_tile, False, False))
    pl.when(f == nf - 1)(functools.partial(ff_tile, False, True))


def _ffn(xs_a, xs_b, w_gate, w_up, w_down):
    e, ma, d = xs_a.shape
    mb = xs_b.shape[1]
    ff = w_gate.shape[2]
    nf = ff // FF_TILE
    assert ff % FF_TILE == 0 and nf >= 2
    return pl.pallas_call(
        _ffn_kernel,
        grid=(e, nf),
        in_specs=[pl.BlockSpec((1, ma, d), lambda ei, f: (ei, 0, 0)),
                  pl.BlockSpec((1, mb, d), lambda ei, f: (ei, 0, 0)),
                  pl.BlockSpec((1, d, FF_TILE), lambda ei, f: (ei, 0, f)),
                  pl.BlockSpec((1, d, FF_TILE), lambda ei, f: (ei, 0, f)),
                  pl.BlockSpec((1, FF_TILE, d), lambda ei, f: (ei, f, 0))],
        out_specs=[pl.BlockSpec((1, ma, d), lambda ei, f: (ei, 0, 0)),
                   pl.BlockSpec((1, mb, d), lambda ei, f: (ei, 0, 0))],
        out_shape=[jax.ShapeDtypeStruct((e, ma, d), BF16), jax.ShapeDtypeStruct((e, mb, d), BF16)],
        scratch_shapes=[pltpu.VMEM((ma, d), F32), pltpu.VMEM((mb, d), F32)],
        compiler_params=_cparams(("arbitrary", "arbitrary")),
        name="ffn",
    )(xs_a, xs_b, w_gate, w_up, w_down)


def _slot_gate(hit, w_row):
    return jnp.sum(jnp.where(hit, w_row, 0.0), axis=1, keepdims=True)


def _finish(x1, moe, gate2, fn_g):
    x = x1 + gate2 * moe
    return (x * lax.rsqrt(jnp.mean(x * x, axis=-1, keepdims=True) + EPS)) * fn_g


def _combine_small_kernel(x1_ref, y_ref, pos_ref, w_ref, mod_ref, fn_ref, o_ref, *, cap):
    for req in range(x1_ref.shape[0]):
        hits = _stacked_hits(pos_ref, req, cap)
        gated = [(y_ref[e, req * cap:(req + 1) * cap, :].astype(F32)
                  * _slot_gate(hits[e], w_ref[req * N_EXPERTS + e:req * N_EXPERTS + e + 1, :])).astype(BF16)
                 for e in range(N_EXPERTS)]
        onehot = jnp.concatenate([h.astype(BF16) for h in hits], axis=0)
        moe = _dot_tn(onehot, jnp.concatenate(gated, axis=0))
        o_ref[req] = _finish(x1_ref[req], moe, mod_ref[req, 5:6], fn_ref[...])


def _combine_small(x1, y, pos, wsel, mod, fn_g, cap):
    b, l, d = x1.shape
    nr = SMALL_REQS if b % SMALL_REQS == 0 else 1
    return pl.pallas_call(
        functools.partial(_combine_small_kernel, cap=cap),
        grid=(b // nr,),
        in_specs=[pl.BlockSpec((nr, l, d), lambda bi: (bi, 0, 0)),
                  pl.BlockSpec((N_EXPERTS, nr * cap, d), lambda bi: (0, bi, 0)),
                  pl.BlockSpec((nr * N_EXPERTS, l), lambda bi: (bi, 0)),
                  pl.BlockSpec((nr * N_EXPERTS, l), lambda bi: (bi, 0)),
                  pl.BlockSpec((nr, 8, d), lambda bi: (bi, 0, 0)),
                  pl.BlockSpec((1, d), lambda bi: (0, 0))],
        out_specs=pl.BlockSpec((nr, l, d), lambda bi: (bi, 0, 0)),
        out_shape=jax.ShapeDtypeStruct((b, l, d), F32),
        compiler_params=_cparams(("arbitrary",)),
        name="combine_small",
    )(x1, y, pos, wsel, mod, fn_g)


def _combine_big_kernel(cnt_ref, x1_ref, y_ref, pos_ref, w_ref, mod_ref, fn_ref, o_ref, acc_ref, *, cap):
    bi, step = pl.program_id(0), pl.program_id(1)

    def window_terms(e, toks, start, first_slot=None):
        hit = _window_hits(pos_ref[e:e + 1, toks], start, first_slot)
        y = y_ref[e, pl.ds(start, GATHER_SLOTS), :].astype(F32)
        return hit.astype(BF16), (y * _slot_gate(hit, w_ref[e:e + 1, toks])).astype(BF16)

    def finish(toks, moe):
        o_ref[0, toks, :] = _finish(x1_ref[0, toks, :], moe, mod_ref[0, 5:6], fn_ref[...])

    blocks, overfull = [], False
    for sub in range(COMBINE_BLOCKS):
        toks = slice(sub * GATHER_TOKENS, (sub + 1) * GATHER_TOKENS)
        tb = step * COMBINE_BLOCKS + sub
        windows = [_slot_windows(cnt_ref, bi * N_EXPERTS + e, tb, cap) for e in range(N_EXPERTS)]
        terms = [window_terms(e, toks, start) for e, (start, _) in enumerate(windows)]
        moe = _dot_tn(jnp.concatenate([t[0] for t in terms], axis=0), jnp.concatenate([t[1] for t in terms], axis=0))
        acc_ref[toks, :] = moe
        finish(toks, moe)
        blocks.append((toks, windows))
        for _, n_win in windows:
            overfull = overfull | (n_win > 1)

    @pl.when(overfull)
    def _():
        for toks, windows in blocks:
            for e, (start, n_win) in enumerate(windows):
                def extra(k, carry, e=e, toks=toks, start=start):
                    hit, gated = window_terms(e, toks, *_next_window(start, k, cap))
                    acc_ref[toks, :] += _dot_tn(hit, gated)
                    return carry
                lax.fori_loop(1, n_win, extra, 0)
            finish(toks, acc_ref[toks, :])


def _combine_big(x1, y, pos, wsel, cnt_flat, mod, fn_g, cap):
    b, l, d = x1.shape
    span = COMBINE_BLOCKS * GATHER_TOKENS
    assert l % span == 0
    grid_spec = pltpu.PrefetchScalarGridSpec(
        num_scalar_prefetch=1,
        grid=(b, l // span),
        in_specs=[pl.BlockSpec((1, span, d), lambda bi, tb, c: (bi, tb, 0)),
                  pl.BlockSpec((N_EXPERTS, cap, d), lambda bi, tb, c: (0, bi, 0)),
                  pl.BlockSpec((N_EXPERTS, span), lambda bi, tb, c: (bi, tb)),
                  pl.BlockSpec((N_EXPERTS, span), lambda bi, tb, c: (bi, tb)),
                  pl.BlockSpec((1, 8, d), lambda bi, tb, c: (bi, 0, 0)),
                  pl.BlockSpec((1, d), lambda bi, tb, c: (0, 0))],
        out_specs=pl.BlockSpec((1, span, d), lambda bi, tb, c: (bi, tb, 0)),
        scratch_shapes=[pltpu.VMEM((span, d), F32)],
    )
    return pl.pallas_call(
        functools.partial(_combine_big_kernel, cap=cap),
        grid_spec=grid_spec,
        out_shape=jax.ShapeDtypeStruct((b, l, d), F32),
        compiler_params=_cparams(("arbitrary", "arbitrary")),
        name="combine_big",
    )(cnt_flat, x1, y, pos, wsel, mod, fn_g)


def _router_parts(w):
    hi, lo = _hi_lo(w)
    pad = ((0, 0), (0, LANES - w.shape[1]))
    return jnp.concatenate([jnp.pad(hi, pad), jnp.pad(lo, pad)], axis=1)


def _rope_tables(l):
    rows = l // GRID_W
    row = jnp.repeat(jnp.arange(rows, dtype=F32), GRID_W)
    col = jnp.tile(jnp.arange(GRID_W, dtype=F32), rows)
    nf = DK // 4
    inv = ROPE_BASE ** (-jnp.arange(nf, dtype=F32) / nf)
    ang = jnp.concatenate([row[:, None] * inv, col[:, None] * inv], axis=-1)
    cos = jnp.repeat(jnp.cos(ang), 2, axis=-1)
    sin = jnp.repeat(jnp.sin(ang), 2, axis=-1) * jnp.tile(jnp.asarray([-1.0, 1.0], F32), DK // 2)
    return cos, sin


def kernel(x_prompt, x_sample, c, state_ret, c_ctx, w_ada, b_ada, norm1_g, w_in, w_pool, pool_scale, w_pool_out,
           ret_decay, ret_gn_g, w_ret_out, w_branch_gate, b_branch_gate, w_out, norm2_g, w_router, w_exp_gate,
           w_exp_up, w_exp_down, final_norm_g):
    depth = w_ada.shape[0]
    assert depth == 1, "single trunk layer"
    bc, lc, d = x_prompt.shape
    bl, ll, _ = x_sample.shape
    assert lc % CHUNK == 0 and STEP_ROWS % lc == 0 and ll % STEP_ROWS == 0 and ll % GATHER_TOKENS == 0
    cap_c = max(1, CAPACITY_FACTOR * lc // N_EXPERTS)
    cap_l = max(1, CAPACITY_FACTOR * ll // N_EXPERTS)
    assert cap_l % GATHER_SLOTS == 0 and ll // GATHER_TOKENS < COUNT_LANES

    cond = jnp.zeros((8 * pl.cdiv(bl + 1, 8), d), F32).at[:bl].set(c).at[bl].set(c_ctx)
    ada = _adaln(cond, w_ada[0], b_ada[0]).reshape(-1, 6, d)
    ada = jnp.pad(ada, ((0, 0), (0, 2), (0, 0)))
    mod_l = ada[:bl]
    mod_c = jnp.broadcast_to(ada[bl:bl + 1], (bc, 8, d))

    p = dict(
        n1g=norm1_g[0][None], w_in=w_in[0].astype(BF16), w_bg=w_branch_gate[0].astype(BF16),
        b_bg=b_branch_gate[0][None], w_pool=w_pool[0].astype(BF16), pool_scale=pool_scale[0][None],
        w_pool_out=w_pool_out[0].astype(BF16), gn_g=ret_gn_g[0][None], w_ret_out=w_ret_out[0].astype(BF16),
        w_out=w_out[0].astype(BF16), n2g=norm2_g[0][None], w_router_parts=_router_parts(w_router[0]),
    )
    decay = ret_decay[0]
    fn_g = final_norm_g[None]

    x1_c, h2_c, aff_c, st_c = _mix(x_prompt, mod_c, None, None, decay, p)
    rope = _rope_tables(ll)
    sb, kv = _revscan(x_sample, mod_l, rope[0], rope[1], state_ret, decay, p["n1g"], p["w_in"])
    x1_l, h2_l, aff_l = _mix(x_sample, mod_l, rope, (state_ret, sb, kv), decay, p)

    wsel_c, pos_c, _ = _route(aff_c.reshape(bc * N_EXPERTS, lc), cap_c, min(ROUTE_ROWS, bc * N_EXPERTS))
    wsel_l, pos_l, cnt_l = _route(aff_l.reshape(bl * N_EXPERTS, ll), cap_l, min(ROUTE_ROWS, bl * N_EXPERTS))
    cnt_flat = cnt_l.reshape(-1)

    xs_c = _gather_small(h2_c, pos_c, cap_c)
    xs_l = _gather_big(h2_l, pos_l, cnt_flat, cap_l)
    y_c, y_l = _ffn(xs_c, xs_l, w_exp_gate[0], w_exp_up[0], w_exp_down[0])

    y_prompt = _combine_small(x1_c, y_c, pos_c, wsel_c, mod_c, fn_g, cap_c)
    y_sample = _combine_big(x1_l, y_l, pos_l, wsel_l, cnt_flat, mod_l, fn_g, cap_l)
    new_state = st_c[:, None].astype(x_prompt.dtype)
    return (y_prompt, y_sample, new_state)
```

```python
import functools

import jax
import jax.numpy as jnp
from jax import lax
from jax.experimental import pallas as pl
from jax.experimental.pallas import tpu as pltpu

F32 = jnp.float32
BF16 = jnp.bfloat16

N_HEADS = 4
DK = 128
DV = 256
POOL_WINDOWS = (2, 4, 8, 16)
POOL_GROUP = 128
POOL_WIDTH = POOL_GROUP * len(POOL_WINDOWS)
QK_WIDTH = N_HEADS * DK
V_WIDTH = N_HEADS * DV
N_EXPERTS = 16
CAPACITY_FACTOR = 2
GRID_W = 64
ROPE_BASE = 10000.0
EPS = 1e-6

LANES = 128
CHUNK = 256
STEP_ROWS = 512
HALO = 16
ROUTE_LANES = 256
ROUTE_ROWS = 512
SMALL_REQS = 4
GATHER_TOKENS = 256
GATHER_SLOTS = 64
GATHER_EXPERTS = 4
COMBINE_BLOCKS = 2
COUNT_LANES = 128
SLOT_ALIGN = 16
FF_TILE = 256
FFN_ROWS = 512
VMEM_LIMIT = 56 * 1024 * 1024


def _cparams(sem):
    return pltpu.CompilerParams(dimension_semantics=sem, vmem_limit_bytes=VMEM_LIMIT)


def _resident(shape):
    return pl.BlockSpec(shape, lambda *_: (0,) * len(shape), pipeline_mode=pl.Buffered(1))


def _sigmoid(x):
    return 0.5 * jnp.tanh(0.5 * x) + 0.5


def _silu(x):
    return x * _sigmoid(x)


def _norm_mod(x, g, scale, shift):
    y = x * lax.rsqrt(jnp.mean(x * x, axis=-1, keepdims=True) + EPS)
    return (y * g) * (1.0 + scale) + shift


def _hi_lo(x):
    hi = x.astype(BF16)
    return hi, (x - hi.astype(F32)).astype(BF16)


def _dot(a, b):
    return jnp.dot(a, b, preferred_element_type=F32)


def _dot_nt(a, b):
    return lax.dot_general(a, b, (((1,), (1,)), ((), ())), preferred_element_type=F32)


def _dot_tn(a, b):
    return lax.dot_general(a, b, (((0,), (0,)), ((), ())), preferred_element_type=F32)


def _adaln_kernel(c_ref, w_ref, b_ref, o_ref):
    rows = c_ref.shape[0]
    s_hi, s_lo = _hi_lo(_silu(c_ref[...]))
    w_hi, w_lo = _hi_lo(w_ref[...])
    by_hi = _dot(jnp.concatenate([s_hi, s_lo], axis=0), w_hi)
    o_ref[...] = by_hi[:rows] + by_hi[rows:] + _dot(s_hi, w_lo) + b_ref[...]


def _adaln(cond, w, b):
    rows, d = cond.shape
    n = w.shape[1]
    tn = d
    return pl.pallas_call(
        _adaln_kernel,
        grid=(n // tn,),
        in_specs=[pl.BlockSpec((rows, d), lambda i: (0, 0)),
                  pl.BlockSpec((d, tn), lambda i: (0, i)),
                  pl.BlockSpec((1, tn), lambda i: (0, i))],
        out_specs=pl.BlockSpec((rows, tn), lambda i: (0, i)),
        out_shape=jax.ShapeDtypeStruct((rows, n), F32),
        compiler_params=_cparams(("arbitrary",)),
        name="adaln",
    )(cond, w, b.reshape(1, n))


def _pool_fold_kernel(wg_ref, scale_ref, wo_ref, o_ref):
    a_hi, a_lo = _hi_lo(wg_ref[0] * scale_ref[...])
    b_hi, b_lo = _hi_lo(wo_ref[...])
    by_hi = _dot(jnp.concatenate([a_hi, a_lo], axis=0), b_hi)
    o_ref[...] = (by_hi[:POOL_GROUP] + by_hi[POOL_GROUP:] + _dot(a_hi, b_lo)).astype(BF16)


def _pool_fold(w_group, scale, w_pool_out):
    g, d = w_group.shape[0], w_pool_out.shape[1]
    return pl.pallas_call(
        _pool_fold_kernel,
        grid=(g,),
        in_specs=[pl.BlockSpec((1, POOL_GROUP, POOL_GROUP), lambda i: (i, 0, 0)),
                  pl.BlockSpec((1, POOL_GROUP), lambda i: (0, i)),
                  pl.BlockSpec((POOL_GROUP, d), lambda i: (i, 0))],
        out_specs=pl.BlockSpec((POOL_GROUP, d), lambda i: (i, 0)),
        out_shape=jax.ShapeDtypeStruct((g * POOL_GROUP, d), BF16),
        compiler_params=_cparams(("arbitrary",)),
        name="pool_fold",
    )(w_group, scale, w_pool_out)


def _decay_tiles(decay_ref, head):
    def log_gamma(direction, shape):
        return -jnp.exp(jnp.full(shape, decay_ref[direction, head], F32))

    diff = (lax.broadcasted_iota(jnp.int32, (CHUNK, CHUNK), 0)
            - lax.broadcasted_iota(jnp.int32, (CHUNK, CHUNK), 1)).astype(F32)
    i = lax.broadcasted_iota(jnp.int32, (CHUNK, DK), 0).astype(F32)
    lgf, lgb = log_gamma(0, (CHUNK, DK)), log_gamma(1, (CHUNK, DK))
    return dict(
        mask=jnp.where(diff >= 0.0, jnp.exp(log_gamma(0, (CHUNK, CHUNK)) * jnp.maximum(diff, 0.0)),
                       jnp.exp(log_gamma(1, (CHUNK, CHUNK)) * jnp.maximum(-diff, 0.0))),
        qdf=jnp.exp(lgf * (i + 1.0)),
        qdb=jnp.exp(lgb * (CHUNK - i)),
        kdf=jnp.exp(lgf * (CHUNK - 1.0 - i)),
        kdb=jnp.exp(lgb * i),
        cdf=jnp.exp(log_gamma(0, (DK, DV)) * float(CHUNK)),
        cdb=jnp.exp(log_gamma(1, (DK, DV)) * float(CHUNK)),
    )


def _rope(x, cos, sin):
    even = (lax.broadcasted_iota(jnp.int32, x.shape, 1) % 2) == 0
    partner = jnp.where(even, pltpu.roll(x, x.shape[1] - 1, 1), pltpu.roll(x, 1, 1))
    return x * cos + partner * sin


def _state_spec(direction):
    return pl.BlockSpec((1, 1, 1, N_HEADS, DK, DV), lambda bi, j: (bi, 0, direction, 0, 0, 0))


def _revscan_kernel(x_ref, mod_ref, cos_ref, sin_ref, s0_ref, decay_ref, n1_ref, w_in_ref, sb_ref, kv_ref, s_scr):
    j = pl.program_id(1)

    @pl.when(j == 0)
    def _():
        s_scr[...] = s0_ref[0, 0, 0]

    mod = mod_ref[0]
    h = _norm_mod(x_ref[0], n1_ref[...], mod[1:2], mod[0:1]).astype(BF16)
    k_lo = POOL_WIDTH + QK_WIDTH
    kv = _dot(h, w_in_ref[:, k_lo:k_lo + QK_WIDTH + V_WIDTH])
    sb_ref[0, 0] = s_scr[...]
    kv_ref[0, :, QK_WIDTH:] = kv[:, QK_WIDTH:].astype(BF16)
    for head in range(N_HEADS):
        t = _decay_tiles(decay_ref, head)
        s = s_scr[head]
        for c in reversed(range(STEP_ROWS // CHUNK)):
            rows = slice(c * CHUNK, (c + 1) * CHUNK)
            cols = slice(head * DK, (head + 1) * DK)
            k = _rope(kv[rows, cols] * (DK ** -0.5), cos_ref[rows], sin_ref[rows])
            kv_ref[0, rows, cols] = k.astype(BF16)
            v = kv[rows, QK_WIDTH + head * DV:QK_WIDTH + (head + 1) * DV]
            s = t["cdb"] * s + _dot_tn((k * t["kdb"]).astype(BF16), v.astype(BF16))
        s_scr[head] = s


def _revscan(x, mod, cos, sin, state_ret, decay, n1g, w_in):
    b, l, d = x.shape
    nblk = l // STEP_ROWS
    return pl.pallas_call(
        _revscan_kernel,
        grid=(b, nblk),
        in_specs=[pl.BlockSpec((1, STEP_ROWS, d), lambda bi, j: (bi, nblk - 1 - j, 0)),
                  pl.BlockSpec((1, 8, d), lambda bi, j: (bi, 0, 0)),
                  pl.BlockSpec((STEP_ROWS, DK), lambda bi, j: (nblk - 1 - j, 0)),
                  pl.BlockSpec((STEP_ROWS, DK), lambda bi, j: (nblk - 1 - j, 0)),
                  _state_spec(1),
                  pl.BlockSpec(memory_space=pltpu.SMEM),
                  _resident(n1g.shape),
                  _resident(w_in.shape)],
        out_specs=[pl.BlockSpec((1, 1, N_HEADS, DK, DV), lambda bi, j: (bi, nblk - 1 - j, 0, 0, 0)),
                   pl.BlockSpec((1, STEP_ROWS, QK_WIDTH + V_WIDTH), lambda bi, j: (bi, nblk - 1 - j, 0))],
        out_shape=[jax.ShapeDtypeStruct((b, nblk, N_HEADS, DK, DV), F32),
                   jax.ShapeDtypeStruct((b, l, QK_WIDTH + V_WIDTH), BF16)],
        scratch_shapes=[pltpu.VMEM((N_HEADS, DK, DV), F32)],
        compiler_params=_cparams(("arbitrary", "arbitrary")),
        name="revscan",
    )(x, mod, cos, sin, state_ret, decay, n1g, w_in)


def _mix_kernel(*refs, seq_len, n_seq, seq_rows, use_rope, has_state, emit_state):
    it = iter(refs)
    x_ref, xp_ref, xn_ref, mod_ref = next(it), next(it), next(it), next(it)
    cos_ref = sin_ref = s0f_ref = sb_ref = kv_ref = None
    if use_rope:
        cos_ref, sin_ref = next(it), next(it)
    if has_state:
        s0f_ref, sb_ref, kv_ref = next(it), next(it), next(it)
    (decay_ref, n1_ref, w_in_ref, w_bg_ref, b_bg_ref, w_pf_ref, gn_ref, w_ro_ref,
     w_out_ref, n2_ref, w_rt_ref) = (next(it) for _ in range(11))
    x1_ref, h2_ref, aff_ref = next(it), next(it), next(it)
    st_ref = next(it) if emit_state else None
    sf_scr = next(it)

    j = pl.program_id(1)
    d = x_ref.shape[2]
    n_chunks = seq_rows // CHUNK
    ext = seq_rows + 2 * HALO

    @pl.when(j == 0)
    def _():
        if has_state:
            sf_scr[...] = s0f_ref[:, 0, 0]
        else:
            sf_scr[...] = jnp.zeros(sf_scr.shape, F32)

    mods = [mod_ref[s] for s in range(n_seq)]

    def rows_of(vals):
        return jnp.concatenate([jnp.broadcast_to(v, (seq_rows, d)) for v in vals], axis=0) if n_seq > 1 else vals[0]

    he_parts = []
    for s in range(n_seq):
        xe = jnp.concatenate([xp_ref[s], x_ref[s], xn_ref[s]], axis=0)
        he_parts.append(_norm_mod(xe, n1_ref[...], mods[s][1:2], mods[s][0:1]).astype(BF16))
    he = jnp.concatenate(he_parts, axis=0) if n_seq > 1 else he_parts[0]
    hb_parts = [hp[HALO:HALO + seq_rows] for hp in he_parts]
    hb = jnp.concatenate(hb_parts, axis=0) if n_seq > 1 else hb_parts[0]
    x = jnp.concatenate([x_ref[s] for s in range(n_seq)], axis=0) if n_seq > 1 else x_ref[0]
    gates = _sigmoid(_dot(hb, w_bg_ref[...]) + b_bg_ref[...])

    ue_all = _dot(he, w_in_ref[:, :POOL_WIDTH])
    epos = j * seq_rows - HALO + lax.broadcasted_iota(jnp.int32, (ext, 1), 0)
    valid = (epos >= 0) & (epos < seq_len)
    tpos = j * seq_rows + lax.broadcasted_iota(jnp.int32, (seq_rows, 1), 0)
    pooled = [[] for _ in POOL_WINDOWS]
    for s in range(n_seq):
        ue = jnp.where(valid, ue_all[s * ext:(s + 1) * ext], 0.0)
        for gi, w in enumerate(POOL_WINDOWS):
            ug = ue[:, gi * POOL_GROUP:(gi + 1) * POOL_GROUP]
            acc, shift = ug, 1
            while shift < w:
                acc = acc + pltpu.roll(acc, shift, 0)
                shift *= 2
            if w // 2 > 1:
                acc = pltpu.roll(acc, ext - (w // 2 - 1), 0)
            cnt = (jnp.minimum(tpos + w // 2, seq_len) - jnp.maximum(tpos - w // 2, 0)).astype(F32)
            own = slice(HALO, HALO + seq_rows)
            pooled[gi].append((acc[own] / cnt - ug[own]).astype(BF16))
    pooled_all = jnp.concatenate([jnp.concatenate(pg, axis=0) if len(pg) > 1 else pg[0] for pg in pooled], axis=1)
    pool_y = _dot(pooled_all, w_pf_ref[...])

    g_lo = POOL_WIDTH + 2 * QK_WIDTH + V_WIDTH
    if has_state:
        zq = _dot(hb, w_in_ref[:, POOL_WIDTH:POOL_WIDTH + QK_WIDTH])
        g_all = _dot(hb, w_in_ref[:, g_lo:])
    else:
        zr = _dot(hb, w_in_ref[:, POOL_WIDTH:])
        zq, g_all = zr[:, :QK_WIDTH], zr[:, 2 * QK_WIDTH + V_WIDTH:]
    seq_out = []
    for s in range(n_seq):
        r0 = s * seq_rows
        head_out = []
        for head in range(N_HEADS):
            t = _decay_tiles(decay_ref, head)
            qs, ks, vs = [], [], []
            for c in range(n_chunks):
                rows = slice(r0 + c * CHUNK, r0 + (c + 1) * CHUNK)
                crow = slice(c * CHUNK, (c + 1) * CHUNK)
                q = zq[rows, head * DK:(head + 1) * DK]
                if use_rope:
                    q = _rope(q, cos_ref[crow], sin_ref[crow])
                if has_state:
                    k = kv_ref[s, crow, head * DK:(head + 1) * DK].astype(F32)
                    v = kv_ref[s, crow, QK_WIDTH + head * DV:QK_WIDTH + (head + 1) * DV]
                else:
                    k = zr[rows, QK_WIDTH + head * DK:QK_WIDTH + (head + 1) * DK] * (DK ** -0.5)
                    if use_rope:
                        k = _rope(k, cos_ref[crow], sin_ref[crow])
                    v = zr[rows, 2 * QK_WIDTH + head * DV:2 * QK_WIDTH + (head + 1) * DV].astype(BF16)
                qs.append(q)
                ks.append(k)
                vs.append(v)
            sf = [sf_scr[s, head]]
            for c in range(n_chunks):
                sf.append(t["cdf"] * sf[c] + _dot_tn((ks[c] * t["kdf"]).astype(BF16), vs[c]))
            sf_scr[s, head] = sf[n_chunks]
            sb = [None] * (n_chunks + 1)
            sb[n_chunks] = sb_ref[s, 0, head] if has_state else jnp.zeros((DK, DV), F32)
            for c in reversed(range(n_chunks)):
                sb[c] = t["cdb"] * sb[c + 1] + _dot_tn((ks[c] * t["kdb"]).astype(BF16), vs[c])
            if emit_state:
                st_ref[s, 0, head] = sf[n_chunks]
                st_ref[s, 1, head] = sb[0]
            outs = []
            for c in range(n_chunks):
                scores = _dot_nt(qs[c].astype(BF16), ks[c].astype(BF16)) * t["mask"]
                q_both = jnp.concatenate([(qs[c] * t["qdf"]).astype(BF16), (qs[c] * t["qdb"]).astype(BF16)], axis=1)
                s_both = jnp.concatenate([sf[c].astype(BF16), sb[c + 1].astype(BF16)], axis=0)
                outs.append(_dot(scores.astype(BF16), vs[c]) + _dot(q_both, s_both))
            o = jnp.concatenate(outs, axis=0)
            mu = jnp.mean(o, axis=-1, keepdims=True)
            oc = o - mu
            var = jnp.mean(oc * oc, axis=-1, keepdims=True)
            head_out.append(oc * lax.rsqrt(var + EPS))
        seq_out.append(jnp.concatenate(head_out, axis=1))
    o_n = (jnp.concatenate(seq_out, axis=0) if n_seq > 1 else seq_out[0]) * gn_ref[...]
    ret_y = _dot((_silu(g_all) * o_n).astype(BF16), w_ro_ref[...])

    merged = gates[:, :d] * pool_y + gates[:, d:] * ret_y
    x1 = x + rows_of([m[2:3] for m in mods]) * _dot(merged.astype(BF16), w_out_ref[...])

    h2 = _norm_mod(x1, n2_ref[...], rows_of([m[4:5] for m in mods]), rows_of([m[3:4] for m in mods]))
    h2_hi, h2_lo = _hi_lo(h2)
    n_rows = n_seq * seq_rows
    parts = _dot(jnp.concatenate([h2_hi, h2_lo], axis=0), w_rt_ref[...])
    logits_t = (parts[:n_rows, :LANES] + parts[n_rows:, :LANES] + parts[:n_rows, LANES:])
    for s in range(n_seq):
        rows = slice(s * seq_rows, (s + 1) * seq_rows)
        x1_ref[s] = x1[rows]
        h2_ref[s] = h2_hi[rows]
        logits = logits_t[rows].T[:N_EXPERTS]
        e = jnp.exp(logits - jnp.max(logits, axis=0, keepdims=True))
        aff_ref[s] = e / jnp.sum(e, axis=0, keepdims=True)


def _mix(x, mod, rope, states, decay, p):
    b, l, d = x.shape
    seq_rows = min(l, STEP_ROWS)
    n_seq = STEP_ROWS // seq_rows
    nblk = l // seq_rows
    hb = seq_rows // HALO
    n_halo = l // HALO
    use_rope, has_state = rope is not None, states is not None
    emit_state = not has_state
    assert b % n_seq == 0 and (n_seq == 1 or not (use_rope or has_state))
    in_specs = [pl.BlockSpec((n_seq, seq_rows, d), lambda bi, j: (bi, j, 0)),
                pl.BlockSpec((n_seq, HALO, d), lambda bi, j: (bi, jnp.maximum(j * hb - 1, 0), 0)),
                pl.BlockSpec((n_seq, HALO, d), lambda bi, j: (bi, jnp.minimum((j + 1) * hb, n_halo - 1), 0)),
                pl.BlockSpec((n_seq, 8, d), lambda bi, j: (bi, 0, 0))]
    args = [x, x, x, mod]
    if use_rope:
        in_specs += [pl.BlockSpec((seq_rows, DK), lambda bi, j: (j, 0))] * 2
        args += list(rope)
    if has_state:
        in_specs += [_state_spec(0),
                     pl.BlockSpec((1, 1, N_HEADS, DK, DV), lambda bi, j: (bi, j, 0, 0, 0)),
                     pl.BlockSpec((1, seq_rows, QK_WIDTH + V_WIDTH), lambda bi, j: (bi, j, 0))]
        args += list(states)
    weights = [p["n1g"], p["w_in"], p["w_bg"], p["b_bg"], p["w_pool_fold"],
               p["gn_g"], p["w_ret_out"], p["w_out"], p["n2g"], p["w_router_parts"]]
    in_specs.append(pl.BlockSpec(memory_space=pltpu.SMEM))
    args.append(decay)
    for w in weights:
        in_specs.append(_resident(w.shape))
        args.append(w)
    out_specs = [pl.BlockSpec((n_seq, seq_rows, d), lambda bi, j: (bi, j, 0)),
                 pl.BlockSpec((n_seq, seq_rows, d), lambda bi, j: (bi, j, 0)),
                 pl.BlockSpec((n_seq, N_EXPERTS, seq_rows), lambda bi, j: (bi, 0, j))]
    out_shape = [jax.ShapeDtypeStruct((b, l, d), F32), jax.ShapeDtypeStruct((b, l, d), BF16),
                 jax.ShapeDtypeStruct((b, N_EXPERTS, l), F32)]
    if emit_state:
        out_specs.append(pl.BlockSpec((n_seq, 2, N_HEADS, DK, DV), lambda bi, j: (bi, 0, 0, 0, 0)))
        out_shape.append(jax.ShapeDtypeStruct((b, 2, N_HEADS, DK, DV), F32))
    kern = functools.partial(_mix_kernel, seq_len=l, n_seq=n_seq, seq_rows=seq_rows, use_rope=use_rope,
                             has_state=has_state, emit_state=emit_state)
    return pl.pallas_call(
        kern,
        grid=(b // n_seq, nblk),
        in_specs=in_specs,
        out_specs=out_specs,
        out_shape=out_shape,
        scratch_shapes=[pltpu.VMEM((n_seq, N_HEADS, DK, DV), F32)],
        compiler_params=_cparams(("arbitrary", "arbitrary")),
        name="mix_rope" if use_rope else "mix",
    )(*args)


def _route_kernel(aff_ref, wsel_ref, pos_ref, cnt_ref, *, cap):
    aff = aff_ref[...]
    rows, l = aff.shape

    def as_value(bits):
        return pltpu.bitcast(bits, F32)

    def bisect(i, tau):
        cand = tau | jnp.left_shift(jnp.int32(1), 30 - i)
        cnt = jnp.sum((aff >= as_value(cand)).astype(F32), axis=1, keepdims=True)
        return jnp.where(cnt >= cap, cand, tau)

    tau = lax.fori_loop(0, 31, bisect, jnp.zeros((rows, 1), jnp.int32))
    gt = aff >= as_value(tau + 1)
    eq = (aff >= as_value(tau)) & jnp.logical_not(gt)
    need = cap - jnp.sum(gt.astype(F32), axis=1, keepdims=True)

    nlb = l // ROUTE_LANES
    tri = (lax.broadcasted_iota(jnp.int32, (ROUTE_LANES, ROUTE_LANES), 0)
           < lax.broadcasted_iota(jnp.int32, (ROUTE_LANES, ROUTE_LANES), 1)).astype(BF16)

    def prefix(mask):
        carry = jnp.zeros((rows, 1), F32)
        parts, starts = [], []
        for blk in range(nlb):
            mb = mask[:, blk * ROUTE_LANES:(blk + 1) * ROUTE_LANES].astype(F32)
            starts.append(carry)
            parts.append(_dot(mb.astype(BF16), tri) + carry)
            carry = carry + jnp.sum(mb, axis=1, keepdims=True)
        starts.append(carry)
        return jnp.concatenate(parts, axis=1), starts

    eq_rank, _ = prefix(eq)
    sel = gt | (eq & (eq_rank < need))
    pos, starts = prefix(sel)
    wsel_ref[...] = jnp.where(sel, aff, 0.0)
    pos_ref[...] = jnp.where(sel, pos.astype(jnp.int32), -1)
    lane = lax.broadcasted_iota(jnp.int32, (rows, COUNT_LANES), 1)
    table = jnp.zeros((rows, COUNT_LANES), jnp.int32)
    per = max(GATHER_TOKENS // ROUTE_LANES, 1)
    for tb in range(nlb // per + 1):
        table = jnp.where(lane == tb, starts[min(tb * per, nlb)].astype(jnp.int32), table)
    cnt_ref[...] = table


def _route(aff_rows, cap, row_block):
    r, l = aff_rows.shape
    spec = pl.BlockSpec((row_block, l), lambda i: (i, 0))
    return pl.pallas_call(
        functools.partial(_route_kernel, cap=cap),
        grid=(r // row_block,),
        in_specs=[spec],
        out_specs=[spec, spec, pl.BlockSpec((row_block, COUNT_LANES), lambda i: (i, 0))],
        out_shape=[jax.ShapeDtypeStruct((r, l), F32), jax.ShapeDtypeStruct((r, l), jnp.int32),
                   jax.ShapeDtypeStruct((r, COUNT_LANES), jnp.int32)],
        compiler_params=_cparams(("arbitrary",)),
        name=f"route_{l}",
    )(aff_rows)


def _stacked_hits(pos_ref, req, cap):
    l = pos_ref.shape[1]
    slot = lax.broadcasted_iota(jnp.int32, (cap, l), 0)
    return [pos_ref[req * N_EXPERTS + e:req * N_EXPERTS + e + 1, :] == slot for e in range(N_EXPERTS)]


def _gather_small_kernel(h_ref, pos_ref, xs_ref, *, cap):
    for req in range(h_ref.shape[0]):
        onehot = jnp.concatenate([h.astype(BF16) for h in _stacked_hits(pos_ref, req, cap)], axis=0)
        xs = _dot(onehot, h_ref[req]).astype(BF16)
        for e in range(N_EXPERTS):
            xs_ref[e, req * cap:(req + 1) * cap, :] = xs[e * cap:(e + 1) * cap]


def _gather_small(h2, pos, cap):
    b, l, d = h2.shape
    nr = SMALL_REQS if b % SMALL_REQS == 0 else 1
    return pl.pallas_call(
        functools.partial(_gather_small_kernel, cap=cap),
        grid=(b // nr,),
        in_specs=[pl.BlockSpec((nr, l, d), lambda bi: (bi, 0, 0)),
                  pl.BlockSpec((nr * N_EXPERTS, l), lambda bi: (bi, 0))],
        out_specs=pl.BlockSpec((N_EXPERTS, nr * cap, d), lambda bi: (0, bi, 0)),
        out_shape=jax.ShapeDtypeStruct((N_EXPERTS, b * cap, d), BF16),
        compiler_params=_cparams(("arbitrary",)),
        name="gather_small",
    )(h2, pos)


def _slot_windows(cnt_ref, row, tb, cap):
    lo = cnt_ref[row * COUNT_LANES + tb]
    hi = cnt_ref[row * COUNT_LANES + tb + 1]
    start = jnp.minimum(lo & jnp.int32(-SLOT_ALIGN), jnp.int32(cap - GATHER_SLOTS))
    n_win = lax.shift_right_logical(hi - start + jnp.int32(GATHER_SLOTS - 1),
                                    jnp.int32(GATHER_SLOTS.bit_length() - 1))
    return pl.multiple_of(start, SLOT_ALIGN), n_win


def _next_window(start, k, cap):
    first = start + k * GATHER_SLOTS
    return pl.multiple_of(jnp.minimum(first, cap - GATHER_SLOTS), SLOT_ALIGN), first


def _window_hits(pos_row, start, first_slot=None):
    slot = start + lax.broadcasted_iota(jnp.int32, (GATHER_SLOTS, pos_row.shape[1]), 0)
    hit = pos_row == slot
    return hit if first_slot is None else hit & (slot >= first_slot)


def _gather_big_kernel(cnt_ref, h_ref, pos_ref, xs_ref, *, cap):
    bi, eg = pl.program_id(0), pl.program_id(1)
    n_tb = h_ref.shape[1] // GATHER_TOKENS
    xs_ref[...] = jnp.zeros(xs_ref.shape, BF16)

    def onehot(el, tb, start, first_slot=None):
        toks = slice(tb * GATHER_TOKENS, (tb + 1) * GATHER_TOKENS)
        return _window_hits(pos_ref[pl.ds(eg * GATHER_EXPERTS + el, 1), toks], start, first_slot).astype(BF16)

    windows = {(tb, el): _slot_windows(cnt_ref, bi * N_EXPERTS + eg * GATHER_EXPERTS + el, tb, cap)
               for tb in range(n_tb) for el in range(GATHER_EXPERTS)}
    for tb in range(n_tb):
        toks = slice(tb * GATHER_TOKENS, (tb + 1) * GATHER_TOKENS)
        stacked = jnp.concatenate([onehot(el, tb, windows[tb, el][0]) for el in range(GATHER_EXPERTS)], axis=0)
        rows = _dot(stacked, h_ref[0, toks, :]).astype(BF16)
        for el in range(GATHER_EXPERTS):
            xs_ref[el, pl.ds(windows[tb, el][0], GATHER_SLOTS), :] += rows[el * GATHER_SLOTS:(el + 1) * GATHER_SLOTS]
    for (tb, el), (start, n_win) in windows.items():
        def extra(k, carry, tb=tb, el=el, start=start):
            clamped, first = _next_window(start, k, cap)
            toks = slice(tb * GATHER_TOKENS, (tb + 1) * GATHER_TOKENS)
            xs_ref[el, pl.ds(clamped, GATHER_SLOTS), :] += _dot(onehot(el, tb, clamped, first),
                                                                h_ref[0, toks, :]).astype(BF16)
            return carry
        lax.fori_loop(1, n_win, extra, 0)


def _gather_big(h2, pos, cnt_flat, cap):
    b, l, d = h2.shape
    grid_spec = pltpu.PrefetchScalarGridSpec(
        num_scalar_prefetch=1,
        grid=(b, N_EXPERTS // GATHER_EXPERTS),
        in_specs=[pl.BlockSpec((1, l, d), lambda bi, eg, c: (bi, 0, 0)),
                  pl.BlockSpec((N_EXPERTS, l), lambda bi, eg, c: (bi, 0))],
        out_specs=pl.BlockSpec((GATHER_EXPERTS, cap, d), lambda bi, eg, c: (eg, bi, 0)),
    )
    return pl.pallas_call(
        functools.partial(_gather_big_kernel, cap=cap),
        grid_spec=grid_spec,
        out_shape=jax.ShapeDtypeStruct((N_EXPERTS, b * cap, d), BF16),
        compiler_params=_cparams(("arbitrary", "arbitrary")),
        name="gather_big",
    )(cnt_flat, h2, pos)


def _ffn_kernel(xa_ref, xb_ref, wg_ref, wu_ref, wd_ref, ya_ref, yb_ref, acca_ref, accb_ref):
    f, nf = pl.program_id(1), pl.num_programs(1)

    def ff_tile(first, last):
        wg = wg_ref[0].astype(BF16)
        wu = wu_ref[0].astype(BF16)
        wd = wd_ref[0].astype(BF16)
        for x_ref, y_ref, acc_ref in ((xa_ref, ya_ref, acca_ref), (xb_ref, yb_ref, accb_ref)):
            m = x_ref.shape[1]
            step = min(FFN_ROWS, m)
            for r0 in range(0, m, step):
                rows = slice(r0, r0 + step)
                x = x_ref[0, rows, :]
                hid = (_silu(_dot(x, wg)) * _dot(x, wu)).astype(BF16)
                part = _dot(hid, wd)
                if first:
                    acc_ref[rows, :] = part
                elif last:
                    y_ref[0, rows, :] = (acc_ref[rows, :] + part).astype(BF16)
                else:
                    acc_ref[rows, :] += part

    pl.when(f == 0)(functools.partial(ff_tile, True, False))
    pl.when((f > 0) & (f < nf - 1))(functools.partial(ff_tile, False, False))
    pl.when(f == nf - 1)(functools.partial(ff_tile, False, True))


def _ffn(xs_a, xs_b, w_gate, w_up, w_down):
    e, ma, d = xs_a.shape
    mb = xs_b.shape[1]
    ff = w_gate.shape[2]
    nf = ff // FF_TILE
    assert ff % FF_TILE == 0 and nf >= 2
    return pl.pallas_call(
        _ffn_kernel,
        grid=(e, nf),
        in_specs=[pl.BlockSpec((1, ma, d), lambda ei, f: (ei, 0, 0)),
                  pl.BlockSpec((1, mb, d), lambda ei, f: (ei, 0, 0)),
                  pl.BlockSpec((1, d, FF_TILE), lambda ei, f: (ei, 0, f)),
                  pl.BlockSpec((1, d, FF_TILE), lambda ei, f: (ei, 0, f)),
                  pl.BlockSpec((1, FF_TILE, d), lambda ei, f: (ei, f, 0))],
        out_specs=[pl.BlockSpec((1, ma, d), lambda ei, f: (ei, 0, 0)),
                   pl.BlockSpec((1, mb, d), lambda ei, f: (ei, 0, 0))],
        out_shape=[jax.ShapeDtypeStruct((e, ma, d), BF16), jax.ShapeDtypeStruct((e, mb, d), BF16)],
        scratch_shapes=[pltpu.VMEM((ma, d), F32), pltpu.VMEM((mb, d), F32)],
        compiler_params=_cparams(("arbitrary", "arbitrary")),
        name="ffn",
    )(xs_a, xs_b, w_gate, w_up, w_down)


def _slot_gate(hit, w_row):
    return jnp.sum(jnp.where(hit, w_row, 0.0), axis=1, keepdims=True)


def _finish(x1, moe, gate2, fn_g):
    x = x1 + gate2 * moe
    return (x * lax.rsqrt(jnp.mean(x * x, axis=-1, keepdims=True) + EPS)) * fn_g


def _combine_small_kernel(x1_ref, y_ref, pos_ref, w_ref, mod_ref, fn_ref, o_ref, *, cap):
    for req in range(x1_ref.shape[0]):
        hits = _stacked_hits(pos_ref, req, cap)
        gated = [(y_ref[e, req * cap:(req + 1) * cap, :].astype(F32)
                  * _slot_gate(hits[e], w_ref[req * N_EXPERTS + e:req * N_EXPERTS + e + 1, :])).astype(BF16)
                 for e in range(N_EXPERTS)]
        onehot = jnp.concatenate([h.astype(BF16) for h in hits], axis=0)
        moe = _dot_tn(onehot, jnp.concatenate(gated, axis=0))
        o_ref[req] = _finish(x1_ref[req], moe, mod_ref[req, 5:6], fn_ref[...])


def _combine_small(x1, y, pos, wsel, mod, fn_g, cap):
    b, l, d = x1.shape
    nr = SMALL_REQS if b % SMALL_REQS == 0 else 1
    return pl.pallas_call(
        functools.partial(_combine_small_kernel, cap=cap),
        grid=(b // nr,),
        in_specs=[pl.BlockSpec((nr, l, d), lambda bi: (bi, 0, 0)),
                  pl.BlockSpec((N_EXPERTS, nr * cap, d), lambda bi: (0, bi, 0)),
                  pl.BlockSpec((nr * N_EXPERTS, l), lambda bi: (bi, 0)),
                  pl.BlockSpec((nr * N_EXPERTS, l), lambda bi: (bi, 0)),
                  pl.BlockSpec((nr, 8, d), lambda bi: (bi, 0, 0)),
                  pl.BlockSpec((1, d), lambda bi: (0, 0))],
        out_specs=pl.BlockSpec((nr, l, d), lambda bi: (bi, 0, 0)),
        out_shape=jax.ShapeDtypeStruct((b, l, d), F32),
        compiler_params=_cparams(("arbitrary",)),
        name="combine_small",
    )(x1, y, pos, wsel, mod, fn_g)


def _combine_big_kernel(cnt_ref, x1_ref, y_ref, pos_ref, w_ref, mod_ref, fn_ref, o_ref, acc_ref, *, cap):
    bi, step = pl.program_id(0), pl.program_id(1)

    def window_terms(e, toks, start, first_slot=None):
        hit = _window_hits(pos_ref[e:e + 1, toks], start, first_slot)
        y = y_ref[e, pl.ds(start, GATHER_SLOTS), :].astype(F32)
        return hit.astype(BF16), (y * _slot_gate(hit, w_ref[e:e + 1, toks])).astype(BF16)

    def finish(toks, moe):
        o_ref[0, toks, :] = _finish(x1_ref[0, toks, :], moe, mod_ref[0, 5:6], fn_ref[...])

    blocks, overfull = [], False
    for sub in range(COMBINE_BLOCKS):
        toks = slice(sub * GATHER_TOKENS, (sub + 1) * GATHER_TOKENS)
        tb = step * COMBINE_BLOCKS + sub
        windows = [_slot_windows(cnt_ref, bi * N_EXPERTS + e, tb, cap) for e in range(N_EXPERTS)]
        terms = [window_terms(e, toks, start) for e, (start, _) in enumerate(windows)]
        moe = _dot_tn(jnp.concatenate([t[0] for t in terms], axis=0), jnp.concatenate([t[1] for t in terms], axis=0))
        acc_ref[toks, :] = moe
        finish(toks, moe)
        blocks.append((toks, windows))
        for _, n_win in windows:
            overfull = overfull | (n_win > 1)

    @pl.when(overfull)
    def _():
        for toks, windows in blocks:
            for e, (start, n_win) in enumerate(windows):
                def extra(k, carry, e=e, toks=toks, start=start):
                    hit, gated = window_terms(e, toks, *_next_window(start, k, cap))
                    acc_ref[toks, :] += _dot_tn(hit, gated)
                    return carry
                lax.fori_loop(1, n_win, extra, 0)
            finish(toks, acc_ref[toks, :])


def _combine_big(x1, y, pos, wsel, cnt_flat, mod, fn_g, cap):
    b, l, d = x1.shape
    span = COMBINE_BLOCKS * GATHER_TOKENS
    assert l % span == 0
    grid_spec = pltpu.PrefetchScalarGridSpec(
        num_scalar_prefetch=1,
        grid=(b, l // span),
        in_specs=[pl.BlockSpec((1, span, d), lambda bi, tb, c: (bi, tb, 0)),
                  pl.BlockSpec((N_EXPERTS, cap, d), lambda bi, tb, c: (0, bi, 0)),
                  pl.BlockSpec((N_EXPERTS, span), lambda bi, tb, c: (bi, tb)),
                  pl.BlockSpec((N_EXPERTS, span), lambda bi, tb, c: (bi, tb)),
                  pl.BlockSpec((1, 8, d), lambda bi, tb, c: (bi, 0, 0)),
                  pl.BlockSpec((1, d), lambda bi, tb, c: (0, 0))],
        out_specs=pl.BlockSpec((1, span, d), lambda bi, tb, c: (bi, tb, 0)),
        scratch_shapes=[pltpu.VMEM((span, d), F32)],
    )
    return pl.pallas_call(
        functools.partial(_combine_big_kernel, cap=cap),
        grid_spec=grid_spec,
        out_shape=jax.ShapeDtypeStruct((b, l, d), F32),
        compiler_params=_cparams(("arbitrary", "arbitrary")),
        name="combine_big",
    )(cnt_flat, x1, y, pos, wsel, mod, fn_g)


def _router_parts(w):
    hi, lo = _hi_lo(w)
    pad = ((0, 0), (0, LANES - w.shape[1]))
    return jnp.concatenate([jnp.pad(hi, pad), jnp.pad(lo, pad)], axis=1)


def _rope_tables(l):
    rows = l // GRID_W
    row = jnp.repeat(jnp.arange(rows, dtype=F32), GRID_W)
    col = jnp.tile(jnp.arange(GRID_W, dtype=F32), rows)
    nf = DK // 4
    inv = ROPE_BASE ** (-jnp.arange(nf, dtype=F32) / nf)
    ang = jnp.concatenate([row[:, None] * inv, col[:, None] * inv], axis=-1)
    cos = jnp.repeat(jnp.cos(ang), 2, axis=-1)
    sin = jnp.repeat(jnp.sin(ang), 2, axis=-1) * jnp.tile(jnp.asarray([-1.0, 1.0], F32), DK // 2)
    return cos, sin


def kernel(x_prompt, x_sample, c, state_ret, c_ctx, w_ada, b_ada, norm1_g, w_in, w_pool, pool_scale, w_pool_out,
           ret_decay, ret_gn_g, w_ret_out, w_branch_gate, b_branch_gate, w_out, norm2_g, w_router, w_exp_gate,
           w_exp_up, w_exp_down, final_norm_g):
    depth = w_ada.shape[0]
    assert depth == 1, "single trunk layer"
    bc, lc, d = x_prompt.shape
    bl, ll, _ = x_sample.shape
    assert lc % CHUNK == 0 and STEP_ROWS % lc == 0 and ll % STEP_ROWS == 0 and ll % GATHER_TOKENS == 0
    cap_c = max(1, CAPACITY_FACTOR * lc // N_EXPERTS)
    cap_l = max(1, CAPACITY_FACTOR * ll // N_EXPERTS)
    assert cap_l % GATHER_SLOTS == 0 and ll // GATHER_TOKENS < COUNT_LANES

    cond = jnp.zeros((8 * pl.cdiv(bl + 1, 8), d), F32).at[:bl].set(c).at[bl].set(c_ctx)
    ada = _adaln(cond, w_ada[0], b_ada[0]).reshape(-1, 6, d)
    ada = jnp.pad(ada, ((0, 0), (0, 2), (0, 0)))
    mod_l = ada[:bl]
    mod_c = jnp.broadcast_to(ada[bl:bl + 1], (bc, 8, d))

    p = dict(
        n1g=norm1_g[0][None], w_in=w_in[0].astype(BF16), w_bg=w_branch_gate[0].astype(BF16),
        b_bg=b_branch_gate[0][None], w_pool_fold=_pool_fold(w_pool[0], pool_scale[0][None], w_pool_out[0]),
        gn_g=ret_gn_g[0][None], w_ret_out=w_ret_out[0].astype(BF16),
        w_out=w_out[0].astype(BF16), n2g=norm2_g[0][None], w_router_parts=_router_parts(w_router[0]),
    )
    decay = ret_decay[0]
    fn_g = final_norm_g[None]

    x1_c, h2_c, aff_c, st_c = _mix(x_prompt, mod_c, None, None, decay, p)
    rope = _rope_tables(ll)
    sb, kv = _revscan(x_sample, mod_l, rope[0], rope[1], state_ret, decay, p["n1g"], p["w_in"])
    x1_l, h2_l, aff_l = _mix(x_sample, mod_l, rope, (state_ret, sb, kv), decay, p)

    wsel_c, pos_c, _ = _route(aff_c.reshape(bc * N_EXPERTS, lc), cap_c, min(ROUTE_ROWS, bc * N_EXPERTS))
    wsel_l, pos_l, cnt_l = _route(aff_l.reshape(bl * N_EXPERTS, ll), cap_l, min(ROUTE_ROWS, bl * N_EXPERTS))
    cnt_flat = cnt_l.reshape(-1)

    xs_c = _gather_small(h2_c, pos_c, cap_c)
    xs_l = _gather_big(h2_l, pos_l, cnt_flat, cap_l)
    y_c, y_l = _ffn(xs_c, xs_l, w_exp_gate[0], w_exp_up[0], w_exp_down[0])

    y_prompt = _combine_small(x1_c, y_c, pos_c, wsel_c, mod_c, fn_g, cap_c)
    y_sample = _combine_big(x1_l, y_l, pos_l, wsel_l, cnt_flat, mod_l, fn_g, cap_l)
    new_state = st_c[:, None].astype(x_prompt.dtype)
    return (y_prompt, y_sample, new_state)
```
